```python
import jax
import jax.numpy as jnp
from jax import lax
import numpy as np

D_MODEL = 2048
BATCH = 1
SEQ = 8192
DEPTH = 4

A_HEADS = 16
A_KV_GROUPS = 4
A_HEADS_PER_GROUP = A_HEADS // A_KV_GROUPS
A_HEAD_DIM = D_MODEL // A_HEADS
A_WIDTH = A_HEADS * A_HEAD_DIM
A_KV_WIDTH = A_KV_GROUPS * A_HEAD_DIM
A_N_BRANCH = 3
CMP_BLOCK = 32
CMP_STRIDE = 16
CMP_HIDDEN = 256
SEL_BLOCK = 64
SEL_TOP_N = 16
SEL_Q_CHUNK = 64
WIN_SIZE = 512
WIN_Q_BLOCK = 128
B_HEADS = 4
B_KEY_WIDTH = D_MODEL // 2
B_VAL_WIDTH = D_MODEL
B_KEY_DIM = B_KEY_WIDTH // B_HEADS
B_VAL_DIM = B_VAL_WIDTH // B_HEADS
B_ALPHA_RANK = 16
B_GATE_TEMP = 16.0
B_CHUNK = 64
NORM_EPS = 1e-6

IN_SIZES = (A_WIDTH, A_KV_WIDTH, A_KV_WIDTH, A_KV_WIDTH, A_KV_WIDTH, A_KV_WIDTH, A_KV_WIDTH,
            A_HEADS * A_N_BRANCH, A_WIDTH,
            B_KEY_WIDTH, B_KEY_WIDTH, B_VAL_WIDTH, B_ALPHA_RANK, B_VAL_WIDTH,
            D_MODEL, D_MODEL)
IN_COLS = sum(IN_SIZES)

kernel_name = 'hybrid_nsa_gla_gated_block'


def rms_norm(x, w):
    xf = x.astype(jnp.float32)
    y = xf * lax.rsqrt(jnp.mean(xf * xf, axis=-1, keepdims=True) + NORM_EPS)
    return (y * w.astype(jnp.float32)).astype(x.dtype)


def masked_softmax(s, mask):
    s = jnp.where(mask, s.astype(jnp.float32), -jnp.inf)
    m = jnp.max(s, axis=-1, keepdims=True)
    m = jnp.where(jnp.isfinite(m), m, 0.0)
    p = jnp.exp(s - m)
    denom = jnp.sum(p, axis=-1, keepdims=True)
    return p / jnp.where(denom > 0, denom, 1.0)


def compress_tokens(kv, pe, w1, b1, w2, b2):
    B, T, G, DH = kv.shape
    ratio = CMP_BLOCK // CMP_STRIDE
    c = kv.reshape(B, T // CMP_STRIDE, CMP_STRIDE, G, DH)
    nc = T // CMP_STRIDE - ratio + 1
    blocks = jnp.concatenate([c[:, r:r + nc] for r in range(ratio)], axis=2)
    blocks = blocks + pe[None, None, :, None, :]
    flat = blocks.transpose(0, 1, 3, 2, 4).reshape(B, nc, G, CMP_BLOCK * DH)
    hid = jax.nn.gelu(flat @ w1 + b1)
    return hid @ w2 + b2


def selected_attention(q, k_sel, v_sel, idx):
    B, T, G, HG, DH = q.shape
    n = idx.shape[-1]
    nb = T // SEL_BLOCK
    kb = k_sel.reshape(B, nb, SEL_BLOCK, G, DH).transpose(0, 3, 1, 2, 4)
    vb = v_sel.reshape(B, nb, SEL_BLOCK, G, DH).transpose(0, 3, 1, 2, 4)
    nqc = T // SEL_Q_CHUNK
    q_c = jnp.moveaxis(q.reshape(B, nqc, SEL_Q_CHUNK, G, HG, DH), 1, 0)
    i_c = jnp.moveaxis(idx.reshape(B, nqc, SEL_Q_CHUNK, G, n), 1, 0)
    t_c = jnp.arange(T, dtype=jnp.int32).reshape(nqc, SEL_Q_CHUNK)
    b_ix = jnp.arange(B)[:, None, None, None]
    g_ix = jnp.arange(G)[None, None, :, None]
    offs = jnp.arange(SEL_BLOCK, dtype=jnp.int32)

    def one_chunk(args):
        qc, ic, tc = args
        kg = kb[b_ix, g_ix, ic]
        vg = vb[b_ix, g_ix, ic]
        kpos = ic[..., None] * SEL_BLOCK + offs
        mask = kpos <= tc[None, :, None, None, None]
        s = jnp.einsum('bcghd,bcgnld->bcghnl', qc, kg)
        s = s.reshape(B, SEL_Q_CHUNK, G, HG, n * SEL_BLOCK)
        p = masked_softmax(s, mask.reshape(B, SEL_Q_CHUNK, G, 1, n * SEL_BLOCK))
        return jnp.einsum('bcghk,bcgkd->bcghd', p.astype(vg.dtype),
                          vg.reshape(B, SEL_Q_CHUNK, G, n * SEL_BLOCK, DH))

    o = lax.map(one_chunk, (q_c, i_c, t_c))
    return jnp.moveaxis(o, 0, 1).reshape(B, T, G, HG, DH)


def window_attention(q, k_win, v_win):
    B, T, G, HG, DH = q.shape
    nq = T // WIN_Q_BLOCK
    r = WIN_SIZE // WIN_Q_BLOCK
    nk = (r + 1) * WIN_Q_BLOCK

    def band(u):
        up = jnp.pad(u, ((0, 0), (WIN_SIZE, 0), (0, 0), (0, 0)))
        up = up.reshape(B, (T + WIN_SIZE) // WIN_Q_BLOCK, WIN_Q_BLOCK, G, DH)
        return jnp.concatenate([up[:, i:i + nq] for i in range(r + 1)], axis=2)

    kw, vw = band(k_win), band(v_win)
    qb = q.reshape(B, nq, WIN_Q_BLOCK, G, HG, DH)
    a = jnp.arange(WIN_Q_BLOCK)[:, None]
    c = jnp.arange(nk)[None, :]
    rel = a - c + WIN_SIZE
    kpos = jnp.arange(nq)[:, None, None] * WIN_Q_BLOCK - WIN_SIZE + c[None]
    mask = (rel >= 0) & (rel < WIN_SIZE) & (kpos >= 0)
    s = jnp.einsum('bnqghd,bnkgd->bnghqk', qb, kw)
    p = masked_softmax(s, mask[None, :, None, None, :, :])
    o = jnp.einsum('bnghqk,bnkgd->bnqghd', p.astype(vw.dtype), vw)
    return o.reshape(B, T, G, HG, DH)


def nsa_mixer(q, k_cmp, v_cmp, k_sel, v_sel, k_win, v_win, g_br,
              cmp_pe, cmp_w1, cmp_b1, cmp_w2, cmp_b2):
    B, T = q.shape[:2]
    G, HG, DH = A_KV_GROUPS, A_HEADS_PER_GROUP, A_HEAD_DIM
    q = q.reshape(B, T, G, HG, DH) * (DH ** -0.5)
    kv = lambda u: u.reshape(B, T, G, DH)
    pos = jnp.arange(T, dtype=jnp.int32)
    kc = compress_tokens(kv(k_cmp), cmp_pe[0], cmp_w1[0], cmp_b1[0], cmp_w2[0], cmp_b2[0])
    vc = compress_tokens(kv(v_cmp), cmp_pe[1], cmp_w1[1], cmp_b1[1], cmp_w2[1], cmp_b2[1])
    nc = kc.shape[1]
    c_start = jnp.arange(nc, dtype=jnp.int32) * CMP_STRIDE
    c_mask = (c_start + CMP_BLOCK - 1)[None, :] <= pos[:, None]
    s_c = jnp.einsum('btghd,bngd->btghn', q, kc)
    p_c = masked_softmax(s_c, c_mask[None, :, None, None, :])
    o_cmp = jnp.einsum('btghn,bngd->btghd', p_c.astype(vc.dtype), vc)
    nb = T // SEL_BLOCK
    j = jnp.arange(nb, dtype=jnp.int32)
    overlap = ((c_start[:, None] <= j[None, :] * SEL_BLOCK + SEL_BLOCK - 1) &
               (c_start[:, None] + CMP_BLOCK - 1 >= j[None, :] * SEL_BLOCK)).astype(jnp.float32)
    p_slc = jnp.einsum('btgn,nj->btgj', jnp.sum(p_c, axis=3), overlap)
    cur = pos[:, None] // SEL_BLOCK
    valid = j[None, :] * SEL_BLOCK <= pos[:, None]
    forced = (j[None, :] == 0) | (j[None, :] == cur) | (j[None, :] == cur - 1)
    score = jnp.where(forced[None, :, None, :], jnp.inf,
                      jnp.where(valid[None, :, None, :], p_slc, -jnp.inf))
    n_sel = min(SEL_TOP_N, nb)
    _, idx = lax.top_k(score, n_sel)
    o_sel = selected_attention(q, kv(k_sel), kv(v_sel), idx.astype(jnp.int32))
    o_win = window_attention(q, kv(k_win), kv(v_win))
    g = jax.nn.sigmoid(g_br).reshape(B, T, G, HG, A_N_BRANCH)
    o = g[..., 0:1] * o_cmp + g[..., 1:2] * o_sel + g[..., 2:3] * o_win
    return o.reshape(B, T, A_WIDTH)


def gla_mixer(q, k, v, a_lr, alpha_w, alpha_b, norm_w):
    B, T = q.shape[:2]
    H, DK, DV, C = B_HEADS, B_KEY_DIM, B_VAL_DIM, B_CHUNK
    nck = T // C
    f32 = jnp.float32
    log_a = jax.nn.log_sigmoid((a_lr @ alpha_w + alpha_b).astype(f32)) / B_GATE_TEMP
    chunks = lambda u, d: u.astype(f32).reshape(B, nck, C, H, d)
    qc = chunks(q, DK) * (DK ** -0.5)
    kc = chunks(k, DK)
    vc = chunks(v, DV)
    cum = jnp.cumsum(chunks(log_a, DK), axis=2)
    last = cum[:, :, -1:]
    q_dec = qc * jnp.exp(cum)
    k_inv = kc * jnp.exp(-cum)
    k_state = kc * jnp.exp(last - cum)
    tril = jnp.tril(jnp.ones((C, C), dtype=bool))
    att = jnp.where(tril, jnp.einsum('bnthd,bnshd->bnhts', q_dec, k_inv), 0.0)
    o_intra = jnp.einsum('bnhts,bnshv->bnthv', att, vc)

    def step(S, xs):
        q_c, k_c, v_c, dec = xs
        o = jnp.einsum('bthd,bhdv->bthv', q_c, S)
        S = S * dec[..., None] + jnp.einsum('bshd,bshv->bhdv', k_c, v_c)
        return S, o

    S0 = jnp.zeros((B, H, DK, DV), f32)
    xs = (jnp.moveaxis(q_dec, 1, 0), jnp.moveaxis(k_state, 1, 0), jnp.moveaxis(vc, 1, 0),
          jnp.moveaxis(jnp.exp(last[:, :, 0]), 1, 0))
    _, o_inter = lax.scan(step, S0, xs)
    o = (o_intra + jnp.moveaxis(o_inter, 0, 1)).reshape(B, T, H, DV)
    o = o * lax.rsqrt(jnp.mean(o * o, axis=-1, keepdims=True) + NORM_EPS) * norm_w.astype(f32)
    return o.reshape(B, T, B_VAL_WIDTH).astype(q.dtype)


def setup_inputs(seed: int = 0) -> dict:
    key = jax.random.key(seed)
    ks = jax.random.split(key, 18)
    f32 = jnp.float32
    nrm = lambda k, shape, scale: jax.random.normal(k, shape, f32) * scale
    return {
        'x': nrm(ks[0], (BATCH, SEQ, D_MODEL), 1.0),
        'norm_w': 1.0 + nrm(ks[1], (DEPTH, D_MODEL), 0.02),
        'w_in': nrm(ks[2], (DEPTH, D_MODEL, IN_COLS), D_MODEL ** -0.5),
        'cmp_pe': nrm(ks[3], (DEPTH, 2, CMP_BLOCK, A_HEAD_DIM), 0.5),
        'cmp_w1': nrm(ks[4], (DEPTH, 2, CMP_BLOCK * A_HEAD_DIM, CMP_HIDDEN), (CMP_BLOCK * A_HEAD_DIM) ** -0.5),
        'cmp_b1': nrm(ks[5], (DEPTH, 2, CMP_HIDDEN), 0.01),
        'cmp_w2': nrm(ks[6], (DEPTH, 2, CMP_HIDDEN, A_HEAD_DIM), CMP_HIDDEN ** -0.5),
        'cmp_b2': nrm(ks[7], (DEPTH, 2, A_HEAD_DIM), 0.01),
        'gla_alpha_w': nrm(ks[8], (DEPTH, B_ALPHA_RANK, B_KEY_WIDTH), B_ALPHA_RANK ** -0.5),
        'gla_alpha_b': nrm(ks[9], (DEPTH, B_KEY_WIDTH), 0.1),
        'gla_norm_w': 1.0 + nrm(ks[10], (DEPTH, B_VAL_DIM), 0.02),
        'p_a': nrm(ks[11], (DEPTH, A_WIDTH, D_MODEL), A_WIDTH ** -0.5),
        'p_b': nrm(ks[12], (DEPTH, B_VAL_WIDTH, D_MODEL), B_VAL_WIDTH ** -0.5),
        'w_out': nrm(ks[13], (DEPTH, D_MODEL, D_MODEL), D_MODEL ** -0.5),
        'final_norm_w': 1.0 + nrm(ks[14], (D_MODEL,), 0.02),
    }


def reference(x, norm_w, w_in, cmp_pe, cmp_w1, cmp_b1, cmp_w2, cmp_b2,
              gla_alpha_w, gla_alpha_b, gla_norm_w, p_a, p_b, w_out, final_norm_w):
    split_points = np.cumsum(IN_SIZES)[:-1].tolist()
    for layer in range(DEPTH):
        h = rms_norm(x, norm_w[layer])
        cols = h @ w_in[layer]
        (q_a, k_cmp, v_cmp, k_sel, v_sel, k_win, v_win, g_br, z_a,
         q_b, k_b, v_b, a_lr, z_b, gate_a, gate_b) = jnp.split(cols, split_points, axis=-1)
        o_a = nsa_mixer(q_a, k_cmp, v_cmp, k_sel, v_sel, k_win, v_win, g_br,
                        cmp_pe[layer], cmp_w1[layer], cmp_b1[layer], cmp_w2[layer], cmp_b2[layer])
        y_a = (o_a * jax.nn.silu(z_a)) @ p_a[layer]
        o_b = gla_mixer(q_b, k_b, v_b, a_lr, gla_alpha_w[layer], gla_alpha_b[layer], gla_norm_w[layer])
        y_b = (o_b * jax.nn.silu(z_b)) @ p_b[layer]
        y = jax.nn.sigmoid(gate_a) * y_a + jax.nn.sigmoid(gate_b) * y_b
        x = x + y @ w_out[layer]
    return rms_norm(x, final_norm_w)
```

```python
import functools

import jax
import jax.numpy as jnp
import numpy as np
from jax import lax
from jax.experimental import pallas as pl
from jax.experimental.pallas import tpu as pltpu

F32 = jnp.float32
BF16 = jnp.bfloat16

D_MODEL = 2048
A_HEADS = 16
A_GROUPS = 4
A_HPG = A_HEADS // A_GROUPS
A_DH = 128
A_KVW = A_GROUPS * A_DH
CMP_BLOCK = 32
CMP_STRIDE = 16
CMP_HIDDEN = 256
SEL_BLOCK = 64
SEL_SHIFT = 6
SEL_TOP_N = 16
WIN_SIZE = 512
B_HEADS = 4
B_DK = 256
B_DV = 512
B_RANK = 16
B_GATE_TEMP = 16.0
B_CHUNK = 64
NORM_EPS = 1e-6

LANES = 128
NB_PAD = LANES
MASK_BIG = 2.0 ** 100

_SEGS = (
    ("q_a", 0, 2048, 2048),
    ("z_a", 5168, 2048, 2048),
    ("v_b", 9264, 2048, 2048),
    ("z_b", 11328, 2048, 2048),
    ("gate_a", 13376, 2048, 2048),
    ("gate_b", 15424, 2048, 2048),
    ("q_b", 7216, 1024, 1024),
    ("k_b", 8240, 1024, 1024),
    ("k_cmp", 2048, 512, 512),
    ("v_cmp", 2560, 512, 512),
    ("k_sel", 3072, 512, 512),
    ("v_sel", 3584, 512, 512),
    ("k_win", 4096, 512, 512),
    ("v_win", 4608, 512, 512),
    ("g_br", 5120, 48, LANES),
    ("a_lr", 11312, 16, LANES),
)
COL = {}
_off = 0
for _name, _src, _w, _pw in _SEGS:
    COL[_name] = _off
    _off += _pw
IN_COLS_PAD = _off

VMEM_LIMIT = 56 * 1024 * 1024


def _cparams(sem):
    return pltpu.CompilerParams(dimension_semantics=sem, vmem_limit_bytes=VMEM_LIMIT)


def _dot(a, b):
    return jnp.dot(a, b, preferred_element_type=F32)


def _dot_nt(a, b):
    return lax.dot_general(a, b, (((1,), (1,)), ((), ())), preferred_element_type=F32)


def _dot_tn(a, b):
    return lax.dot_general(a, b, (((0,), (0,)), ((), ())), preferred_element_type=F32)


def _split_bf16(x):
    hi = x.astype(BF16)
    lo = (x - hi.astype(F32)).astype(BF16)
    return hi, lo


def _inproj_kernel(x_ref, nw_ref, w_ref, o_ref, h_ref):
    @pl.when(pl.program_id(1) == 0)
    def _():
        x = x_ref[...]
        ms = jnp.mean(x * x, axis=-1, keepdims=True)
        h_ref[...] = (x * lax.rsqrt(ms + NORM_EPS) * nw_ref[...]).astype(BF16)

    o_ref[...] = _dot(h_ref[...], w_ref[...]).astype(o_ref.dtype)


def _inproj(x, nw, w, tm=1024, tn=768):
    T, D = x.shape
    N = w.shape[1]
    tm = min(tm, T)
    return pl.pallas_call(
        _inproj_kernel,
        grid=(T // tm, N // tn),
        in_specs=[
            pl.BlockSpec((tm, D), lambda i, j: (i, 0)),
            pl.BlockSpec((1, D), lambda i, j: (0, 0)),
            pl.BlockSpec((D, tn), lambda i, j: (0, j)),
        ],
        out_specs=pl.BlockSpec((tm, tn), lambda i, j: (i, j)),
        out_shape=jax.ShapeDtypeStruct((T, N), BF16),
        scratch_shapes=[pltpu.VMEM((tm, D), BF16)],
        compiler_params=_cparams(("parallel", "arbitrary")),
        name="inproj",
    )(x, nw, w)


def _gelu_tanh(x):
    c = np.float32(np.sqrt(2.0 / np.pi))
    return 0.5 * x * (1.0 + jnp.tanh(c * (x + np.float32(0.044715) * (x * x * x))))


def _compress_kernel(x_ref, pe_ref, w1_ref, b1_ref, w2_ref, b2_ref, o_ref):
    nch = x_ref.shape[2]
    half = CMP_STRIDE * A_DH
    x = x_ref[0, 0].astype(F32)
    pe = pe_ref[0]
    xlo = (x + pe[0:1, :]).astype(BF16)
    xhi = (x + pe[1:2, :]).astype(BF16)
    a = _dot(xlo, w1_ref[0, 0:half, :])
    b = _dot(xhi, w1_ref[0, half:2 * half, :])
    b_next = pltpu.roll(b, nch - 1, axis=0)
    row = lax.broadcasted_iota(jnp.int32, b.shape, 0)
    b_next = jnp.where(row < nch - 1, b_next, 0.0)
    hid = _gelu_tanh(a + b_next + b1_ref[0])
    out = _dot(hid.astype(BF16), w2_ref[0]) + b2_ref[0]
    o_ref[0, 0] = out.astype(o_ref.dtype)


def _compress(xc, pe, w1, b1, w2, b2):
    _, G, nch, width = xc.shape
    return pl.pallas_call(
        _compress_kernel,
        grid=(2, G),
        in_specs=[
            pl.BlockSpec((1, 1, nch, width), lambda s, g: (s, g, 0, 0)),
            pl.BlockSpec((1, 2, width), lambda s, g: (s, 0, 0)),
            pl.BlockSpec((1, 2 * width, CMP_HIDDEN), lambda s, g: (s, 0, 0)),
            pl.BlockSpec((1, 1, CMP_HIDDEN), lambda s, g: (s, 0, 0)),
            pl.BlockSpec((1, CMP_HIDDEN, A_DH), lambda s, g: (s, 0, 0)),
            pl.BlockSpec((1, 1, A_DH), lambda s, g: (s, 0, 0)),
        ],
        out_specs=pl.BlockSpec((1, 1, nch, A_DH), lambda s, g: (s, g, 0, 0)),
        out_shape=jax.ShapeDtypeStruct((2, G, nch, A_DH), BF16),
        compiler_params=_cparams(("parallel", "parallel")),
        name="compress",
    )(xc, pe, w1, b1, w2, b2)


def _cmp_attn_kernel(q_ref, kc_ref, vc_ref, o_ref, qsel_ref):
    tq = q_ref.shape[0]
    nch = kc_ref.shape[2]
    t0 = pl.program_id(1) * tq
    kc = kc_ref[0, 0]
    vc = vc_ref[0, 0]
    tpos = t0 + lax.broadcasted_iota(jnp.int32, (tq, nch), 0)
    cidx = lax.broadcasted_iota(jnp.int32, (tq, nch), 1)
    cmask = cidx * CMP_STRIDE + (CMP_BLOCK - 1) <= tpos
    psum = jnp.zeros((tq, nch), F32)
    for h in range(A_HPG):
        qh = q_ref[:, h * A_DH:(h + 1) * A_DH]
        s = jnp.where(cmask, _dot_nt(qh, kc), -jnp.inf)
        m = jnp.max(s, axis=-1, keepdims=True)
        m = jnp.where(m > -jnp.inf, m, 0.0)
        p = jnp.exp(s - m)
        denom = jnp.sum(p, axis=-1, keepdims=True)
        inv = 1.0 / jnp.where(denom > 0, denom, 1.0)
        o_ref[:, h * A_DH:(h + 1) * A_DH] = (_dot(p.astype(BF16), vc) * inv).astype(o_ref.dtype)
        psum = psum + p * inv

    jj = lax.broadcasted_iota(jnp.int32, (NB_PAD, nch), 0)
    cs = lax.broadcasted_iota(jnp.int32, (NB_PAD, nch), 1) * CMP_STRIDE
    ov = ((cs <= jj * SEL_BLOCK + SEL_BLOCK - 1) & (cs + CMP_BLOCK - 1 >= jj * SEL_BLOCK))
    ov = jnp.where(ov, 1.0, 0.0).astype(BF16)
    p_hi, p_lo = _split_bf16(psum)
    pslc = _dot_nt(ov, p_hi) + _dot_nt(ov, p_lo)

    jf = lax.broadcasted_iota(jnp.int32, (NB_PAD, tq), 0)
    tl = t0 + lax.broadcasted_iota(jnp.int32, (NB_PAD, tq), 1)
    cur = jnp.right_shift(tl, SEL_SHIFT)
    valid = jf <= cur
    forced = (jf == 0) | (jf == cur) | (jf == cur - 1)
    score = jnp.where(forced, jnp.inf, jnp.where(valid, pslc, -jnp.inf))
    jff = jf.astype(F32)
    sel = jnp.zeros((NB_PAD, tq), F32)
    for _ in range(SEL_TOP_N):
        mx = jnp.max(score, axis=0, keepdims=True)
        first = jnp.min(jnp.where(score == mx, jff, float(NB_PAD)), axis=0, keepdims=True)
        pick = jff == first
        sel = jnp.where(pick, 1.0, sel)
        score = jnp.where(pick, -jnp.inf, score)
    neg = jnp.where((sel > 0.5) & valid, 0.0, -MASK_BIG)
    qsel_ref[...] = neg.T.astype(qsel_ref.dtype)


def _cmp_attn(cols, kvc, tq=256):
    T = cols.shape[0]
    G = A_GROUPS
    nch = kvc.shape[2]
    tq = min(tq, T)
    qw = A_HPG * A_DH
    return pl.pallas_call(
        _cmp_attn_kernel,
        grid=(G, T // tq),
        in_specs=[
            pl.BlockSpec((tq, qw), lambda g, i: (i, COL["q_a"] // qw + g)),
            pl.BlockSpec((1, 1, nch, A_DH), lambda g, i: (0, g, 0, 0)),
            pl.BlockSpec((1, 1, nch, A_DH), lambda g, i: (1, g, 0, 0)),
        ],
        out_specs=[
            pl.BlockSpec((tq, qw), lambda g, i: (i, g)),
            pl.BlockSpec((tq, NB_PAD), lambda g, i: (i, g)),
        ],
        out_shape=[
            jax.ShapeDtypeStruct((T, A_HEADS * A_DH), BF16),
            jax.ShapeDtypeStruct((T, G * NB_PAD), BF16),
        ],
        compiler_params=_cparams(("parallel", "parallel")),
        name="cmp_attn",
    )(cols, kvc, kvc)


def _sel_attn_kernel(q_ref, qsel_ref, k_ref, v_ref, o_ref, qa_ref, m_ref, l_ref, acc_ref):
    tq = q_ref.shape[0]
    tk = k_ref.shape[0]
    i = pl.program_id(1)
    kt = pl.program_id(2)
    last_kt = (i * tq + tq - 1) // tk

    @pl.when(kt == 0)
    def _():
        for h in range(A_HPG):
            qa_ref[h * tq:(h + 1) * tq, 0:A_DH] = q_ref[:, h * A_DH:(h + 1) * A_DH]
            qa_ref[h * tq:(h + 1) * tq, A_DH:2 * A_DH] = qsel_ref[...]
        m_ref[...] = jnp.full(m_ref.shape, -jnp.inf, F32)
        l_ref[...] = jnp.zeros(l_ref.shape, F32)
        acc_ref[...] = jnp.zeros(acc_ref.shape, F32)

    @pl.when(kt <= last_kt)
    def _():
        kpos_r = kt * tk + lax.broadcasted_iota(jnp.int32, (tk, NB_PAD), 0)
        blk = lax.broadcasted_iota(jnp.int32, (tk, NB_PAD), 1)
        onehot = jnp.where(jnp.right_shift(kpos_r, SEL_SHIFT) == blk, 1.0, 0.0).astype(BF16)
        ka = jnp.concatenate([k_ref[...], onehot], axis=1)
        s = _dot_nt(qa_ref[...], ka)
        tpos = i * tq + lax.broadcasted_iota(jnp.int32, (tq, tk), 0)
        kpos = kt * tk + lax.broadcasted_iota(jnp.int32, (tq, tk), 1)
        causal = kpos <= tpos
        s = jnp.concatenate(
            [jnp.where(causal, s[h * tq:(h + 1) * tq], -MASK_BIG) for h in range(A_HPG)], axis=0)
        m_old = m_ref[...]
        m_new = jnp.maximum(m_old, jnp.max(s, axis=-1, keepdims=True))
        alpha = jnp.exp(m_old - m_new)
        p = jnp.exp(s - m_new)
        l_ref[...] = alpha * l_ref[...] + jnp.sum(p, axis=-1, keepdims=True)
        acc_ref[...] = alpha * acc_ref[...] + _dot(p.astype(BF16), v_ref[...])
        m_ref[...] = m_new

    @pl.when(kt == pl.num_programs(2) - 1)
    def _():
        for h in range(A_HPG):
            o = acc_ref[h * tq:(h + 1) * tq] / l_ref[h * tq:(h + 1) * tq]
            o_ref[:, h * A_DH:(h + 1) * A_DH] = o.astype(o_ref.dtype)


def _sel_attn(cols, qsel, tq=256, tk=512):
    T = cols.shape[0]
    G = A_GROUPS
    tq = min(tq, T)
    tk = min(tk, T)
    qw = A_HPG * A_DH
    rows = A_HPG * tq

    def kv_map(col):
        def f(g, i, kt):
            return (jnp.minimum(kt, (i * tq + tq - 1) // tk), col // A_DH + g)
        return f

    return pl.pallas_call(
        _sel_attn_kernel,
        grid=(G, T // tq, T // tk),
        in_specs=[
            pl.BlockSpec((tq, qw), lambda g, i, kt: (i, COL["q_a"] // qw + g)),
            pl.BlockSpec((tq, NB_PAD), lambda g, i, kt: (i, g)),
            pl.BlockSpec((tk, A_DH), kv_map(COL["k_sel"])),
            pl.BlockSpec((tk, A_DH), kv_map(COL["v_sel"])),
        ],
        out_specs=pl.BlockSpec((tq, qw), lambda g, i, kt: (i, g)),
        out_shape=jax.ShapeDtypeStruct((T, A_HEADS * A_DH), BF16),
        scratch_shapes=[
            pltpu.VMEM((rows, 2 * A_DH), BF16),
            pltpu.VMEM((rows, 1), F32),
            pltpu.VMEM((rows, 1), F32),
            pltpu.VMEM((rows, A_DH), F32),
        ],
        compiler_params=_cparams(("parallel", "parallel", "arbitrary")),
        name="sel_attn",
    )(cols, qsel, cols, cols)


def _win_attn_kernel(q_ref, k0_ref, k1_ref, k2_ref, v0_ref, v1_ref, v2_ref, o_ref):
    tq = q_ref.shape[0]
    i = pl.program_id(1)
    nk = 3 * tq
    k = jnp.concatenate([k0_ref[...], k1_ref[...], k2_ref[...]], axis=0)
    v = jnp.concatenate([v0_ref[...], v1_ref[...], v2_ref[...]], axis=0)
    r = lax.broadcasted_iota(jnp.int32, (tq, nk), 0)
    c = lax.broadcasted_iota(jnp.int32, (tq, nk), 1)
    rel = 2 * tq + r - c
    kpos = (i - 2) * tq + c
    mask = (rel >= 0) & (rel < WIN_SIZE) & (kpos >= 0)
    for h in range(A_HPG):
        qh = q_ref[:, h * A_DH:(h + 1) * A_DH]
        s = jnp.where(mask, _dot_nt(qh, k), -jnp.inf)
        m = jnp.max(s, axis=-1, keepdims=True)
        p = jnp.exp(s - m)
        denom = jnp.sum(p, axis=-1, keepdims=True)
        o = _dot(p.astype(BF16), v) / denom
        o_ref[:, h * A_DH:(h + 1) * A_DH] = o.astype(o_ref.dtype)


def _win_attn(cols, tq=256):
    T = cols.shape[0]
    G = A_GROUPS
    tq = min(tq, T)
    assert WIN_SIZE <= 2 * tq
    qw = A_HPG * A_DH

    def kv_spec(col, back):
        return pl.BlockSpec((tq, A_DH), lambda g, i: (jnp.maximum(i - back, 0), col // A_DH + g))

    return pl.pallas_call(
        _win_attn_kernel,
        grid=(G, T // tq),
        in_specs=[pl.BlockSpec((tq, qw), lambda g, i: (i, COL["q_a"] // qw + g)),
                  kv_spec(COL["k_win"], 2), kv_spec(COL["k_win"], 1), kv_spec(COL["k_win"], 0),
                  kv_spec(COL["v_win"], 2), kv_spec(COL["v_win"], 1), kv_spec(COL["v_win"], 0)],
        out_specs=pl.BlockSpec((tq, qw), lambda g, i: (i, g)),
        out_shape=jax.ShapeDtypeStruct((T, A_HEADS * A_DH), BF16),
        compiler_params=_cparams(("parallel", "parallel")),
        name="win_attn",
    )(cols, cols, cols, cols, cols, cols, cols)


def _gla_kernel(q_ref, k_ref, v_ref, a_ref, aw_ref, ab_ref, nw_ref, o_ref, st_ref):
    tc = q_ref.shape[0]
    C = B_CHUNK

    @pl.when(pl.program_id(1) == 0)
    def _():
        st_ref[...] = jnp.zeros(st_ref.shape, F32)

    logits = _dot(a_ref[...], aw_ref[...]) + ab_ref[...]
    log_a = jax.nn.log_sigmoid(logits) * (1.0 / B_GATE_TEMP)
    ri = lax.broadcasted_iota(jnp.int32, (C, C), 0)
    ci = lax.broadcasted_iota(jnp.int32, (C, C), 1)
    tril = ci <= ri
    tril_b = jnp.where(tril, 1.0, 0.0).astype(BF16)
    nw = nw_ref[...]
    for c in range(tc // C):
        sl = slice(c * C, (c + 1) * C)
        la_hi, la_lo = _split_bf16(log_a[sl])
        cum = _dot(tril_b, la_hi) + _dot(tril_b, la_lo)
        last = cum[C - 1:C, :]
        qc = q_ref[sl, :].astype(F32)
        kc = k_ref[sl, :].astype(F32)
        vc = v_ref[sl, :]
        q_dec = (qc * jnp.exp(cum)).astype(BF16)
        k_inv = (kc * jnp.exp(-cum)).astype(BF16)
        k_state = (kc * jnp.exp(last - cum)).astype(BF16)
        att = jnp.where(tril, _dot_nt(q_dec, k_inv), 0.0)
        st = st_ref[...]
        o = _dot(att.astype(BF16), vc) + _dot_nt(q_dec, st.astype(BF16))
        st_ref[...] = st * jnp.exp(last) + _dot_tn(vc, k_state)
        o = o * lax.rsqrt(jnp.mean(o * o, axis=-1, keepdims=True) + NORM_EPS) * nw
        o_ref[sl, :] = o.astype(o_ref.dtype)


def _gla(cols, aw, ab, nw, tc=512):
    T = cols.shape[0]
    tc = min(tc, T)
    return pl.pallas_call(
        _gla_kernel,
        grid=(B_HEADS, T // tc),
        in_specs=[
            pl.BlockSpec((tc, B_DK), lambda h, i: (i, COL["q_b"] // B_DK + h)),
            pl.BlockSpec((tc, B_DK), lambda h, i: (i, COL["k_b"] // B_DK + h)),
            pl.BlockSpec((tc, B_DV), lambda h, i: (i, COL["v_b"] // B_DV + h)),
            pl.BlockSpec((tc, LANES), lambda h, i: (i, COL["a_lr"] // LANES)),
            pl.BlockSpec((LANES, B_DK), lambda h, i: (0, h)),
            pl.BlockSpec((1, B_DK), lambda h, i: (0, h)),
            pl.BlockSpec((1, B_DV), lambda h, i: (0, 0)),
        ],
        out_specs=pl.BlockSpec((tc, B_DV), lambda h, i: (i, h)),
        out_shape=jax.ShapeDtypeStruct((T, B_HEADS * B_DV), BF16),
        scratch_shapes=[pltpu.VMEM((B_DV, B_DK), F32)],
        compiler_params=_cparams(("parallel", "arbitrary")),
        name="gla",
    )(cols, cols, cols, cols, aw, ab, nw)


def _silu(x):
    return x * jax.nn.sigmoid(x)


def _merge_kernel(ocmp_ref, osel_ref, owin_ref, gbr_ref, za_ref, ob_ref, zb_ref, ga_ref, gb_ref,
                  pa_ref, pb_ref, y_ref, ua_ref):
    sg = jax.nn.sigmoid(gbr_ref[...].astype(F32))
    for hh in range(A_HEADS):
        cs = slice(hh * A_DH, (hh + 1) * A_DH)
        oa = (sg[:, 3 * hh:3 * hh + 1] * ocmp_ref[:, cs].astype(F32)
              + sg[:, 3 * hh + 1:3 * hh + 2] * osel_ref[:, cs].astype(F32)
              + sg[:, 3 * hh + 2:3 * hh + 3] * owin_ref[:, cs].astype(F32))
        ua_ref[:, cs] = (oa * _silu(za_ref[:, cs].astype(F32))).astype(BF16)
    ya = _dot(ua_ref[...], pa_ref[...])
    ub = (ob_ref[...].astype(F32) * _silu(zb_ref[...].astype(F32))).astype(BF16)
    yb = _dot(ub, pb_ref[...])
    y = jax.nn.sigmoid(ga_ref[...].astype(F32)) * ya + jax.nn.sigmoid(gb_ref[...].astype(F32)) * yb
    y_ref[...] = y.astype(y_ref.dtype)


def _merge(o_cmp, o_sel, o_win, cols, o_b, pa, pb, tm=256):
    T = cols.shape[0]
    D = D_MODEL
    tm = min(tm, T)
    row = lambda c: pl.BlockSpec((tm, D), lambda i: (i, c))
    const = lambda: pl.BlockSpec((D, D), lambda i: (0, 0), pipeline_mode=pl.Buffered(1))
    return pl.pallas_call(
        _merge_kernel,
        grid=(T // tm,),
        in_specs=[row(0), row(0), row(0),
                  pl.BlockSpec((tm, LANES), lambda i: (i, COL["g_br"] // LANES)),
                  row(COL["z_a"] // D), row(0), row(COL["z_b"] // D),
                  row(COL["gate_a"] // D), row(COL["gate_b"] // D),
                  const(), const()],
        out_specs=row(0),
        out_shape=jax.ShapeDtypeStruct((T, D), BF16),
        scratch_shapes=[pltpu.VMEM((tm, D), BF16)],
        compiler_params=_cparams(("parallel",)),
        name="merge",
    )(o_cmp, o_sel, o_win, cols, cols, o_b, cols, cols, cols, pa, pb)


def _outproj_kernel(x_ref, y_ref, w_ref, fw_ref, o_ref, *, final_norm):
    xn = x_ref[...] + _dot(y_ref[...], w_ref[...])
    if final_norm:
        ms = jnp.mean(xn * xn, axis=-1, keepdims=True)
        xn = xn * lax.rsqrt(ms + NORM_EPS) * fw_ref[...]
    o_ref[...] = xn


def _outproj(x, y, w, fw, final_norm, tm=512):
    T, D = x.shape
    tm = min(tm, T)
    return pl.pallas_call(
        functools.partial(_outproj_kernel, final_norm=final_norm),
        grid=(T // tm,),
        in_specs=[
            pl.BlockSpec((tm, D), lambda i: (i, 0)),
            pl.BlockSpec((tm, D), lambda i: (i, 0)),
            pl.BlockSpec((D, D), lambda i: (0, 0), pipeline_mode=pl.Buffered(1)),
            pl.BlockSpec((1, D), lambda i: (0, 0)),
        ],
        out_specs=pl.BlockSpec((tm, D), lambda i: (i, 0)),
        out_shape=jax.ShapeDtypeStruct((T, D), F32),
        compiler_params=_cparams(("parallel",)),
        name="outproj",
    )(x, y, w, fw)


def _prep_w_in(w_in):
    parts = []
    for name, src, w, pw in _SEGS:
        seg = w_in[:, :, src:src + w]
        if name == "q_a":
            seg = seg * np.float32(A_DH ** -0.5)
        elif name == "q_b":
            seg = seg * np.float32(B_DK ** -0.5)
        if pw > w:
            seg = jnp.pad(seg, ((0, 0), (0, 0), (0, pw - w)))
        parts.append(seg)
    return jnp.concatenate(parts, axis=-1).astype(BF16)


def kernel(x, norm_w, w_in, cmp_pe, cmp_w1, cmp_b1, cmp_w2, cmp_b2, gla_alpha_w, gla_alpha_b, gla_norm_w,
           p_a, p_b, w_out, final_norm_w):
    B, T, D = x.shape
    L = norm_w.shape[0]
    assert B == 1 and D == D_MODEL and T % 512 == 0 and T // SEL_BLOCK <= NB_PAD
    G = A_GROUPS
    nch = T // CMP_STRIDE

    w_in_p = _prep_w_in(w_in)
    pe_r = cmp_pe.reshape(L, 2, 2, CMP_STRIDE * A_DH)
    w1_b = cmp_w1.astype(BF16)
    w2_b = cmp_w2.astype(BF16)
    b1_r = cmp_b1.reshape(L, 2, 1, CMP_HIDDEN)
    b2_r = cmp_b2.reshape(L, 2, 1, A_DH)
    aw_p = jnp.pad(gla_alpha_w, ((0, 0), (0, LANES - B_RANK), (0, 0))).astype(BF16)
    ab_r = gla_alpha_b.reshape(L, 1, B_HEADS * B_DK)
    gnw_r = gla_norm_w.reshape(L, 1, B_DV)
    pa_b = p_a.astype(BF16)
    pb_b = p_b.astype(BF16)
    wo_b = w_out.astype(BF16)
    fw = final_norm_w.reshape(1, D)

    xs = x.reshape(T, D)
    for l in range(L):
        cols = _inproj(xs, norm_w[l].reshape(1, D), w_in_p[l])
        kv = cols[:, COL["k_cmp"]:COL["k_cmp"] + 2 * A_KVW]
        xc = kv.reshape(nch, CMP_STRIDE, 2, G, A_DH).transpose(2, 3, 0, 1, 4).reshape(2, G, nch, CMP_STRIDE * A_DH)
        kvc = _compress(xc, pe_r[l], w1_b[l], b1_r[l], w2_b[l], b2_r[l])
        o_cmp, qsel = _cmp_attn(cols, kvc)
        o_sel = _sel_attn(cols, qsel)
        o_win = _win_attn(cols)
        o_b = _gla(cols, aw_p[l], ab_r[l], gnw_r[l])
        y = _merge(o_cmp, o_sel, o_win, cols, o_b, pa_b[l], pb_b[l])
        xs = _outproj(xs, y, wo_b[l], fw, final_norm=(l == L - 1))
    return xs.reshape(B, T, D)
```

```python
import functools

import jax
import jax.numpy as jnp
import numpy as np
from jax import lax
from jax.experimental import pallas as pl
from jax.experimental.pallas import tpu as pltpu

F32 = jnp.float32
BF16 = jnp.bfloat16

D_MODEL = 2048
A_HEADS = 16
A_GROUPS = 4
A_HPG = A_HEADS // A_GROUPS
A_DH = 128
A_KVW = A_GROUPS * A_DH
CMP_BLOCK = 32
CMP_STRIDE = 16
CMP_HIDDEN = 256
SEL_BLOCK = 64
SEL_SHIFT = 6
SEL_TOP_N = 16
WIN_SIZE = 512
B_HEADS = 4
B_DK = 256
B_DV = 512
B_RANK = 16
B_GATE_TEMP = 16.0
B_CHUNK = 64
NORM_EPS = 1e-6

LANES = 128
NB_PAD = LANES
MASK_BIG = 2.0 ** 100
LOG2E = float(np.log2(np.e))

_SEGS = (
    ("q_a", 0, 2048, 2048),
    ("z_a", 5168, 2048, 2048),
    ("v_b", 9264, 2048, 2048),
    ("z_b", 11328, 2048, 2048),
    ("gate_a", 13376, 2048, 2048),
    ("gate_b", 15424, 2048, 2048),
    ("q_b", 7216, 1024, 1024),
    ("k_b", 8240, 1024, 1024),
    ("k_cmp", 2048, 512, 512),
    ("v_cmp", 2560, 512, 512),
    ("k_sel", 3072, 512, 512),
    ("v_sel", 3584, 512, 512),
    ("k_win", 4096, 512, 512),
    ("v_win", 4608, 512, 512),
    ("g_br", 5120, 48, LANES),
    ("a_lr", 11312, 16, LANES),
)
COL = {}
_off = 0
for _name, _src, _w, _pw in _SEGS:
    COL[_name] = _off
    _off += _pw
IN_COLS_PAD = _off

VMEM_LIMIT = 56 * 1024 * 1024


def _cparams(sem):
    return pltpu.CompilerParams(dimension_semantics=sem, vmem_limit_bytes=VMEM_LIMIT)


def _dot(a, b):
    return jnp.dot(a, b, preferred_element_type=F32)


def _dot_nt(a, b):
    return lax.dot_general(a, b, (((1,), (1,)), ((), ())), preferred_element_type=F32)


def _dot_tn(a, b):
    return lax.dot_general(a, b, (((0,), (0,)), ((), ())), preferred_element_type=F32)


def _split_bf16(x):
    hi = x.astype(BF16)
    lo = (x - hi.astype(F32)).astype(BF16)
    return hi, lo


def _inproj_kernel(x_ref, nw_ref, w_ref, o_ref, h_ref):
    @pl.when(pl.program_id(1) == 0)
    def _():
        x = x_ref[...]
        ms = jnp.mean(x * x, axis=-1, keepdims=True)
        h_ref[...] = (x * lax.rsqrt(ms + NORM_EPS) * nw_ref[...]).astype(BF16)

    o_ref[...] = _dot(h_ref[...], w_ref[...]).astype(o_ref.dtype)


def _inproj(x, nw, w, tm=1024, tn=768):
    T, D = x.shape
    N = w.shape[1]
    tm = min(tm, T)
    return pl.pallas_call(
        _inproj_kernel,
        grid=(T // tm, N // tn),
        in_specs=[
            pl.BlockSpec((tm, D), lambda i, j: (i, 0)),
            pl.BlockSpec((1, D), lambda i, j: (0, 0)),
            pl.BlockSpec((D, tn), lambda i, j: (0, j)),
        ],
        out_specs=pl.BlockSpec((tm, tn), lambda i, j: (i, j)),
        out_shape=jax.ShapeDtypeStruct((T, N), BF16),
        scratch_shapes=[pltpu.VMEM((tm, D), BF16)],
        compiler_params=_cparams(("parallel", "arbitrary")),
        name="inproj",
    )(x, nw, w)


def _gelu_tanh(x):
    c = np.float32(np.sqrt(2.0 / np.pi))
    return 0.5 * x * (1.0 + jnp.tanh(c * (x + np.float32(0.044715) * (x * x * x))))


def _compress_kernel(x_ref, pe_ref, w1_ref, b1_ref, w2_ref, b2_ref, o_ref):
    nch = x_ref.shape[2]
    half = CMP_STRIDE * A_DH
    x = x_ref[0, 0].astype(F32)
    pe = pe_ref[0]
    xlo = (x + pe[0:1, :]).astype(BF16)
    xhi = (x + pe[1:2, :]).astype(BF16)
    a = _dot(xlo, w1_ref[0, 0:half, :])
    b = _dot(xhi, w1_ref[0, half:2 * half, :])
    b_next = pltpu.roll(b, nch - 1, axis=0)
    row = lax.broadcasted_iota(jnp.int32, b.shape, 0)
    b_next = jnp.where(row < nch - 1, b_next, 0.0)
    hid = _gelu_tanh(a + b_next + b1_ref[0])
    out = _dot(hid.astype(BF16), w2_ref[0]) + b2_ref[0]
    o_ref[0, 0] = out.astype(o_ref.dtype)


def _compress(xc, pe, w1, b1, w2, b2):
    _, G, nch, width = xc.shape
    return pl.pallas_call(
        _compress_kernel,
        grid=(2, G),
        in_specs=[
            pl.BlockSpec((1, 1, nch, width), lambda s, g: (s, g, 0, 0)),
            pl.BlockSpec((1, 2, width), lambda s, g: (s, 0, 0)),
            pl.BlockSpec((1, 2 * width, CMP_HIDDEN), lambda s, g: (s, 0, 0)),
            pl.BlockSpec((1, 1, CMP_HIDDEN), lambda s, g: (s, 0, 0)),
            pl.BlockSpec((1, CMP_HIDDEN, A_DH), lambda s, g: (s, 0, 0)),
            pl.BlockSpec((1, 1, A_DH), lambda s, g: (s, 0, 0)),
        ],
        out_specs=pl.BlockSpec((1, 1, nch, A_DH), lambda s, g: (s, g, 0, 0)),
        out_shape=jax.ShapeDtypeStruct((2, G, nch, A_DH), BF16),
        compiler_params=_cparams(("parallel", "parallel")),
        name="compress",
    )(xc, pe, w1, b1, w2, b2)


def _cmp_attn_kernel(q_ref, kc_ref, vc_ref, o_ref, qsel_ref):
    tq = q_ref.shape[0]
    nch = kc_ref.shape[2]
    t0 = pl.program_id(1) * tq
    kc = kc_ref[0, 0]
    vc = vc_ref[0, 0]
    tpos = t0 + lax.broadcasted_iota(jnp.int32, (tq, nch), 0)
    cidx = lax.broadcasted_iota(jnp.int32, (tq, nch), 1)
    cmask = cidx * CMP_STRIDE + (CMP_BLOCK - 1) <= tpos
    psum = jnp.zeros((tq, nch), F32)
    for h in range(A_HPG):
        qh = q_ref[:, h * A_DH:(h + 1) * A_DH]
        s = jnp.where(cmask, _dot_nt(qh, kc), -jnp.inf)
        m = jnp.max(s, axis=-1, keepdims=True)
        m = jnp.where(m > -jnp.inf, m, 0.0)
        p = jnp.exp2(s - m)
        denom = jnp.sum(p, axis=-1, keepdims=True)
        inv = 1.0 / jnp.where(denom > 0, denom, 1.0)
        o_ref[:, h * A_DH:(h + 1) * A_DH] = (_dot(p.astype(BF16), vc) * inv).astype(o_ref.dtype)
        psum = psum + p * inv

    jj = lax.broadcasted_iota(jnp.int32, (NB_PAD, nch), 0)
    cs = lax.broadcasted_iota(jnp.int32, (NB_PAD, nch), 1) * CMP_STRIDE
    ov = ((cs <= jj * SEL_BLOCK + SEL_BLOCK - 1) & (cs + CMP_BLOCK - 1 >= jj * SEL_BLOCK))
    ov = jnp.where(ov, 1.0, 0.0).astype(BF16)
    p_hi, p_lo = _split_bf16(psum)
    pslc = _dot_nt(ov, p_hi) + _dot_nt(ov, p_lo)

    jf = lax.broadcasted_iota(jnp.int32, (NB_PAD, tq), 0)
    tl = t0 + lax.broadcasted_iota(jnp.int32, (NB_PAD, tq), 1)
    cur = jnp.right_shift(tl, SEL_SHIFT)
    valid = jf <= cur
    forced = (jf == 0) | (jf == cur) | (jf == cur - 1)
    score = jnp.where(forced, jnp.inf, jnp.where(valid, pslc, -jnp.inf))
    jff = jf.astype(F32)
    sel = jnp.zeros((NB_PAD, tq), F32)
    for _ in range(SEL_TOP_N):
        mx = jnp.max(score, axis=0, keepdims=True)
        first = jnp.min(jnp.where(score == mx, jff, float(NB_PAD)), axis=0, keepdims=True)
        pick = jff == first
        sel = jnp.where(pick, 1.0, sel)
        score = jnp.where(pick, -jnp.inf, score)
    neg = jnp.where((sel > 0.5) & valid, 0.0, -MASK_BIG)
    qsel_ref[...] = neg.T.astype(qsel_ref.dtype)


def _cmp_attn(cols, kvc, tq=256):
    T = cols.shape[0]
    G = A_GROUPS
    nch = kvc.shape[2]
    tq = min(tq, T)
    qw = A_HPG * A_DH
    return pl.pallas_call(
        _cmp_attn_kernel,
        grid=(G, T // tq),
        in_specs=[
            pl.BlockSpec((tq, qw), lambda g, i: (i, COL["q_a"] // qw + g)),
            pl.BlockSpec((1, 1, nch, A_DH), lambda g, i: (0, g, 0, 0)),
            pl.BlockSpec((1, 1, nch, A_DH), lambda g, i: (1, g, 0, 0)),
        ],
        out_specs=[
            pl.BlockSpec((tq, qw), lambda g, i: (i, g)),
            pl.BlockSpec((tq, NB_PAD), lambda g, i: (i, g)),
        ],
        out_shape=[
            jax.ShapeDtypeStruct((T, A_HEADS * A_DH), BF16),
            jax.ShapeDtypeStruct((T, G * NB_PAD), BF16),
        ],
        compiler_params=_cparams(("parallel", "parallel")),
        name="cmp_attn",
    )(cols, kvc, kvc)


def _sel_attn_kernel(q_ref, qsel_ref, k_ref, v_ref, oh_ref, o_ref, qa_ref, m_ref, l_ref, acc_ref):
    tq = q_ref.shape[0]
    tk = k_ref.shape[0]
    i = pl.program_id(1)
    kt = pl.program_id(2)
    last_kt = (i * tq + tq - 1) // tk

    nlc = tk // LANES

    @pl.when(kt == 0)
    def _():
        for h in range(A_HPG):
            qa_ref[h * tq:(h + 1) * tq, 0:A_DH] = q_ref[:, h * A_DH:(h + 1) * A_DH]
            qa_ref[h * tq:(h + 1) * tq, A_DH:2 * A_DH] = qsel_ref[...]
        m_ref[...] = jnp.full(m_ref.shape, -jnp.inf, F32)
        l_ref[...] = jnp.zeros(l_ref.shape, F32)
        acc_ref[...] = jnp.zeros(acc_ref.shape, F32)

    def step(diagonal):
        ka = jnp.concatenate([k_ref[...], oh_ref[...]], axis=1)
        s = _dot_nt(qa_ref[...], ka)
        sc = [s[:, c * LANES:(c + 1) * LANES] for c in range(nlc)]
        if diagonal:
            tpos = i * tq + lax.broadcasted_iota(jnp.int32, (tq, LANES), 0)
            lane = kt * tk + lax.broadcasted_iota(jnp.int32, (tq, LANES), 1)
            for c in range(nlc):
                causal = lane + c * LANES <= tpos
                sc[c] = jnp.concatenate(
                    [jnp.where(causal, sc[c][h * tq:(h + 1) * tq], -MASK_BIG) for h in range(A_HPG)], axis=0)
        mx = sc[0]
        for c in range(1, nlc):
            mx = jnp.maximum(mx, sc[c])
        m_old = m_ref[...]
        m_new = jnp.maximum(m_old, jnp.max(mx, axis=-1, keepdims=True))
        alpha = jnp.exp2(m_old - m_new)
        pc = [jnp.exp2(sc[c] - m_new) for c in range(nlc)]
        ps = pc[0]
        for c in range(1, nlc):
            ps = ps + pc[c]
        l_ref[...] = alpha * l_ref[...] + ps
        p = jnp.concatenate([x.astype(BF16) for x in pc], axis=1)
        acc_ref[...] = alpha * acc_ref[...] + _dot(p, v_ref[...])
        m_ref[...] = m_new

    @pl.when(kt < last_kt)
    def _():
        step(False)

    @pl.when(kt == last_kt)
    def _():
        step(True)

    @pl.when(kt == pl.num_programs(2) - 1)
    def _():
        for h in range(A_HPG):
            l = jnp.sum(l_ref[h * tq:(h + 1) * tq], axis=-1, keepdims=True)
            o = acc_ref[h * tq:(h + 1) * tq] / l
            o_ref[:, h * A_DH:(h + 1) * A_DH] = o.astype(o_ref.dtype)


def _sel_attn(cols, qsel, tq=256, tk=512):
    T = cols.shape[0]
    G = A_GROUPS
    tq = min(tq, T)
    tk = min(tk, T)
    qw = A_HPG * A_DH
    rows = A_HPG * tq

    def kv_map(col, per_group):
        def f(g, i, kt):
            return (jnp.minimum(kt, (i * tq + tq - 1) // tk), col // A_DH + g * per_group)
        return f

    onehot = jnp.asarray(np.arange(T)[:, None] // SEL_BLOCK == np.arange(NB_PAD)[None, :], dtype=BF16)

    return pl.pallas_call(
        _sel_attn_kernel,
        grid=(G, T // tq, T // tk),
        in_specs=[
            pl.BlockSpec((tq, qw), lambda g, i, kt: (i, COL["q_a"] // qw + g)),
            pl.BlockSpec((tq, NB_PAD), lambda g, i, kt: (i, g)),
            pl.BlockSpec((tk, A_DH), kv_map(COL["k_sel"], 1)),
            pl.BlockSpec((tk, A_DH), kv_map(COL["v_sel"], 1)),
            pl.BlockSpec((tk, NB_PAD), kv_map(0, 0)),
        ],
        out_specs=pl.BlockSpec((tq, qw), lambda g, i, kt: (i, g)),
        out_shape=jax.ShapeDtypeStruct((T, A_HEADS * A_DH), BF16),
        scratch_shapes=[
            pltpu.VMEM((rows, 2 * A_DH), BF16),
            pltpu.VMEM((rows, LANES), F32),
            pltpu.VMEM((rows, LANES), F32),
            pltpu.VMEM((rows, A_DH), F32),
        ],
        compiler_params=_cparams(("parallel", "parallel", "arbitrary")),
        name="sel_attn",
    )(cols, qsel, cols, cols, onehot)


def _win_attn_kernel(q_ref, k0_ref, k1_ref, k2_ref, v0_ref, v1_ref, v2_ref, o_ref):
    tq = q_ref.shape[0]
    i = pl.program_id(1)
    nk = 3 * tq
    k = jnp.concatenate([k0_ref[...], k1_ref[...], k2_ref[...]], axis=0)
    v = jnp.concatenate([v0_ref[...], v1_ref[...], v2_ref[...]], axis=0)
    r = lax.broadcasted_iota(jnp.int32, (tq, nk), 0)
    c = lax.broadcasted_iota(jnp.int32, (tq, nk), 1)
    rel = 2 * tq + r - c
    kpos = (i - 2) * tq + c
    mask = (rel >= 0) & (rel < WIN_SIZE) & (kpos >= 0)
    for h in range(A_HPG):
        qh = q_ref[:, h * A_DH:(h + 1) * A_DH]
        s = jnp.where(mask, _dot_nt(qh, k), -jnp.inf)
        m = jnp.max(s, axis=-1, keepdims=True)
        p = jnp.exp2(s - m)
        denom = jnp.sum(p, axis=-1, keepdims=True)
        o = _dot(p.astype(BF16), v) / denom
        o_ref[:, h * A_DH:(h + 1) * A_DH] = o.astype(o_ref.dtype)


def _win_attn(cols, tq=256):
    T = cols.shape[0]
    G = A_GROUPS
    tq = min(tq, T)
    assert WIN_SIZE <= 2 * tq
    qw = A_HPG * A_DH

    def kv_spec(col, back):
        return pl.BlockSpec((tq, A_DH), lambda g, i: (jnp.maximum(i - back, 0), col // A_DH + g))

    return pl.pallas_call(
        _win_attn_kernel,
        grid=(G, T // tq),
        in_specs=[pl.BlockSpec((tq, qw), lambda g, i: (i, COL["q_a"] // qw + g)),
                  kv_spec(COL["k_win"], 2), kv_spec(COL["k_win"], 1), kv_spec(COL["k_win"], 0),
                  kv_spec(COL["v_win"], 2), kv_spec(COL["v_win"], 1), kv_spec(COL["v_win"], 0)],
        out_specs=pl.BlockSpec((tq, qw), lambda g, i: (i, g)),
        out_shape=jax.ShapeDtypeStruct((T, A_HEADS * A_DH), BF16),
        compiler_params=_cparams(("parallel", "parallel")),
        name="win_attn",
    )(cols, cols, cols, cols, cols, cols, cols)


def _gla_kernel(q_ref, k_ref, v_ref, a_ref, aw_ref, ab_ref, nw_ref, o_ref, st_ref):
    tc = q_ref.shape[0]
    C = B_CHUNK

    @pl.when(pl.program_id(1) == 0)
    def _():
        st_ref[...] = jnp.zeros(st_ref.shape, F32)

    logits = _dot(a_ref[...], aw_ref[...]) + ab_ref[...]
    log_a = jax.nn.log_sigmoid(logits) * (1.0 / B_GATE_TEMP)
    ri = lax.broadcasted_iota(jnp.int32, (C, C), 0)
    ci = lax.broadcasted_iota(jnp.int32, (C, C), 1)
    tril = ci <= ri
    tril_b = jnp.where(tril, 1.0, 0.0).astype(BF16)
    nw = nw_ref[...]
    for c in range(tc // C):
        sl = slice(c * C, (c + 1) * C)
        la_hi, la_lo = _split_bf16(log_a[sl])
        cum = _dot(tril_b, la_hi) + _dot(tril_b, la_lo)
        last = cum[C - 1:C, :]
        qc = q_ref[sl, :].astype(F32)
        kc = k_ref[sl, :].astype(F32)
        vc = v_ref[sl, :]
        q_dec = (qc * jnp.exp(cum)).astype(BF16)
        k_inv = (kc * jnp.exp(-cum)).astype(BF16)
        k_state = (kc * jnp.exp(last - cum)).astype(BF16)
        att = jnp.where(tril, _dot_nt(q_dec, k_inv), 0.0)
        st = st_ref[...]
        o = _dot(att.astype(BF16), vc) + _dot_nt(q_dec, st.astype(BF16))
        st_ref[...] = st * jnp.exp(last) + _dot_tn(vc, k_state)
        o = o * lax.rsqrt(jnp.mean(o * o, axis=-1, keepdims=True) + NORM_EPS) * nw
        o_ref[sl, :] = o.astype(o_ref.dtype)


def _gla(cols, aw, ab, nw, tc=512):
    T = cols.shape[0]
    tc = min(tc, T)
    return pl.pallas_call(
        _gla_kernel,
        grid=(B_HEADS, T // tc),
        in_specs=[
            pl.BlockSpec((tc, B_DK), lambda h, i: (i, COL["q_b"] // B_DK + h)),
            pl.BlockSpec((tc, B_DK), lambda h, i: (i, COL["k_b"] // B_DK + h)),
            pl.BlockSpec((tc, B_DV), lambda h, i: (i, COL["v_b"] // B_DV + h)),
            pl.BlockSpec((tc, LANES), lambda h, i: (i, COL["a_lr"] // LANES)),
            pl.BlockSpec((LANES, B_DK), lambda h, i: (0, h)),
            pl.BlockSpec((1, B_DK), lambda h, i: (0, h)),
            pl.BlockSpec((1, B_DV), lambda h, i: (0, 0)),
        ],
        out_specs=pl.BlockSpec((tc, B_DV), lambda h, i: (i, h)),
        out_shape=jax.ShapeDtypeStruct((T, B_HEADS * B_DV), BF16),
        scratch_shapes=[pltpu.VMEM((B_DV, B_DK), F32)],
        compiler_params=_cparams(("parallel", "arbitrary")),
        name="gla",
    )(cols, cols, cols, cols, aw, ab, nw)


def _silu(x):
    return x * jax.nn.sigmoid(x)


def _merge_kernel(ocmp_ref, osel_ref, owin_ref, gbr_ref, za_ref, ob_ref, zb_ref, ga_ref, gb_ref,
                  pa_ref, pb_ref, y_ref, ua_ref):
    sg = jax.nn.sigmoid(gbr_ref[...].astype(F32))
    for hh in range(A_HEADS):
        cs = slice(hh * A_DH, (hh + 1) * A_DH)
        oa = (sg[:, 3 * hh:3 * hh + 1] * ocmp_ref[:, cs].astype(F32)
              + sg[:, 3 * hh + 1:3 * hh + 2] * osel_ref[:, cs].astype(F32)
              + sg[:, 3 * hh + 2:3 * hh + 3] * owin_ref[:, cs].astype(F32))
        ua_ref[:, cs] = (oa * _silu(za_ref[:, cs].astype(F32))).astype(BF16)
    ya = _dot(ua_ref[...], pa_ref[...])
    ub = (ob_ref[...].astype(F32) * _silu(zb_ref[...].astype(F32))).astype(BF16)
    yb = _dot(ub, pb_ref[...])
    y = jax.nn.sigmoid(ga_ref[...].astype(F32)) * ya + jax.nn.sigmoid(gb_ref[...].astype(F32)) * yb
    y_ref[...] = y.astype(y_ref.dtype)


def _merge(o_cmp, o_sel, o_win, cols, o_b, pa, pb, tm=256):
    T = cols.shape[0]
    D = D_MODEL
    tm = min(tm, T)
    row = lambda c: pl.BlockSpec((tm, D), lambda i: (i, c))
    const = lambda: pl.BlockSpec((D, D), lambda i: (0, 0), pipeline_mode=pl.Buffered(1))
    return pl.pallas_call(
        _merge_kernel,
        grid=(T // tm,),
        in_specs=[row(0), row(0), row(0),
                  pl.BlockSpec((tm, LANES), lambda i: (i, COL["g_br"] // LANES)),
                  row(COL["z_a"] // D), row(0), row(COL["z_b"] // D),
                  row(COL["gate_a"] // D), row(COL["gate_b"] // D),
                  const(), const()],
        out_specs=row(0),
        out_shape=jax.ShapeDtypeStruct((T, D), BF16),
        scratch_shapes=[pltpu.VMEM((tm, D), BF16)],
        compiler_params=_cparams(("parallel",)),
        name="merge",
    )(o_cmp, o_sel, o_win, cols, cols, o_b, cols, cols, cols, pa, pb)


def _outproj_kernel(x_ref, y_ref, w_ref, fw_ref, o_ref, *, final_norm):
    xn = x_ref[...] + _dot(y_ref[...], w_ref[...])
    if final_norm:
        ms = jnp.mean(xn * xn, axis=-1, keepdims=True)
        xn = xn * lax.rsqrt(ms + NORM_EPS) * fw_ref[...]
    o_ref[...] = xn


def _outproj(x, y, w, fw, final_norm, tm=512):
    T, D = x.shape
    tm = min(tm, T)
    return pl.pallas_call(
        functools.partial(_outproj_kernel, final_norm=final_norm),
        grid=(T // tm,),
        in_specs=[
            pl.BlockSpec((tm, D), lambda i: (i, 0)),
            pl.BlockSpec((tm, D), lambda i: (i, 0)),
            pl.BlockSpec((D, D), lambda i: (0, 0), pipeline_mode=pl.Buffered(1)),
            pl.BlockSpec((1, D), lambda i: (0, 0)),
        ],
        out_specs=pl.BlockSpec((tm, D), lambda i: (i, 0)),
        out_shape=jax.ShapeDtypeStruct((T, D), F32),
        compiler_params=_cparams(("parallel",)),
        name="outproj",
    )(x, y, w, fw)


def _prep_w_in(w_in):
    parts = []
    for name, src, w, pw in _SEGS:
        seg = w_in[:, :, src:src + w]
        if name == "q_a":
            seg = seg * np.float32(A_DH ** -0.5 * LOG2E)
        elif name == "q_b":
            seg = seg * np.float32(B_DK ** -0.5)
        if pw > w:
            seg = jnp.pad(seg, ((0, 0), (0, 0), (0, pw - w)))
        parts.append(seg)
    return jnp.concatenate(parts, axis=-1).astype(BF16)


def kernel(x, norm_w, w_in, cmp_pe, cmp_w1, cmp_b1, cmp_w2, cmp_b2, gla_alpha_w, gla_alpha_b, gla_norm_w,
           p_a, p_b, w_out, final_norm_w):
    B, T, D = x.shape
    L = norm_w.shape[0]
    assert B == 1 and D == D_MODEL and T % 512 == 0 and T // SEL_BLOCK <= NB_PAD
    G = A_GROUPS
    nch = T // CMP_STRIDE

    w_in_p = _prep_w_in(w_in)
    pe_r = cmp_pe.reshape(L, 2, 2, CMP_STRIDE * A_DH)
    w1_b = cmp_w1.astype(BF16)
    w2_b = cmp_w2.astype(BF16)
    b1_r = cmp_b1.reshape(L, 2, 1, CMP_HIDDEN)
    b2_r = cmp_b2.reshape(L, 2, 1, A_DH)
    aw_p = jnp.pad(gla_alpha_w, ((0, 0), (0, LANES - B_RANK), (0, 0))).astype(BF16)
    ab_r = gla_alpha_b.reshape(L, 1, B_HEADS * B_DK)
    gnw_r = gla_norm_w.reshape(L, 1, B_DV)
    pa_b = p_a.astype(BF16)
    pb_b = p_b.astype(BF16)
    wo_b = w_out.astype(BF16)
    fw = final_norm_w.reshape(1, D)

    xs = x.reshape(T, D)
    for l in range(L):
        cols = _inproj(xs, norm_w[l].reshape(1, D), w_in_p[l])
        kv = cols[:, COL["k_cmp"]:COL["k_cmp"] + 2 * A_KVW]
        xc = kv.reshape(nch, CMP_STRIDE, 2, G, A_DH).transpose(2, 3, 0, 1, 4).reshape(2, G, nch, CMP_STRIDE * A_DH)
        kvc = _compress(xc, pe_r[l], w1_b[l], b1_r[l], w2_b[l], b2_r[l])
        o_cmp, qsel = _cmp_attn(cols, kvc)
        o_sel = _sel_attn(cols, qsel)
        o_win = _win_attn(cols)
        o_b = _gla(cols, aw_p[l], ab_r[l], gnw_r[l])
        y = _merge(o_cmp, o_sel, o_win, cols, o_b, pa_b[l], pb_b[l])
        xs = _outproj(xs, y, wo_b[l], fw, final_norm=(l == L - 1))
    return xs.reshape(B, T, D)
```

```python
import functools

import jax
import jax.numpy as jnp
import numpy as np
from jax import lax
from jax.experimental import pallas as pl
from jax.experimental.pallas import tpu as pltpu

F32 = jnp.float32
BF16 = jnp.bfloat16

D_MODEL = 2048
A_HEADS = 16
A_GROUPS = 4
A_HPG = A_HEADS // A_GROUPS
A_DH = 128
A_KVW = A_GROUPS * A_DH
CMP_BLOCK = 32
CMP_STRIDE = 16
CMP_HIDDEN = 256
SEL_BLOCK = 64
SEL_SHIFT = 6
SEL_TOP_N = 16
WIN_SIZE = 512
B_HEADS = 4
B_DK = 256
B_DV = 512
B_RANK = 16
B_GATE_TEMP = 16.0
B_CHUNK = 64
NORM_EPS = 1e-6

LANES = 128
NB_PAD = LANES
MASK_BIG = 2.0 ** 100
LOG2E = float(np.log2(np.e))

SRC_Q_A = 0
SRC_KV_CMP = 2048
SRC_KV_SW = 3072
SRC_G_BR = 5120
SRC_R2 = 5168
SRC_R3 = 11328
SRC_END = 17472
R2_PAD = 6272
KVSW_K_SEL, KVSW_V_SEL, KVSW_K_WIN, KVSW_V_WIN = 0, 512, 1024, 1536
R2_Z_A, R2_Q_B, R2_K_B, R2_V_B, R2_A_LR = 0, 2048, 3072, 4096, 6144
R3_Z_B, R3_GATE_A, R3_GATE_B = 0, 2048, 4096

VMEM_LIMIT = 56 * 1024 * 1024


def _cparams(sem):
    return pltpu.CompilerParams(dimension_semantics=sem, vmem_limit_bytes=VMEM_LIMIT)


def _dot(a, b):
    return jnp.dot(a, b, preferred_element_type=F32)


def _dot_nt(a, b):
    return lax.dot_general(a, b, (((1,), (1,)), ((), ())), preferred_element_type=F32)


def _dot_tn(a, b):
    return lax.dot_general(a, b, (((0,), (0,)), ((), ())), preferred_element_type=F32)


def _split_bf16(x):
    hi = x.astype(BF16)
    lo = (x - hi.astype(F32)).astype(BF16)
    return hi, lo


def _rmsnorm_kernel(x_ref, w_ref, o_ref):
    x = x_ref[...]
    ms = jnp.mean(x * x, axis=-1, keepdims=True)
    o_ref[...] = (x * lax.rsqrt(ms + NORM_EPS) * w_ref[...]).astype(o_ref.dtype)


def _rmsnorm(x, w, tm=512):
    T, D = x.shape
    tm = min(tm, T)
    return pl.pallas_call(
        _rmsnorm_kernel,
        grid=(T // tm,),
        in_specs=[pl.BlockSpec((tm, D), lambda i: (i, 0)), pl.BlockSpec((1, D), lambda i: (0, 0))],
        out_specs=pl.BlockSpec((tm, D), lambda i: (i, 0)),
        out_shape=jax.ShapeDtypeStruct((T, D), BF16),
        compiler_params=_cparams(("parallel",)),
        name="rmsnorm",
    )(x, w)


def _proj_f32w_kernel(h_ref, w_ref, o_ref, wb_ref, *, scale):
    @pl.when(pl.program_id(1) == 0)
    def _():
        wb_ref[...] = w_ref[...].astype(BF16)

    r = _dot(h_ref[...], wb_ref[...])
    if scale != 1.0:
        r = r * scale
    o_ref[...] = r.astype(o_ref.dtype)


def _proj_f32w(h, w_in, layer, col0, width, tn, out_dtype, scale=1.0, tm=2048):
    T, D = h.shape
    tm = min(tm, T)
    assert col0 % tn == 0 and width % tn == 0
    return pl.pallas_call(
        functools.partial(_proj_f32w_kernel, scale=scale),
        grid=(width // tn, T // tm),
        in_specs=[
            pl.BlockSpec((tm, D), lambda j, i: (i, 0)),
            pl.BlockSpec((None, D, tn), lambda j, i: (layer, 0, col0 // tn + j)),
        ],
        out_specs=pl.BlockSpec((tm, tn), lambda j, i: (i, j)),
        out_shape=jax.ShapeDtypeStruct((T, width), out_dtype),
        scratch_shapes=[pltpu.VMEM((D, tn), BF16)],
        compiler_params=_cparams(("arbitrary", "arbitrary")),
        name="proj_f32w",
    )(h, w_in)


def _proj_bf16w_kernel(h_ref, w_ref, o_ref):
    o_ref[...] = _dot(h_ref[...], w_ref[...]).astype(o_ref.dtype)


def _proj_bf16w(h, w, layer, tn, tm=2048):
    T, D = h.shape
    N = w.shape[2]
    tm = min(tm, T)
    return pl.pallas_call(
        _proj_bf16w_kernel,
        grid=(N // tn, T // tm),
        in_specs=[
            pl.BlockSpec((tm, D), lambda j, i: (i, 0)),
            pl.BlockSpec((None, D, tn), lambda j, i: (layer, 0, j)),
        ],
        out_specs=pl.BlockSpec((tm, tn), lambda j, i: (i, j)),
        out_shape=jax.ShapeDtypeStruct((T, N), BF16),
        compiler_params=_cparams(("parallel", "parallel")),
        name="proj_bf16w",
    )(h, w)


def _gelu_tanh(x):
    c = np.float32(np.sqrt(2.0 / np.pi))
    return 0.5 * x * (1.0 + jnp.tanh(c * (x + np.float32(0.044715) * (x * x * x))))


def _compress_kernel(x_ref, pe_ref, w1_ref, b1_ref, w2_ref, b2_ref, o_ref, xlo_ref, xhi_ref):
    nch = o_ref.shape[2]
    half = CMP_STRIDE * A_DH
    for p in range(CMP_STRIDE):
        tok = x_ref[pl.ds(p, nch, stride=CMP_STRIDE), :]
        cs = slice(p * A_DH, (p + 1) * A_DH)
        xlo_ref[:, cs] = (tok + pe_ref[0, p:p + 1, :]).astype(BF16)
        xhi_ref[:, cs] = (tok + pe_ref[0, CMP_STRIDE + p:CMP_STRIDE + p + 1, :]).astype(BF16)
    a = _dot(xlo_ref[...], w1_ref[0, 0:half, :])
    b = _dot(xhi_ref[...], w1_ref[0, half:2 * half, :])
    b_next = pltpu.roll(b, nch - 1, axis=0)
    row = lax.broadcasted_iota(jnp.int32, b.shape, 0)
    b_next = jnp.where(row < nch - 1, b_next, 0.0)
    hid = _gelu_tanh(a + b_next + b1_ref[0])
    out = _dot(hid.astype(BF16), w2_ref[0]) + b2_ref[0]
    o_ref[0, 0] = out.astype(o_ref.dtype)


def _compress(kv, pe, w1, b1, w2, b2):
    T = kv.shape[0]
    G = A_GROUPS
    nch = T // CMP_STRIDE
    width = CMP_STRIDE * A_DH
    return pl.pallas_call(
        _compress_kernel,
        grid=(2, G),
        in_specs=[
            pl.BlockSpec((T, A_DH), lambda s, g: (0, s * G + g)),
            pl.BlockSpec((1, CMP_BLOCK, A_DH), lambda s, g: (s, 0, 0)),
            pl.BlockSpec((1, 2 * width, CMP_HIDDEN), lambda s, g: (s, 0, 0)),
            pl.BlockSpec((1, 1, CMP_HIDDEN), lambda s, g: (s, 0, 0)),
            pl.BlockSpec((1, CMP_HIDDEN, A_DH), lambda s, g: (s, 0, 0)),
            pl.BlockSpec((1, 1, A_DH), lambda s, g: (s, 0, 0)),
        ],
        out_specs=pl.BlockSpec((1, 1, nch, A_DH), lambda s, g: (s, g, 0, 0)),
        out_shape=jax.ShapeDtypeStruct((2, G, nch, A_DH), BF16),
        scratch_shapes=[pltpu.VMEM((nch, width), BF16), pltpu.VMEM((nch, width), BF16)],
        compiler_params=_cparams(("parallel", "parallel")),
        name="compress",
    )(kv, pe, w1, b1, w2, b2)


def _cmp_attn_kernel(q_ref, kc_ref, vc_ref, o_ref, qsel_ref):
    tq = q_ref.shape[0]
    nch = kc_ref.shape[2]
    t0 = pl.program_id(1) * tq
    kc = kc_ref[0, 0]
    vc = vc_ref[0, 0]
    tpos = t0 + lax.broadcasted_iota(jnp.int32, (tq, nch), 0)
    cidx = lax.broadcasted_iota(jnp.int32, (tq, nch), 1)
    cmask = cidx * CMP_STRIDE + (CMP_BLOCK - 1) <= tpos
    psum = jnp.zeros((tq, nch), F32)
    for h in range(A_HPG):
        qh = q_ref[:, h * A_DH:(h + 1) * A_DH]
        s = jnp.where(cmask, _dot_nt(qh, kc), -jnp.inf)
        m = jnp.max(s, axis=-1, keepdims=True)
        m = jnp.where(m > -jnp.inf, m, 0.0)
        p = jnp.exp2(s - m)
        denom = jnp.sum(p, axis=-1, keepdims=True)
        inv = 1.0 / jnp.where(denom > 0, denom, 1.0)
        o_ref[:, h * A_DH:(h + 1) * A_DH] = (_dot(p.astype(BF16), vc) * inv).astype(o_ref.dtype)
        psum = psum + p * inv

    jj = lax.broadcasted_iota(jnp.int32, (NB_PAD, nch), 0)
    cs = lax.broadcasted_iota(jnp.int32, (NB_PAD, nch), 1) * CMP_STRIDE
    ov = ((cs <= jj * SEL_BLOCK + SEL_BLOCK - 1) & (cs + CMP_BLOCK - 1 >= jj * SEL_BLOCK))
    ov = jnp.where(ov, 1.0, 0.0).astype(BF16)
    p_hi, p_lo = _split_bf16(psum)
    pslc = _dot_nt(ov, p_hi) + _dot_nt(ov, p_lo)

    jf = lax.broadcasted_iota(jnp.int32, (NB_PAD, tq), 0)
    tl = t0 + lax.broadcasted_iota(jnp.int32, (NB_PAD, tq), 1)
    cur = jnp.right_shift(tl, SEL_SHIFT)
    valid = jf <= cur
    forced = (jf == 0) | (jf == cur) | (jf == cur - 1)
    score = jnp.where(forced, jnp.inf, jnp.where(valid, pslc, -jnp.inf))
    jff = jf.astype(F32)
    sel = jnp.zeros((NB_PAD, tq), F32)
    for _ in range(SEL_TOP_N):
        mx = jnp.max(score, axis=0, keepdims=True)
        first = jnp.min(jnp.where(score == mx, jff, float(NB_PAD)), axis=0, keepdims=True)
        pick = jff == first
        sel = jnp.where(pick, 1.0, sel)
        score = jnp.where(pick, -jnp.inf, score)
    neg = jnp.where((sel > 0.5) & valid, 0.0, -MASK_BIG)
    qsel_ref[...] = neg.T.astype(qsel_ref.dtype)


def _cmp_attn(qa, kvc, tq=256):
    T = qa.shape[0]
    G = A_GROUPS
    nch = kvc.shape[2]
    tq = min(tq, T)
    qw = A_HPG * A_DH
    return pl.pallas_call(
        _cmp_attn_kernel,
        grid=(G, T // tq),
        in_specs=[
            pl.BlockSpec((tq, qw), lambda g, i: (i, g)),
            pl.BlockSpec((1, 1, nch, A_DH), lambda g, i: (0, g, 0, 0)),
            pl.BlockSpec((1, 1, nch, A_DH), lambda g, i: (1, g, 0, 0)),
        ],
        out_specs=[
            pl.BlockSpec((tq, qw), lambda g, i: (i, g)),
            pl.BlockSpec((tq, NB_PAD), lambda g, i: (i, g)),
        ],
        out_shape=[
            jax.ShapeDtypeStruct((T, A_HEADS * A_DH), BF16),
            jax.ShapeDtypeStruct((T, G * NB_PAD), BF16),
        ],
        compiler_params=_cparams(("parallel", "parallel")),
        name="cmp_attn",
    )(qa, kvc, kvc)


def _sel_attn_kernel(q_ref, qsel_ref, k_ref, v_ref, oh_ref, o_ref, qa_ref, m_ref, l_ref, acc_ref):
    tq = q_ref.shape[0]
    tk = k_ref.shape[0]
    i = pl.program_id(1)
    kt = pl.program_id(2)
    last_kt = (i * tq + tq - 1) // tk
    nlc = tk // LANES

    @pl.when(kt == 0)
    def _():
        for h in range(A_HPG):
            qa_ref[h * tq:(h + 1) * tq, 0:A_DH] = q_ref[:, h * A_DH:(h + 1) * A_DH]
            qa_ref[h * tq:(h + 1) * tq, A_DH:2 * A_DH] = qsel_ref[...]
        m_ref[...] = jnp.full(m_ref.shape, -jnp.inf, F32)
        l_ref[...] = jnp.zeros(l_ref.shape, F32)
        acc_ref[...] = jnp.zeros(acc_ref.shape, F32)

    def step(diagonal):
        ka = jnp.concatenate([k_ref[...], oh_ref[...]], axis=1)
        s = _dot_nt(qa_ref[...], ka)
        sc = [s[:, c * LANES:(c + 1) * LANES] for c in range(nlc)]
        if diagonal:
            tpos = i * tq + lax.broadcasted_iota(jnp.int32, (tq, LANES), 0)
            lane = kt * tk + lax.broadcasted_iota(jnp.int32, (tq, LANES), 1)
            for c in range(nlc):
                causal = lane + c * LANES <= tpos
                sc[c] = jnp.concatenate(
                    [jnp.where(causal, sc[c][h * tq:(h + 1) * tq], -MASK_BIG) for h in range(A_HPG)], axis=0)
        mx = sc[0]
        for c in range(1, nlc):
            mx = jnp.maximum(mx, sc[c])
        m_old = m_ref[...]
        m_new = jnp.maximum(m_old, jnp.max(mx, axis=-1, keepdims=True))
        alpha = jnp.exp2(m_old - m_new)
        pc = [jnp.exp2(sc[c] - m_new) for c in range(nlc)]
        ps = pc[0]
        for c in range(1, nlc):
            ps = ps + pc[c]
        l_ref[...] = alpha * l_ref[...] + ps
        p = jnp.concatenate([x.astype(BF16) for x in pc], axis=1)
        acc_ref[...] = alpha * acc_ref[...] + _dot(p, v_ref[...])
        m_ref[...] = m_new

    @pl.when(kt < last_kt)
    def _():
        step(False)

    @pl.when(kt == last_kt)
    def _():
        step(True)

    @pl.when(kt == pl.num_programs(2) - 1)
    def _():
        for h in range(A_HPG):
            l = jnp.sum(l_ref[h * tq:(h + 1) * tq], axis=-1, keepdims=True)
            o = acc_ref[h * tq:(h + 1) * tq] / l
            o_ref[:, h * A_DH:(h + 1) * A_DH] = o.astype(o_ref.dtype)


def _sel_attn(qa, kvsw, qsel, tq=512, tk=512):
    T = qa.shape[0]
    G = A_GROUPS
    tq = min(tq, T)
    tk = min(tk, T)
    qw = A_HPG * A_DH
    rows = A_HPG * tq

    def kv_map(col, per_group):
        def f(g, i, kt):
            return (jnp.minimum(kt, (i * tq + tq - 1) // tk), col // A_DH + g * per_group)
        return f

    onehot = jnp.asarray(np.arange(T)[:, None] // SEL_BLOCK == np.arange(NB_PAD)[None, :], dtype=BF16)

    return pl.pallas_call(
        _sel_attn_kernel,
        grid=(G, T // tq, T // tk),
        in_specs=[
            pl.BlockSpec((tq, qw), lambda g, i, kt: (i, g)),
            pl.BlockSpec((tq, NB_PAD), lambda g, i, kt: (i, g)),
            pl.BlockSpec((tk, A_DH), kv_map(KVSW_K_SEL, 1)),
            pl.BlockSpec((tk, A_DH), kv_map(KVSW_V_SEL, 1)),
            pl.BlockSpec((tk, NB_PAD), kv_map(0, 0)),
        ],
        out_specs=pl.BlockSpec((tq, qw), lambda g, i, kt: (i, g)),
        out_shape=jax.ShapeDtypeStruct((T, A_HEADS * A_DH), BF16),
        scratch_shapes=[
            pltpu.VMEM((rows, 2 * A_DH), BF16),
            pltpu.VMEM((rows, LANES), F32),
            pltpu.VMEM((rows, LANES), F32),
            pltpu.VMEM((rows, A_DH), F32),
        ],
        compiler_params=_cparams(("parallel", "parallel", "arbitrary")),
        name="sel_attn",
    )(qa, qsel, kvsw, kvsw, onehot)


def _win_attn_kernel(q_ref, k0_ref, k1_ref, k2_ref, v0_ref, v1_ref, v2_ref, o_ref):
    tq = q_ref.shape[0]
    i = pl.program_id(1)
    nk = 3 * tq
    k = jnp.concatenate([k0_ref[...], k1_ref[...], k2_ref[...]], axis=0)
    v = jnp.concatenate([v0_ref[...], v1_ref[...], v2_ref[...]], axis=0)
    r = lax.broadcasted_iota(jnp.int32, (tq, nk), 0)
    c = lax.broadcasted_iota(jnp.int32, (tq, nk), 1)
    rel = 2 * tq + r - c
    kpos = (i - 2) * tq + c
    mask = (rel >= 0) & (rel < WIN_SIZE) & (kpos >= 0)
    for h in range(A_HPG):
        qh = q_ref[:, h * A_DH:(h + 1) * A_DH]
        s = jnp.where(mask, _dot_nt(qh, k), -jnp.inf)
        m = jnp.max(s, axis=-1, keepdims=True)
        p = jnp.exp2(s - m)
        denom = jnp.sum(p, axis=-1, keepdims=True)
        o = _dot(p.astype(BF16), v) / denom
        o_ref[:, h * A_DH:(h + 1) * A_DH] = o.astype(o_ref.dtype)


def _win_attn(qa, kvsw, tq=256):
    T = qa.shape[0]
    G = A_GROUPS
    tq = min(tq, T)
    assert WIN_SIZE <= 2 * tq
    qw = A_HPG * A_DH

    def kv_spec(col, back):
        return pl.BlockSpec((tq, A_DH), lambda g, i: (jnp.maximum(i - back, 0), col // A_DH + g))

    return pl.pallas_call(
        _win_attn_kernel,
        grid=(G, T // tq),
        in_specs=[pl.BlockSpec((tq, qw), lambda g, i: (i, g)),
                  kv_spec(KVSW_K_WIN, 2), kv_spec(KVSW_K_WIN, 1), kv_spec(KVSW_K_WIN, 0),
                  kv_spec(KVSW_V_WIN, 2), kv_spec(KVSW_V_WIN, 1), kv_spec(KVSW_V_WIN, 0)],
        out_specs=pl.BlockSpec((tq, qw), lambda g, i: (i, g)),
        out_shape=jax.ShapeDtypeStruct((T, A_HEADS * A_DH), BF16),
        compiler_params=_cparams(("parallel", "parallel")),
        name="win_attn",
    )(qa, kvsw, kvsw, kvsw, kvsw, kvsw, kvsw)


def _gla_kernel(q_ref, k_ref, v_ref, a_ref, aw_ref, ab_ref, nw_ref, o_ref, st_ref):
    tc = q_ref.shape[0]
    C = B_CHUNK

    @pl.when(pl.program_id(1) == 0)
    def _():
        st_ref[...] = jnp.zeros(st_ref.shape, F32)

    logits = _dot(a_ref[...], aw_ref[...]) + ab_ref[...]
    log_a = jax.nn.log_sigmoid(logits) * (1.0 / B_GATE_TEMP)
    ri = lax.broadcasted_iota(jnp.int32, (C, C), 0)
    ci = lax.broadcasted_iota(jnp.int32, (C, C), 1)
    tril = ci <= ri
    tril_b = jnp.where(tril, 1.0, 0.0).astype(BF16)
    nw = nw_ref[...]
    for c in range(tc // C):
        sl = slice(c * C, (c + 1) * C)
        la_hi, la_lo = _split_bf16(log_a[sl])
        cum = _dot(tril_b, la_hi) + _dot(tril_b, la_lo)
        last = cum[C - 1:C, :]
        qc = q_ref[sl, :].astype(F32) * (B_DK ** -0.5)
        kc = k_ref[sl, :].astype(F32)
        vc = v_ref[sl, :]
        q_dec = (qc * jnp.exp(cum)).astype(BF16)
        k_inv = (kc * jnp.exp(-cum)).astype(BF16)
        k_state = (kc * jnp.exp(last - cum)).astype(BF16)
        att = jnp.where(tril, _dot_nt(q_dec, k_inv), 0.0)
        st = st_ref[...]
        o = _dot(att.astype(BF16), vc) + _dot_nt(q_dec, st.astype(BF16))
        st_ref[...] = st * jnp.exp(last) + _dot_tn(vc, k_state)
        o = o * lax.rsqrt(jnp.mean(o * o, axis=-1, keepdims=True) + NORM_EPS) * nw
        o_ref[sl, :] = o.astype(o_ref.dtype)


def _gla(cols2, aw, ab, nw, tc=512):
    T = cols2.shape[0]
    tc = min(tc, T)
    return pl.pallas_call(
        _gla_kernel,
        grid=(B_HEADS, T // tc),
        in_specs=[
            pl.BlockSpec((tc, B_DK), lambda h, i: (i, R2_Q_B // B_DK + h)),
            pl.BlockSpec((tc, B_DK), lambda h, i: (i, R2_K_B // B_DK + h)),
            pl.BlockSpec((tc, B_DV), lambda h, i: (i, R2_V_B // B_DV + h)),
            pl.BlockSpec((tc, LANES), lambda h, i: (i, R2_A_LR // LANES)),
            pl.BlockSpec((LANES, B_DK), lambda h, i: (0, h)),
            pl.BlockSpec((1, B_DK), lambda h, i: (0, h)),
            pl.BlockSpec((1, B_DV), lambda h, i: (0, 0)),
        ],
        out_specs=pl.BlockSpec((tc, B_DV), lambda h, i: (i, h)),
        out_shape=jax.ShapeDtypeStruct((T, B_HEADS * B_DV), BF16),
        scratch_shapes=[pltpu.VMEM((B_DV, B_DK), F32)],
        compiler_params=_cparams(("parallel", "arbitrary")),
        name="gla",
    )(cols2, cols2, cols2, cols2, aw, ab, nw)


def _silu(x):
    return x * jax.nn.sigmoid(x)


def _merge_kernel(ocmp_ref, osel_ref, owin_ref, gbr_ref, za_ref, ob_ref, zb_ref, ga_ref, gb_ref,
                  pa_ref, pb_ref, y_ref, ua_ref):
    sg = jax.nn.sigmoid(gbr_ref[...].astype(F32))
    for hh in range(A_HEADS):
        cs = slice(hh * A_DH, (hh + 1) * A_DH)
        oa = (sg[:, 3 * hh:3 * hh + 1] * ocmp_ref[:, cs].astype(F32)
              + sg[:, 3 * hh + 1:3 * hh + 2] * osel_ref[:, cs].astype(F32)
              + sg[:, 3 * hh + 2:3 * hh + 3] * owin_ref[:, cs].astype(F32))
        ua_ref[:, cs] = (oa * _silu(za_ref[:, cs].astype(F32))).astype(BF16)
    ya = _dot(ua_ref[...], pa_ref[...])
    ub = (ob_ref[...].astype(F32) * _silu(zb_ref[...].astype(F32))).astype(BF16)
    yb = _dot(ub, pb_ref[...])
    y = jax.nn.sigmoid(ga_ref[...].astype(F32)) * ya + jax.nn.sigmoid(gb_ref[...].astype(F32)) * yb
    y_ref[...] = y.astype(y_ref.dtype)


def _merge(o_cmp, o_sel, o_win, gbr, cols2, o_b, cols3, pa, pb, tm=256):
    T = gbr.shape[0]
    D = D_MODEL
    tm = min(tm, T)
    row = lambda c: pl.BlockSpec((tm, D), lambda i: (i, c))
    const = lambda: pl.BlockSpec((D, D), lambda i: (0, 0), pipeline_mode=pl.Buffered(1))
    return pl.pallas_call(
        _merge_kernel,
        grid=(T // tm,),
        in_specs=[row(0), row(0), row(0),
                  pl.BlockSpec((tm, LANES), lambda i: (i, 0)),
                  row(R2_Z_A // D), row(0), row(R3_Z_B // D),
                  row(R3_GATE_A // D), row(R3_GATE_B // D),
                  const(), const()],
        out_specs=row(0),
        out_shape=jax.ShapeDtypeStruct((T, D), BF16),
        scratch_shapes=[pltpu.VMEM((tm, D), BF16)],
        compiler_params=_cparams(("parallel",)),
        name="merge",
    )(o_cmp, o_sel, o_win, gbr, cols2, o_b, cols3, cols3, cols3, pa, pb)


def _outproj_kernel(x_ref, y_ref, w_ref, nw_ref, *out_refs, last):
    xn = x_ref[...] + _dot(y_ref[...], w_ref[...])
    ms = jnp.mean(xn * xn, axis=-1, keepdims=True)
    hn = xn * lax.rsqrt(ms + NORM_EPS) * nw_ref[...]
    if last:
        out_refs[0][...] = hn
    else:
        out_refs[0][...] = xn
        out_refs[1][...] = hn.astype(BF16)


def _outproj(x, y, w, nw, last, tm=512):
    T, D = x.shape
    tm = min(tm, T)
    row = pl.BlockSpec((tm, D), lambda i: (i, 0))
    if last:
        out_specs, out_shape = row, jax.ShapeDtypeStruct((T, D), F32)
    else:
        out_specs = [row, row]
        out_shape = [jax.ShapeDtypeStruct((T, D), F32), jax.ShapeDtypeStruct((T, D), BF16)]
    return pl.pallas_call(
        functools.partial(_outproj_kernel, last=last),
        grid=(T // tm,),
        in_specs=[
            row, row,
            pl.BlockSpec((D, D), lambda i: (0, 0), pipeline_mode=pl.Buffered(1)),
            pl.BlockSpec((1, D), lambda i: (0, 0)),
        ],
        out_specs=out_specs,
        out_shape=out_shape,
        compiler_params=_cparams(("parallel",)),
        name="outproj",
    )(x, y, w, nw)


def kernel(x, norm_w, w_in, cmp_pe, cmp_w1, cmp_b1, cmp_w2, cmp_b2, gla_alpha_w, gla_alpha_b, gla_norm_w,
           p_a, p_b, w_out, final_norm_w):
    B, T, D = x.shape
    L = norm_w.shape[0]
    assert B == 1 and D == D_MODEL and T % 512 == 0 and T // SEL_BLOCK <= NB_PAD
    assert w_in.shape[2] == SRC_END

    w_r2 = jnp.pad(w_in[:, :, SRC_R2:SRC_R3], ((0, 0), (0, 0), (0, R2_PAD - (SRC_R3 - SRC_R2)))).astype(BF16)
    w_r3 = w_in[:, :, SRC_R3:SRC_END].astype(BF16)
    w1_b = cmp_w1.astype(BF16)
    w2_b = cmp_w2.astype(BF16)
    b1_r = cmp_b1.reshape(L, 2, 1, CMP_HIDDEN)
    b2_r = cmp_b2.reshape(L, 2, 1, A_DH)
    aw_p = jnp.pad(gla_alpha_w, ((0, 0), (0, LANES - B_RANK), (0, 0))).astype(BF16)
    ab_r = gla_alpha_b.reshape(L, 1, B_HEADS * B_DK)
    gnw_r = gla_norm_w.reshape(L, 1, B_DV)
    pa_b = p_a.astype(BF16)
    pb_b = p_b.astype(BF16)
    wo_b = w_out.astype(BF16)
    nw_next = jnp.concatenate([norm_w[1:], final_norm_w[None, :]], axis=0).reshape(L, 1, D)

    xs = x.reshape(T, D)
    h = _rmsnorm(xs, norm_w[0].reshape(1, D))
    for l in range(L):
        qa = _proj_f32w(h, w_in, l, SRC_Q_A, A_HEADS * A_DH, 512, BF16, scale=A_DH ** -0.5 * LOG2E)
        kv_cmp = _proj_f32w(h, w_in, l, SRC_KV_CMP, 2 * A_KVW, 512, F32)
        kvsw = _proj_f32w(h, w_in, l, SRC_KV_SW, 4 * A_KVW, 512, BF16)
        gbr = _proj_f32w(h, w_in, l, SRC_G_BR, LANES, LANES, BF16)
        cols2 = _proj_bf16w(h, w_r2, l, 896)
        cols3 = _proj_bf16w(h, w_r3, l, 768)
        kvc = _compress(kv_cmp, cmp_pe[l], w1_b[l], b1_r[l], w2_b[l], b2_r[l])
        o_cmp, qsel = _cmp_attn(qa, kvc)
        o_sel = _sel_attn(qa, kvsw, qsel)
        o_win = _win_attn(qa, kvsw)
        o_b = _gla(cols2, aw_p[l], ab_r[l], gnw_r[l])
        y = _merge(o_cmp, o_sel, o_win, gbr, cols2, o_b, cols3, pa_b[l], pb_b[l])
        if l == L - 1:
            xs = _outproj(xs, y, wo_b[l], nw_next[l], last=True)
        else:
            xs, h = _outproj(xs, y, wo_b[l], nw_next[l], last=False)
    return xs.reshape(B, T, D)
```

```python
import functools

import jax
import jax.numpy as jnp
import numpy as np
from jax import lax
from jax.experimental import pallas as pl
from jax.experimental.pallas import tpu as pltpu

F32 = jnp.float32
BF16 = jnp.bfloat16

D_MODEL = 2048
A_HEADS = 16
A_GROUPS = 4
A_HPG = A_HEADS // A_GROUPS
A_DH = 128
A_KVW = A_GROUPS * A_DH
CMP_BLOCK = 32
CMP_STRIDE = 16
CMP_HIDDEN = 256
SEL_BLOCK = 64
SEL_SHIFT = 6
SEL_TOP_N = 16
WIN_SIZE = 512
B_HEADS = 4
B_DK = 256
B_DV = 512
B_RANK = 16
B_GATE_TEMP = 16.0
B_CHUNK = 64
NORM_EPS = 1e-6

LANES = 128
NB_PAD = LANES
MASK_BIG = 2.0 ** 100
LOG2E = float(np.log2(np.e))

SRC_Q_A = 0
SRC_KV_CMP = 2048
SRC_KV_SW = 3072
SRC_G_BR = 5120
SRC_R2 = 5168
SRC_A_LR = 11312
SRC_R3 = 11328
SRC_END = 17472
G_BR_COLS = 48
KVSW_K_SEL, KVSW_V_SEL, KVSW_K_WIN, KVSW_V_WIN = 0, 512, 1024, 1536
R2_Z_A, R2_Q_B, R2_K_B, R2_V_B = 0, 2048, 3072, 4096
R3_Z_B, R3_GATE_A, R3_GATE_B = 0, 2048, 4096

VMEM_LIMIT = 56 * 1024 * 1024


def _cparams(sem):
    return pltpu.CompilerParams(dimension_semantics=sem, vmem_limit_bytes=VMEM_LIMIT)


def _dot(a, b):
    return jnp.dot(a, b, preferred_element_type=F32)


def _dot_nt(a, b):
    return lax.dot_general(a, b, (((1,), (1,)), ((), ())), preferred_element_type=F32)


def _dot_tn(a, b):
    return lax.dot_general(a, b, (((0,), (0,)), ((), ())), preferred_element_type=F32)


def _split_bf16(x):
    hi = x.astype(BF16)
    lo = (x - hi.astype(F32)).astype(BF16)
    return hi, lo


def _rmsnorm_kernel(x_ref, w_ref, o_ref):
    x = x_ref[...]
    ms = jnp.mean(x * x, axis=-1, keepdims=True)
    o_ref[...] = (x * lax.rsqrt(ms + NORM_EPS) * w_ref[...]).astype(o_ref.dtype)


def _rmsnorm(x, w, tm=512):
    T, D = x.shape
    tm = min(tm, T)
    return pl.pallas_call(
        _rmsnorm_kernel,
        grid=(T // tm,),
        in_specs=[pl.BlockSpec((tm, D), lambda i: (i, 0)), pl.BlockSpec((1, D), lambda i: (0, 0))],
        out_specs=pl.BlockSpec((tm, D), lambda i: (i, 0)),
        out_shape=jax.ShapeDtypeStruct((T, D), BF16),
        compiler_params=_cparams(("parallel",)),
        name="rmsnorm",
    )(x, w)


def _proj_kernel(h_ref, *refs, shift, valid, scale):
    if shift:
        wa_ref, wb_ref, o_ref, wc_ref = refs
    else:
        wa_ref, o_ref, wc_ref = refs
    tn = o_ref.shape[1]

    @pl.when(pl.program_id(1) == 0)
    def _():
        w = wa_ref[...]
        if shift:
            w = jnp.concatenate([w, wb_ref[...]], axis=1)[:, shift:shift + tn]
        if valid < tn:
            w = jnp.where(lax.broadcasted_iota(jnp.int32, w.shape, 1) < valid, w, 0.0)
        wc_ref[...] = w.astype(BF16)

    r = _dot(h_ref[...], wc_ref[...])
    if scale != 1.0:
        r = r * scale
    o_ref[...] = r.astype(o_ref.dtype)


def _proj(h, w_in, layer, col0, width, tn, out_dtype, scale=1.0, valid=None, tm=2048):
    T, D = h.shape
    tm = min(tm, T)
    shift = col0 % LANES
    a0 = col0 - shift
    valid = tn if valid is None else valid
    assert a0 % tn == 0 and width % tn == 0 and (valid == tn or width == tn)
    in_specs = [
        pl.BlockSpec((tm, D), lambda j, i: (i, 0)),
        pl.BlockSpec((None, D, tn), lambda j, i: (layer, 0, a0 // tn + j)),
    ]
    args = [h, w_in]
    if shift:
        in_specs.append(pl.BlockSpec((None, D, LANES), lambda j, i: (layer, 0, (a0 + (j + 1) * tn) // LANES)))
        args.append(w_in)
    return pl.pallas_call(
        functools.partial(_proj_kernel, shift=shift, valid=valid, scale=scale),
        grid=(width // tn, T // tm),
        in_specs=in_specs,
        out_specs=pl.BlockSpec((tm, tn), lambda j, i: (i, j)),
        out_shape=jax.ShapeDtypeStruct((T, width), out_dtype),
        scratch_shapes=[pltpu.VMEM((D, tn), BF16)],
        compiler_params=_cparams(("arbitrary", "arbitrary")),
        name="proj",
    )(*args)


def _gelu_tanh(x):
    c = np.float32(np.sqrt(2.0 / np.pi))
    return 0.5 * x * (1.0 + jnp.tanh(c * (x + np.float32(0.044715) * (x * x * x))))


def _compress_kernel(x_ref, pe_ref, w1_ref, b1_ref, w2_ref, b2_ref, o_ref, xlo_ref, xhi_ref):
    nch = o_ref.shape[2]
    half = CMP_STRIDE * A_DH
    for p in range(CMP_STRIDE):
        tok = x_ref[pl.ds(p, nch, stride=CMP_STRIDE), :]
        cs = slice(p * A_DH, (p + 1) * A_DH)
        xlo_ref[:, cs] = (tok + pe_ref[0, p:p + 1, :]).astype(BF16)
        xhi_ref[:, cs] = (tok + pe_ref[0, CMP_STRIDE + p:CMP_STRIDE + p + 1, :]).astype(BF16)
    a = _dot(xlo_ref[...], w1_ref[0, 0:half, :])
    b = _dot(xhi_ref[...], w1_ref[0, half:2 * half, :])
    b_next = pltpu.roll(b, nch - 1, axis=0)
    row = lax.broadcasted_iota(jnp.int32, b.shape, 0)
    b_next = jnp.where(row < nch - 1, b_next, 0.0)
    hid = _gelu_tanh(a + b_next + b1_ref[0])
    out = _dot(hid.astype(BF16), w2_ref[0]) + b2_ref[0]
    o_ref[0, 0] = out.astype(o_ref.dtype)


def _compress(kv, pe, w1, b1, w2, b2):
    T = kv.shape[0]
    G = A_GROUPS
    nch = T // CMP_STRIDE
    width = CMP_STRIDE * A_DH
    return pl.pallas_call(
        _compress_kernel,
        grid=(2, G),
        in_specs=[
            pl.BlockSpec((T, A_DH), lambda s, g: (0, s * G + g)),
            pl.BlockSpec((1, CMP_BLOCK, A_DH), lambda s, g: (s, 0, 0)),
            pl.BlockSpec((1, 2 * width, CMP_HIDDEN), lambda s, g: (s, 0, 0)),
            pl.BlockSpec((1, 1, CMP_HIDDEN), lambda s, g: (s, 0, 0)),
            pl.BlockSpec((1, CMP_HIDDEN, A_DH), lambda s, g: (s, 0, 0)),
            pl.BlockSpec((1, 1, A_DH), lambda s, g: (s, 0, 0)),
        ],
        out_specs=pl.BlockSpec((1, 1, nch, A_DH), lambda s, g: (s, g, 0, 0)),
        out_shape=jax.ShapeDtypeStruct((2, G, nch, A_DH), BF16),
        scratch_shapes=[pltpu.VMEM((nch, width), BF16), pltpu.VMEM((nch, width), BF16)],
        compiler_params=_cparams(("parallel", "parallel")),
        name="compress",
    )(kv, pe, w1, b1, w2, b2)


def _stack_heads(q_ref, qs_ref):
    tq = q_ref.shape[0]
    for h in range(A_HPG):
        qs_ref[h * tq:(h + 1) * tq, :] = q_ref[:, h * A_DH:(h + 1) * A_DH]


def _masked_chunks(s, masks, tq):
    out = []
    for c, mk in enumerate(masks):
        sc = s[:, c * LANES:(c + 1) * LANES]
        out.append(jnp.concatenate(
            [jnp.where(mk, sc[h * tq:(h + 1) * tq], -jnp.inf) for h in range(A_HPG)], axis=0))
    return out


def _chunk_softmax(sc):
    mx = sc[0]
    for x in sc[1:]:
        mx = jnp.maximum(mx, x)
    m = jnp.max(mx, axis=-1, keepdims=True)
    m = jnp.where(m > -jnp.inf, m, 0.0)
    pc = [jnp.exp2(x - m) for x in sc]
    ps = pc[0]
    for x in pc[1:]:
        ps = ps + x
    return pc, jnp.sum(ps, axis=-1, keepdims=True)


def _cmp_attn_kernel(q_ref, kc_ref, vc_ref, o_ref, qsel_ref, qs_ref):
    tq = q_ref.shape[0]
    nch = kc_ref.shape[2]
    t0 = pl.program_id(1) * tq
    _stack_heads(q_ref, qs_ref)
    s = _dot_nt(qs_ref[...], kc_ref[0, 0])
    tpos = t0 + lax.broadcasted_iota(jnp.int32, (tq, LANES), 0)
    lane = lax.broadcasted_iota(jnp.int32, (tq, LANES), 1)
    masks = [(lane + c * LANES) * CMP_STRIDE + (CMP_BLOCK - 1) <= tpos for c in range(nch // LANES)]
    pc, denom = _chunk_softmax(_masked_chunks(s, masks, tq))
    inv = 1.0 / jnp.where(denom > 0, denom, 1.0)
    o = _dot(jnp.concatenate([x.astype(BF16) for x in pc], axis=1), vc_ref[0, 0]) * inv
    for h in range(A_HPG):
        o_ref[:, h * A_DH:(h + 1) * A_DH] = o[h * tq:(h + 1) * tq].astype(o_ref.dtype)
    pn = [x * inv for x in pc]
    psum = jnp.concatenate(
        [sum(x[h * tq:(h + 1) * tq] for h in range(A_HPG)) for x in pn], axis=1)

    jj = lax.broadcasted_iota(jnp.int32, (NB_PAD, nch), 0)
    cs = lax.broadcasted_iota(jnp.int32, (NB_PAD, nch), 1) * CMP_STRIDE
    ov = ((cs <= jj * SEL_BLOCK + SEL_BLOCK - 1) & (cs + CMP_BLOCK - 1 >= jj * SEL_BLOCK))
    ov = jnp.where(ov, 1.0, 0.0).astype(BF16)
    p_hi, p_lo = _split_bf16(psum)
    pslc = _dot_nt(ov, p_hi) + _dot_nt(ov, p_lo)

    jf = lax.broadcasted_iota(jnp.int32, (NB_PAD, tq), 0)
    tl = t0 + lax.broadcasted_iota(jnp.int32, (NB_PAD, tq), 1)
    cur = jnp.right_shift(tl, SEL_SHIFT)
    valid = jf <= cur
    forced = (jf == 0) | (jf == cur) | (jf == cur - 1)
    score = jnp.where(forced, jnp.inf, jnp.where(valid, pslc, -jnp.inf))
    jff = jf.astype(F32)
    sel = jnp.zeros((NB_PAD, tq), F32)
    for _ in range(SEL_TOP_N):
        mx = jnp.max(score, axis=0, keepdims=True)
        first = jnp.min(jnp.where(score == mx, jff, float(NB_PAD)), axis=0, keepdims=True)
        pick = jff == first
        sel = jnp.where(pick, 1.0, sel)
        score = jnp.where(pick, -jnp.inf, score)
    neg = jnp.where((sel > 0.5) & valid, 0.0, -MASK_BIG)
    qsel_ref[...] = neg.T.astype(qsel_ref.dtype)


def _cmp_attn(qa, kvc, tq=256):
    T = qa.shape[0]
    G = A_GROUPS
    nch = kvc.shape[2]
    tq = min(tq, T)
    qw = A_HPG * A_DH
    return pl.pallas_call(
        _cmp_attn_kernel,
        grid=(G, T // tq),
        in_specs=[
            pl.BlockSpec((tq, qw), lambda g, i: (i, g)),
            pl.BlockSpec((1, 1, nch, A_DH), lambda g, i: (0, g, 0, 0)),
            pl.BlockSpec((1, 1, nch, A_DH), lambda g, i: (1, g, 0, 0)),
        ],
        out_specs=[
            pl.BlockSpec((tq, qw), lambda g, i: (i, g)),
            pl.BlockSpec((tq, NB_PAD), lambda g, i: (i, g)),
        ],
        out_shape=[
            jax.ShapeDtypeStruct((T, A_HEADS * A_DH), BF16),
            jax.ShapeDtypeStruct((T, G * NB_PAD), BF16),
        ],
        scratch_shapes=[pltpu.VMEM((A_HPG * tq, A_DH), BF16)],
        compiler_params=_cparams(("parallel", "parallel")),
        name="cmp_attn",
    )(qa, kvc, kvc)


def _sel_attn_kernel(q_ref, qsel_ref, k_ref, v_ref, oh_ref, o_ref, qa_ref, m_ref, l_ref, acc_ref):
    tq = q_ref.shape[0]
    tk = k_ref.shape[0]
    i = pl.program_id(1)
    kt = pl.program_id(2)
    last_kt = (i * tq + tq - 1) // tk
    nlc = tk // LANES

    @pl.when(kt == 0)
    def _():
        for h in range(A_HPG):
            qa_ref[h * tq:(h + 1) * tq, 0:A_DH] = q_ref[:, h * A_DH:(h + 1) * A_DH]
            qa_ref[h * tq:(h + 1) * tq, A_DH:2 * A_DH] = qsel_ref[...]
        m_ref[...] = jnp.full(m_ref.shape, -jnp.inf, F32)
        l_ref[...] = jnp.zeros(l_ref.shape, F32)
        acc_ref[...] = jnp.zeros(acc_ref.shape, F32)

    def step(diagonal):
        ka = jnp.concatenate([k_ref[...], oh_ref[...]], axis=1)
        s = _dot_nt(qa_ref[...], ka)
        sc = [s[:, c * LANES:(c + 1) * LANES] for c in range(nlc)]
        if diagonal:
            tpos = i * tq + lax.broadcasted_iota(jnp.int32, (tq, LANES), 0)
            lane = kt * tk + lax.broadcasted_iota(jnp.int32, (tq, LANES), 1)
            for c in range(nlc):
                causal = lane + c * LANES <= tpos
                sc[c] = jnp.concatenate(
                    [jnp.where(causal, sc[c][h * tq:(h + 1) * tq], -MASK_BIG) for h in range(A_HPG)], axis=0)
        mx = sc[0]
        for c in range(1, nlc):
            mx = jnp.maximum(mx, sc[c])
        m_old = m_ref[...]
        m_new = jnp.maximum(m_old, jnp.max(mx, axis=-1, keepdims=True))
        alpha = jnp.exp2(m_old - m_new)
        pc = [jnp.exp2(sc[c] - m_new) for c in range(nlc)]
        ps = pc[0]
        for c in range(1, nlc):
            ps = ps + pc[c]
        l_ref[...] = alpha * l_ref[...] + ps
        p = jnp.concatenate([x.astype(BF16) for x in pc], axis=1)
        acc_ref[...] = alpha * acc_ref[...] + _dot(p, v_ref[...])
        m_ref[...] = m_new

    @pl.when(kt < last_kt)
    def _():
        step(False)

    @pl.when(kt == last_kt)
    def _():
        step(True)

    @pl.when(kt == pl.num_programs(2) - 1)
    def _():
        for h in range(A_HPG):
            l = jnp.sum(l_ref[h * tq:(h + 1) * tq], axis=-1, keepdims=True)
            o = acc_ref[h * tq:(h + 1) * tq] / l
            o_ref[:, h * A_DH:(h + 1) * A_DH] = o.astype(o_ref.dtype)


def _sel_attn(qa, kvsw, qsel, tq=512, tk=512):
    T = qa.shape[0]
    G = A_GROUPS
    tq = min(tq, T)
    tk = min(tk, T)
    qw = A_HPG * A_DH
    rows = A_HPG * tq

    def kv_map(col, per_group):
        def f(g, i, kt):
            return (jnp.minimum(kt, (i * tq + tq - 1) // tk), col // A_DH + g * per_group)
        return f

    onehot = jnp.asarray(np.arange(T)[:, None] // SEL_BLOCK == np.arange(NB_PAD)[None, :], dtype=BF16)

    return pl.pallas_call(
        _sel_attn_kernel,
        grid=(G, T // tq, T // tk),
        in_specs=[
            pl.BlockSpec((tq, qw), lambda g, i, kt: (i, g)),
            pl.BlockSpec((tq, NB_PAD), lambda g, i, kt: (i, g)),
            pl.BlockSpec((tk, A_DH), kv_map(KVSW_K_SEL, 1)),
            pl.BlockSpec((tk, A_DH), kv_map(KVSW_V_SEL, 1)),
            pl.BlockSpec((tk, NB_PAD), kv_map(0, 0)),
        ],
        out_specs=pl.BlockSpec((tq, qw), lambda g, i, kt: (i, g)),
        out_shape=jax.ShapeDtypeStruct((T, A_HEADS * A_DH), BF16),
        scratch_shapes=[
            pltpu.VMEM((rows, 2 * A_DH), BF16),
            pltpu.VMEM((rows, LANES), F32),
            pltpu.VMEM((rows, LANES), F32),
            pltpu.VMEM((rows, A_DH), F32),
        ],
        compiler_params=_cparams(("parallel", "parallel", "arbitrary")),
        name="sel_attn",
    )(qa, qsel, kvsw, kvsw, onehot)


def _win_attn_kernel(q_ref, k0_ref, k1_ref, k2_ref, v0_ref, v1_ref, v2_ref, o_ref, qs_ref):
    tq = q_ref.shape[0]
    i = pl.program_id(1)
    k = jnp.concatenate([k0_ref[...], k1_ref[...], k2_ref[...]], axis=0)
    v = jnp.concatenate([v0_ref[...], v1_ref[...], v2_ref[...]], axis=0)
    _stack_heads(q_ref, qs_ref)
    s = _dot_nt(qs_ref[...], k)
    r = lax.broadcasted_iota(jnp.int32, (tq, LANES), 0)
    lane = lax.broadcasted_iota(jnp.int32, (tq, LANES), 1)
    masks = []
    for c in range(3 * tq // LANES):
        col = lane + c * LANES
        rel = 2 * tq + r - col
        masks.append((rel >= 0) & (rel < WIN_SIZE) & ((i - 2) * tq + col >= 0))
    pc, denom = _chunk_softmax(_masked_chunks(s, masks, tq))
    o = _dot(jnp.concatenate([x.astype(BF16) for x in pc], axis=1), v) / denom
    for h in range(A_HPG):
        o_ref[:, h * A_DH:(h + 1) * A_DH] = o[h * tq:(h + 1) * tq].astype(o_ref.dtype)


def _win_attn(qa, kvsw, tq=256):
    T = qa.shape[0]
    G = A_GROUPS
    tq = min(tq, T)
    assert WIN_SIZE <= 2 * tq
    qw = A_HPG * A_DH

    def kv_spec(col, back):
        return pl.BlockSpec((tq, A_DH), lambda g, i: (jnp.maximum(i - back, 0), col // A_DH + g))

    return pl.pallas_call(
        _win_attn_kernel,
        grid=(G, T // tq),
        in_specs=[pl.BlockSpec((tq, qw), lambda g, i: (i, g)),
                  kv_spec(KVSW_K_WIN, 2), kv_spec(KVSW_K_WIN, 1), kv_spec(KVSW_K_WIN, 0),
                  kv_spec(KVSW_V_WIN, 2), kv_spec(KVSW_V_WIN, 1), kv_spec(KVSW_V_WIN, 0)],
        out_specs=pl.BlockSpec((tq, qw), lambda g, i: (i, g)),
        out_shape=jax.ShapeDtypeStruct((T, A_HEADS * A_DH), BF16),
        scratch_shapes=[pltpu.VMEM((A_HPG * tq, A_DH), BF16)],
        compiler_params=_cparams(("parallel", "parallel")),
        name="win_attn",
    )(qa, kvsw, kvsw, kvsw, kvsw, kvsw, kvsw)


def _gla_kernel(q_ref, k_ref, v_ref, a_ref, aw_ref, ab_ref, nw_ref, o_ref, st_ref):
    tc = q_ref.shape[0]
    C = B_CHUNK

    @pl.when(pl.program_id(0) == 0)
    def _():
        st_ref[...] = jnp.zeros(st_ref.shape, F32)

    logits = _dot(a_ref[...], aw_ref[...]) + ab_ref[...]
    log_a = jax.nn.log_sigmoid(logits) * (1.0 / B_GATE_TEMP)
    ri = lax.broadcasted_iota(jnp.int32, (C, C), 0)
    ci = lax.broadcasted_iota(jnp.int32, (C, C), 1)
    tril = ci <= ri
    tril_b = jnp.where(tril, 1.0, 0.0).astype(BF16)
    nw = nw_ref[...]
    for c in range(tc // C):
        sl = slice(c * C, (c + 1) * C)
        la_hi, la_lo = _split_bf16(log_a[sl])
        cum = _dot(tril_b, la_hi) + _dot(tril_b, la_lo)
        last = cum[C - 1:C, :]
        qc = q_ref[sl, :].astype(F32) * (B_DK ** -0.5)
        kc = k_ref[sl, :].astype(F32)
        q_dec = (qc * jnp.exp(cum)).astype(BF16)
        k_inv = (kc * jnp.exp(-cum)).astype(BF16)
        k_state = (kc * jnp.exp(last - cum)).astype(BF16)
        dec = jnp.exp(last)
        for h in range(B_HEADS):
            ks = slice(h * B_DK, (h + 1) * B_DK)
            vs = slice(h * B_DV, (h + 1) * B_DV)
            vc = v_ref[sl, vs]
            att = jnp.where(tril, _dot_nt(q_dec[:, ks], k_inv[:, ks]), 0.0)
            st = st_ref[h]
            o = _dot(att.astype(BF16), vc) + _dot_nt(q_dec[:, ks], st.astype(BF16))
            st_ref[h] = st * dec[:, ks] + _dot_tn(vc, k_state[:, ks])
            o = o * lax.rsqrt(jnp.mean(o * o, axis=-1, keepdims=True) + NORM_EPS) * nw
            o_ref[sl, vs] = o.astype(o_ref.dtype)


def _gla(cols2, alr, aw, ab, nw, tc=512):
    T = cols2.shape[0]
    tc = min(tc, T)
    kw = B_HEADS * B_DK
    vw = B_HEADS * B_DV
    return pl.pallas_call(
        _gla_kernel,
        grid=(T // tc,),
        in_specs=[
            pl.BlockSpec((tc, kw), lambda i: (i, R2_Q_B // kw)),
            pl.BlockSpec((tc, kw), lambda i: (i, R2_K_B // kw)),
            pl.BlockSpec((tc, vw), lambda i: (i, R2_V_B // vw)),
            pl.BlockSpec((tc, LANES), lambda i: (i, 0)),
            pl.BlockSpec((LANES, kw), lambda i: (0, 0)),
            pl.BlockSpec((1, kw), lambda i: (0, 0)),
            pl.BlockSpec((1, B_DV), lambda i: (0, 0)),
        ],
        out_specs=pl.BlockSpec((tc, vw), lambda i: (i, 0)),
        out_shape=jax.ShapeDtypeStruct((T, vw), BF16),
        scratch_shapes=[pltpu.VMEM((B_HEADS, B_DV, B_DK), F32)],
        compiler_params=_cparams(("arbitrary",)),
        name="gla",
    )(cols2, cols2, cols2, alr, aw, ab, nw)


def _silu(x):
    return x * jax.nn.sigmoid(x)


def _merge_a_kernel(ocmp_ref, osel_ref, owin_ref, gbr_ref, za_ref, ga_ref, pa_ref, y_ref, ua_ref):
    sg = jax.nn.sigmoid(gbr_ref[...].astype(F32))
    for hh in range(A_HEADS):
        cs = slice(hh * A_DH, (hh + 1) * A_DH)
        oa = (sg[:, 3 * hh:3 * hh + 1] * ocmp_ref[:, cs].astype(F32)
              + sg[:, 3 * hh + 1:3 * hh + 2] * osel_ref[:, cs].astype(F32)
              + sg[:, 3 * hh + 2:3 * hh + 3] * owin_ref[:, cs].astype(F32))
        ua_ref[:, cs] = (oa * _silu(za_ref[:, cs].astype(F32))).astype(BF16)
    ya = _dot(ua_ref[...], pa_ref[...])
    y_ref[...] = (jax.nn.sigmoid(ga_ref[...].astype(F32)) * ya).astype(y_ref.dtype)


def _merge_b_kernel(ya_ref, ob_ref, zb_ref, gb_ref, pb_ref, y_ref):
    ub = (ob_ref[...].astype(F32) * _silu(zb_ref[...].astype(F32))).astype(BF16)
    yb = _dot(ub, pb_ref[...])
    y = ya_ref[...].astype(F32) + jax.nn.sigmoid(gb_ref[...].astype(F32)) * yb
    y_ref[...] = y.astype(y_ref.dtype)


def _merge(o_cmp, o_sel, o_win, gbr, cols2, o_b, cols3, pa, pb, tm=512):
    T = gbr.shape[0]
    D = D_MODEL
    tm = min(tm, T)
    row = lambda c: pl.BlockSpec((tm, D), lambda i: (i, c))
    const = lambda: pl.BlockSpec((D, D), lambda i: (0, 0), pipeline_mode=pl.Buffered(1))
    ya = pl.pallas_call(
        _merge_a_kernel,
        grid=(T // tm,),
        in_specs=[row(0), row(0), row(0),
                  pl.BlockSpec((tm, LANES), lambda i: (i, 0)),
                  row(R2_Z_A // D), row(R3_GATE_A // D), const()],
        out_specs=row(0),
        out_shape=jax.ShapeDtypeStruct((T, D), BF16),
        scratch_shapes=[pltpu.VMEM((tm, D), BF16)],
        compiler_params=_cparams(("parallel",)),
        name="merge_a",
    )(o_cmp, o_sel, o_win, gbr, cols2, cols3, pa)
    return pl.pallas_call(
        _merge_b_kernel,
        grid=(T // tm,),
        in_specs=[row(0), row(0), row(R3_Z_B // D), row(R3_GATE_B // D), const()],
        out_specs=row(0),
        out_shape=jax.ShapeDtypeStruct((T, D), BF16),
        compiler_params=_cparams(("parallel",)),
        name="merge_b",
    )(ya, o_b, cols3, cols3, pb)


def _outproj_kernel(x_ref, y_ref, w_ref, nw_ref, *out_refs, last):
    xn = x_ref[...] + _dot(y_ref[...], w_ref[...])
    ms = jnp.mean(xn * xn, axis=-1, keepdims=True)
    hn = xn * lax.rsqrt(ms + NORM_EPS) * nw_ref[...]
    if last:
        out_refs[0][...] = hn
    else:
        out_refs[0][...] = xn
        out_refs[1][...] = hn.astype(BF16)


def _outproj(x, y, w, nw, last, tm=512):
    T, D = x.shape
    tm = min(tm, T)
    row = pl.BlockSpec((tm, D), lambda i: (i, 0))
    if last:
        out_specs, out_shape = row, jax.ShapeDtypeStruct((T, D), F32)
    else:
        out_specs = [row, row]
        out_shape = [jax.ShapeDtypeStruct((T, D), F32), jax.ShapeDtypeStruct((T, D), BF16)]
    return pl.pallas_call(
        functools.partial(_outproj_kernel, last=last),
        grid=(T // tm,),
        in_specs=[
            row, row,
            pl.BlockSpec((D, D), lambda i: (0, 0), pipeline_mode=pl.Buffered(1)),
            pl.BlockSpec((1, D), lambda i: (0, 0)),
        ],
        out_specs=out_specs,
        out_shape=out_shape,
        compiler_params=_cparams(("parallel",)),
        name="outproj",
    )(x, y, w, nw)


def kernel(x, norm_w, w_in, cmp_pe, cmp_w1, cmp_b1, cmp_w2, cmp_b2, gla_alpha_w, gla_alpha_b, gla_norm_w,
           p_a, p_b, w_out, final_norm_w):
    B, T, D = x.shape
    L = norm_w.shape[0]
    assert B == 1 and D == D_MODEL and T % 512 == 0 and T // SEL_BLOCK <= NB_PAD
    assert w_in.shape[2] == SRC_END

    w1_b = cmp_w1.astype(BF16)
    w2_b = cmp_w2.astype(BF16)
    b1_r = cmp_b1.reshape(L, 2, 1, CMP_HIDDEN)
    b2_r = cmp_b2.reshape(L, 2, 1, A_DH)
    aw_p = jnp.pad(gla_alpha_w, ((0, 0), (0, LANES - B_RANK), (0, 0))).astype(BF16)
    ab_r = gla_alpha_b.reshape(L, 1, B_HEADS * B_DK)
    gnw_r = gla_norm_w.reshape(L, 1, B_DV)
    pa_b = p_a.astype(BF16)
    pb_b = p_b.astype(BF16)
    wo_b = w_out.astype(BF16)
    nw_next = jnp.concatenate([norm_w[1:], final_norm_w[None, :]], axis=0).reshape(L, 1, D)

    xs = x.reshape(T, D)
    h = _rmsnorm(xs, norm_w[0].reshape(1, D))
    for l in range(L):
        qa = _proj(h, w_in, l, SRC_Q_A, A_HEADS * A_DH, 512, BF16, scale=A_DH ** -0.5 * LOG2E)
        kv_cmp = _proj(h, w_in, l, SRC_KV_CMP, 2 * A_KVW, 512, F32)
        kvsw = _proj(h, w_in, l, SRC_KV_SW, 4 * A_KVW, 512, BF16)
        gbr = _proj(h, w_in, l, SRC_G_BR, LANES, LANES, BF16, valid=G_BR_COLS)
        cols2 = _proj(h, w_in, l, SRC_R2, SRC_A_LR - SRC_R2, 512, BF16)
        alr = _proj(h, w_in, l, SRC_A_LR, LANES, LANES, BF16, valid=B_RANK)
        cols3 = _proj(h, w_in, l, SRC_R3, SRC_END - SRC_R3, 512, BF16)
        kvc = _compress(kv_cmp, cmp_pe[l], w1_b[l], b1_r[l], w2_b[l], b2_r[l])
        o_cmp, qsel = _cmp_attn(qa, kvc)
        o_sel = _sel_attn(qa, kvsw, qsel)
        o_win = _win_attn(qa, kvsw)
        o_b = _gla(cols2, alr, aw_p[l], ab_r[l], gnw_r[l])
        y = _merge(o_cmp, o_sel, o_win, gbr, cols2, o_b, cols3, pa_b[l], pb_b[l])
        if l == L - 1:
            xs = _outproj(xs, y, wo_b[l], nw_next[l], last=True)
        else:
            xs, h = _outproj(xs, y, wo_b[l], nw_next[l], last=False)
    return xs.reshape(B, T, D)
```

```python
import functools

import jax
import jax.numpy as jnp
import numpy as np
from jax import lax
from jax.experimental import pallas as pl
from jax.experimental.pallas import tpu as pltpu

F32 = jnp.float32
BF16 = jnp.bfloat16

D_MODEL = 2048
A_HEADS = 16
A_GROUPS = 4
A_HPG = A_HEADS // A_GROUPS
A_DH = 128
A_KVW = A_GROUPS * A_DH
CMP_BLOCK = 32
CMP_STRIDE = 16
CMP_HIDDEN = 256
SEL_BLOCK = 64
SEL_SHIFT = 6
SEL_TOP_N = 16
WIN_SIZE = 512
B_HEADS = 4
B_DK = 256
B_DV = 512
B_RANK = 16
B_GATE_TEMP = 16.0
B_CHUNK = 64
NORM_EPS = 1e-6

LANES = 128
NB_PAD = LANES
MASK_BIG = 2.0 ** 100
LOG2E = float(np.log2(np.e))

SRC_Q_A = 0
SRC_KV_CMP = 2048
SRC_KV_SW = 3072
SRC_G_BR = 5120
SRC_R2 = 5168
SRC_A_LR = 11312
SRC_R3 = 11328
SRC_END = 17472
G_BR_COLS = 48
KVSW_K_SEL, KVSW_V_SEL, KVSW_K_WIN, KVSW_V_WIN = 0, 512, 1024, 1536
R2_Z_A, R2_Q_B, R2_K_B, R2_V_B = 0, 2048, 3072, 4096
R3_Z_B, R3_GATE_A, R3_GATE_B = 0, 2048, 4096

VMEM_LIMIT = 56 * 1024 * 1024


def _cparams(sem):
    return pltpu.CompilerParams(dimension_semantics=sem, vmem_limit_bytes=VMEM_LIMIT)


def _dot(a, b):
    return jnp.dot(a, b, preferred_element_type=F32)


def _dot_nt(a, b):
    return lax.dot_general(a, b, (((1,), (1,)), ((), ())), preferred_element_type=F32)


def _dot_tn(a, b):
    return lax.dot_general(a, b, (((0,), (0,)), ((), ())), preferred_element_type=F32)


def _split_bf16(x):
    hi = x.astype(BF16)
    lo = (x - hi.astype(F32)).astype(BF16)
    return hi, lo


def _rmsnorm_kernel(x_ref, w_ref, o_ref):
    x = x_ref[...]
    ms = jnp.mean(x * x, axis=-1, keepdims=True)
    o_ref[...] = (x * lax.rsqrt(ms + NORM_EPS) * w_ref[...]).astype(o_ref.dtype)


def _rmsnorm(x, w, tm=512):
    T, D = x.shape
    tm = min(tm, T)
    return pl.pallas_call(
        _rmsnorm_kernel,
        grid=(T // tm,),
        in_specs=[pl.BlockSpec((tm, D), lambda i: (i, 0)), pl.BlockSpec((1, D), lambda i: (0, 0))],
        out_specs=pl.BlockSpec((tm, D), lambda i: (i, 0)),
        out_shape=jax.ShapeDtypeStruct((T, D), BF16),
        compiler_params=_cparams(("parallel",)),
        name="rmsnorm",
    )(x, w)


def _proj_kernel(h_ref, *refs, shift, valid, scale):
    if shift:
        wa_ref, wb_ref, o_ref, wc_ref = refs
    else:
        wa_ref, o_ref, wc_ref = refs
    tn = o_ref.shape[1]

    @pl.when(pl.program_id(1) == 0)
    def _():
        if shift:
            wc_ref[0:tn - shift, :] = wa_ref[shift:tn, :].astype(BF16)
            wc_ref[tn - shift:tn, :] = wb_ref[0:shift, :].astype(BF16)
        else:
            wc_ref[...] = wa_ref[...].astype(BF16)
        if valid < tn:
            wc_ref[valid:tn, :] = jnp.zeros((tn - valid, wc_ref.shape[1]), BF16)

    r = _dot_nt(h_ref[...], wc_ref[...])
    if scale != 1.0:
        r = r * scale
    o_ref[...] = r.astype(o_ref.dtype)


def _proj(h, w_t, layer, col0, width, tn, out_dtype, scale=1.0, valid=None, tm=2048):
    T, D = h.shape
    tm = min(tm, T)
    shift = col0 % LANES
    a0 = col0 - shift
    valid = tn if valid is None else valid
    assert a0 % tn == 0 and width % tn == 0 and (valid == tn or width == tn)
    assert shift % 16 == 0 and valid % 16 == 0
    in_specs = [
        pl.BlockSpec((tm, D), lambda j, i: (i, 0)),
        pl.BlockSpec((None, tn, D), lambda j, i: (layer, a0 // tn + j, 0)),
    ]
    args = [h, w_t]
    if shift:
        in_specs.append(pl.BlockSpec((None, LANES, D), lambda j, i: (layer, (a0 + (j + 1) * tn) // LANES, 0)))
        args.append(w_t)
    return pl.pallas_call(
        functools.partial(_proj_kernel, shift=shift, valid=valid, scale=scale),
        grid=(width // tn, T // tm),
        in_specs=in_specs,
        out_specs=pl.BlockSpec((tm, tn), lambda j, i: (i, j)),
        out_shape=jax.ShapeDtypeStruct((T, width), out_dtype),
        scratch_shapes=[pltpu.VMEM((tn, D), BF16)],
        compiler_params=_cparams(("arbitrary", "arbitrary")),
        name="proj",
    )(*args)


def _gelu_tanh(x):
    c = np.float32(np.sqrt(2.0 / np.pi))
    return 0.5 * x * (1.0 + jnp.tanh(c * (x + np.float32(0.044715) * (x * x * x))))


def _compress_kernel(x_ref, pe_ref, w1_ref, b1_ref, w2_ref, b2_ref, o_ref, xlo_ref, xhi_ref):
    nch = o_ref.shape[2]
    half = CMP_STRIDE * A_DH
    for p in range(CMP_STRIDE):
        tok = x_ref[pl.ds(p, nch, stride=CMP_STRIDE), :]
        cs = slice(p * A_DH, (p + 1) * A_DH)
        xlo_ref[:, cs] = (tok + pe_ref[0, p:p + 1, :]).astype(BF16)
        xhi_ref[:, cs] = (tok + pe_ref[0, CMP_STRIDE + p:CMP_STRIDE + p + 1, :]).astype(BF16)
    a = _dot(xlo_ref[...], w1_ref[0, 0:half, :])
    b = _dot(xhi_ref[...], w1_ref[0, half:2 * half, :])
    b_next = pltpu.roll(b, nch - 1, axis=0)
    row = lax.broadcasted_iota(jnp.int32, b.shape, 0)
    b_next = jnp.where(row < nch - 1, b_next, 0.0)
    hid = _gelu_tanh(a + b_next + b1_ref[0])
    out = _dot(hid.astype(BF16), w2_ref[0]) + b2_ref[0]
    o_ref[0, 0] = out.astype(o_ref.dtype)


def _compress(kv, pe, w1, b1, w2, b2):
    T = kv.shape[0]
    G = A_GROUPS
    nch = T // CMP_STRIDE
    width = CMP_STRIDE * A_DH
    return pl.pallas_call(
        _compress_kernel,
        grid=(2, G),
        in_specs=[
            pl.BlockSpec((T, A_DH), lambda s, g: (0, s * G + g)),
            pl.BlockSpec((1, CMP_BLOCK, A_DH), lambda s, g: (s, 0, 0)),
            pl.BlockSpec((1, 2 * width, CMP_HIDDEN), lambda s, g: (s, 0, 0)),
            pl.BlockSpec((1, 1, CMP_HIDDEN), lambda s, g: (s, 0, 0)),
            pl.BlockSpec((1, CMP_HIDDEN, A_DH), lambda s, g: (s, 0, 0)),
            pl.BlockSpec((1, 1, A_DH), lambda s, g: (s, 0, 0)),
        ],
        out_specs=pl.BlockSpec((1, 1, nch, A_DH), lambda s, g: (s, g, 0, 0)),
        out_shape=jax.ShapeDtypeStruct((2, G, nch, A_DH), BF16),
        scratch_shapes=[pltpu.VMEM((nch, width), BF16), pltpu.VMEM((nch, width), BF16)],
        compiler_params=_cparams(("parallel", "parallel")),
        name="compress",
    )(kv, pe, w1, b1, w2, b2)


def _stack_heads(q_ref, qs_ref):
    tq = q_ref.shape[0]
    for h in range(A_HPG):
        qs_ref[h * tq:(h + 1) * tq, :] = q_ref[:, h * A_DH:(h + 1) * A_DH]


def _masked_chunks(s, masks, tq):
    out = []
    for c, mk in enumerate(masks):
        sc = s[:, c * LANES:(c + 1) * LANES]
        out.append(jnp.concatenate(
            [jnp.where(mk, sc[h * tq:(h + 1) * tq], -jnp.inf) for h in range(A_HPG)], axis=0))
    return out


def _chunk_softmax(sc):
    mx = sc[0]
    for x in sc[1:]:
        mx = jnp.maximum(mx, x)
    m = jnp.max(mx, axis=-1, keepdims=True)
    m = jnp.where(m > -jnp.inf, m, 0.0)
    pc = [jnp.exp2(x - m) for x in sc]
    ps = pc[0]
    for x in pc[1:]:
        ps = ps + x
    return pc, jnp.sum(ps, axis=-1, keepdims=True)


def _cmp_attn_kernel(q_ref, kc_ref, vc_ref, o_ref, qsel_ref, qs_ref):
    tq = q_ref.shape[0]
    nch = kc_ref.shape[2]
    t0 = pl.program_id(1) * tq
    _stack_heads(q_ref, qs_ref)
    s = _dot_nt(qs_ref[...], kc_ref[0, 0])
    tpos = t0 + lax.broadcasted_iota(jnp.int32, (tq, LANES), 0)
    lane = lax.broadcasted_iota(jnp.int32, (tq, LANES), 1)
    masks = [(lane + c * LANES) * CMP_STRIDE + (CMP_BLOCK - 1) <= tpos for c in range(nch // LANES)]
    pc, denom = _chunk_softmax(_masked_chunks(s, masks, tq))
    inv = 1.0 / jnp.where(denom > 0, denom, 1.0)
    o = _dot(jnp.concatenate([x.astype(BF16) for x in pc], axis=1), vc_ref[0, 0]) * inv
    for h in range(A_HPG):
        o_ref[:, h * A_DH:(h + 1) * A_DH] = o[h * tq:(h + 1) * tq].astype(o_ref.dtype)
    pn = [x * inv for x in pc]
    psum = jnp.concatenate(
        [sum(x[h * tq:(h + 1) * tq] for h in range(A_HPG)) for x in pn], axis=1)

    jj = lax.broadcasted_iota(jnp.int32, (NB_PAD, nch), 0)
    cs = lax.broadcasted_iota(jnp.int32, (NB_PAD, nch), 1) * CMP_STRIDE
    ov = ((cs <= jj * SEL_BLOCK + SEL_BLOCK - 1) & (cs + CMP_BLOCK - 1 >= jj * SEL_BLOCK))
    ov = jnp.where(ov, 1.0, 0.0).astype(BF16)
    p_hi, p_lo = _split_bf16(psum)
    pslc = _dot_nt(ov, p_hi) + _dot_nt(ov, p_lo)

    jf = lax.broadcasted_iota(jnp.int32, (NB_PAD, tq), 0)
    tl = t0 + lax.broadcasted_iota(jnp.int32, (NB_PAD, tq), 1)
    cur = jnp.right_shift(tl, SEL_SHIFT)
    valid = jf <= cur
    forced = (jf == 0) | (jf == cur) | (jf == cur - 1)
    score = jnp.where(forced, jnp.inf, jnp.where(valid, pslc, -jnp.inf))
    jff = jf.astype(F32)
    sel = jnp.zeros((NB_PAD, tq), F32)
    for _ in range(SEL_TOP_N):
        mx = jnp.max(score, axis=0, keepdims=True)
        first = jnp.min(jnp.where(score == mx, jff, float(NB_PAD)), axis=0, keepdims=True)
        pick = jff == first
        sel = jnp.where(pick, 1.0, sel)
        score = jnp.where(pick, -jnp.inf, score)
    neg = jnp.where((sel > 0.5) & valid, 0.0, -MASK_BIG)
    qsel_ref[...] = neg.T.astype(qsel_ref.dtype)


def _cmp_attn(qa, kvc, tq=256):
    T = qa.shape[0]
    G = A_GROUPS
    nch = kvc.shape[2]
    tq = min(tq, T)
    qw = A_HPG * A_DH
    return pl.pallas_call(
        _cmp_attn_kernel,
        grid=(G, T // tq),
        in_specs=[
            pl.BlockSpec((tq, qw), lambda g, i: (i, g)),
            pl.BlockSpec((1, 1, nch, A_DH), lambda g, i: (0, g, 0, 0)),
            pl.BlockSpec((1, 1, nch, A_DH), lambda g, i: (1, g, 0, 0)),
        ],
        out_specs=[
            pl.BlockSpec((tq, qw), lambda g, i: (i, g)),
            pl.BlockSpec((tq, NB_PAD), lambda g, i: (i, g)),
        ],
        out_shape=[
            jax.ShapeDtypeStruct((T, A_HEADS * A_DH), BF16),
            jax.ShapeDtypeStruct((T, G * NB_PAD), BF16),
        ],
        scratch_shapes=[pltpu.VMEM((A_HPG * tq, A_DH), BF16)],
        compiler_params=_cparams(("parallel", "parallel")),
        name="cmp_attn",
    )(qa, kvc, kvc)


def _sel_attn_kernel(q_ref, qsel_ref, k_ref, v_ref, oh_ref, o_ref, qa_ref, m_ref, l_ref, acc_ref):
    tq = q_ref.shape[0]
    tk = k_ref.shape[0]
    i = pl.program_id(1)
    kt = pl.program_id(2)
    last_kt = (i * tq + tq - 1) // tk
    nlc = tk // LANES

    @pl.when(kt == 0)
    def _():
        for h in range(A_HPG):
            qa_ref[h * tq:(h + 1) * tq, 0:A_DH] = q_ref[:, h * A_DH:(h + 1) * A_DH]
            qa_ref[h * tq:(h + 1) * tq, A_DH:2 * A_DH] = qsel_ref[...]
        m_ref[...] = jnp.full(m_ref.shape, -jnp.inf, F32)
        l_ref[...] = jnp.zeros(l_ref.shape, F32)
        acc_ref[...] = jnp.zeros(acc_ref.shape, F32)

    def step(diagonal):
        ka = jnp.concatenate([k_ref[...], oh_ref[...]], axis=1)
        s = _dot_nt(qa_ref[...], ka)
        sc = [s[:, c * LANES:(c + 1) * LANES] for c in range(nlc)]
        if diagonal:
            tpos = i * tq + lax.broadcasted_iota(jnp.int32, (tq, LANES), 0)
            lane = kt * tk + lax.broadcasted_iota(jnp.int32, (tq, LANES), 1)
            for c in range(nlc):
                causal = lane + c * LANES <= tpos
                sc[c] = jnp.concatenate(
                    [jnp.where(causal, sc[c][h * tq:(h + 1) * tq], -MASK_BIG) for h in range(A_HPG)], axis=0)
        mx = sc[0]
        for c in range(1, nlc):
            mx = jnp.maximum(mx, sc[c])
        m_old = m_ref[...]
        m_new = jnp.maximum(m_old, jnp.max(mx, axis=-1, keepdims=True))
        alpha = jnp.exp2(m_old - m_new)
        pc = [jnp.exp2(sc[c] - m_new) for c in range(nlc)]
        ps = pc[0]
        for c in range(1, nlc):
            ps = ps + pc[c]
        l_ref[...] = alpha * l_ref[...] + ps
        p = jnp.concatenate([x.astype(BF16) for x in pc], axis=1)
        acc_ref[...] = alpha * acc_ref[...] + _dot(p, v_ref[...])
        m_ref[...] = m_new

    @pl.when(kt < last_kt)
    def _():
        step(False)

    @pl.when(kt == last_kt)
    def _():
        step(True)

    @pl.when(kt == pl.num_programs(2) - 1)
    def _():
        for h in range(A_HPG):
            l = jnp.sum(l_ref[h * tq:(h + 1) * tq], axis=-1, keepdims=True)
            o = acc_ref[h * tq:(h + 1) * tq] / l
            o_ref[:, h * A_DH:(h + 1) * A_DH] = o.astype(o_ref.dtype)


def _sel_attn(qa, kvsw, qsel, tq=512, tk=512):
    T = qa.shape[0]
    G = A_GROUPS
    tq = min(tq, T)
    tk = min(tk, T)
    qw = A_HPG * A_DH
    rows = A_HPG * tq

    def kv_map(col, per_group):
        def f(g, i, kt):
            return (jnp.minimum(kt, (i * tq + tq - 1) // tk), col // A_DH + g * per_group)
        return f

    onehot = jnp.asarray(np.arange(T)[:, None] // SEL_BLOCK == np.arange(NB_PAD)[None, :], dtype=BF16)

    return pl.pallas_call(
        _sel_attn_kernel,
        grid=(G, T // tq, T // tk),
        in_specs=[
            pl.BlockSpec((tq, qw), lambda g, i, kt: (i, g)),
            pl.BlockSpec((tq, NB_PAD), lambda g, i, kt: (i, g)),
            pl.BlockSpec((tk, A_DH), kv_map(KVSW_K_SEL, 1)),
            pl.BlockSpec((tk, A_DH), kv_map(KVSW_V_SEL, 1)),
            pl.BlockSpec((tk, NB_PAD), kv_map(0, 0)),
        ],
        out_specs=pl.BlockSpec((tq, qw), lambda g, i, kt: (i, g)),
        out_shape=jax.ShapeDtypeStruct((T, A_HEADS * A_DH), BF16),
        scratch_shapes=[
            pltpu.VMEM((rows, 2 * A_DH), BF16),
            pltpu.VMEM((rows, LANES), F32),
            pltpu.VMEM((rows, LANES), F32),
            pltpu.VMEM((rows, A_DH), F32),
        ],
        compiler_params=_cparams(("parallel", "parallel", "arbitrary")),
        name="sel_attn",
    )(qa, qsel, kvsw, kvsw, onehot)


def _win_attn_kernel(q_ref, k0_ref, k1_ref, k2_ref, v0_ref, v1_ref, v2_ref, o_ref, qs_ref):
    tq = q_ref.shape[0]
    i = pl.program_id(1)
    k = jnp.concatenate([k0_ref[...], k1_ref[...], k2_ref[...]], axis=0)
    v = jnp.concatenate([v0_ref[...], v1_ref[...], v2_ref[...]], axis=0)
    _stack_heads(q_ref, qs_ref)
    s = _dot_nt(qs_ref[...], k)
    r = lax.broadcasted_iota(jnp.int32, (tq, LANES), 0)
    lane = lax.broadcasted_iota(jnp.int32, (tq, LANES), 1)
    masks = []
    for c in range(3 * tq // LANES):
        col = lane + c * LANES
        rel = 2 * tq + r - col
        masks.append((rel >= 0) & (rel < WIN_SIZE) & ((i - 2) * tq + col >= 0))
    pc, denom = _chunk_softmax(_masked_chunks(s, masks, tq))
    o = _dot(jnp.concatenate([x.astype(BF16) for x in pc], axis=1), v) / denom
    for h in range(A_HPG):
        o_ref[:, h * A_DH:(h + 1) * A_DH] = o[h * tq:(h + 1) * tq].astype(o_ref.dtype)


def _win_attn(qa, kvsw, tq=256):
    T = qa.shape[0]
    G = A_GROUPS
    tq = min(tq, T)
    assert WIN_SIZE <= 2 * tq
    qw = A_HPG * A_DH

    def kv_spec(col, back):
        return pl.BlockSpec((tq, A_DH), lambda g, i: (jnp.maximum(i - back, 0), col // A_DH + g))

    return pl.pallas_call(
        _win_attn_kernel,
        grid=(G, T // tq),
        in_specs=[pl.BlockSpec((tq, qw), lambda g, i: (i, g)),
                  kv_spec(KVSW_K_WIN, 2), kv_spec(KVSW_K_WIN, 1), kv_spec(KVSW_K_WIN, 0),
                  kv_spec(KVSW_V_WIN, 2), kv_spec(KVSW_V_WIN, 1), kv_spec(KVSW_V_WIN, 0)],
        out_specs=pl.BlockSpec((tq, qw), lambda g, i: (i, g)),
        out_shape=jax.ShapeDtypeStruct((T, A_HEADS * A_DH), BF16),
        scratch_shapes=[pltpu.VMEM((A_HPG * tq, A_DH), BF16)],
        compiler_params=_cparams(("parallel", "parallel")),
        name="win_attn",
    )(qa, kvsw, kvsw, kvsw, kvsw, kvsw, kvsw)


def _gla_kernel(q_ref, k_ref, v_ref, a_ref, aw_ref, ab_ref, nw_ref, o_ref, st_ref):
    tc = q_ref.shape[0]
    C = B_CHUNK

    @pl.when(pl.program_id(0) == 0)
    def _():
        st_ref[...] = jnp.zeros(st_ref.shape, F32)

    logits = _dot(a_ref[...], aw_ref[...]) + ab_ref[...]
    log_a = jax.nn.log_sigmoid(logits) * (1.0 / B_GATE_TEMP)
    ri = lax.broadcasted_iota(jnp.int32, (C, C), 0)
    ci = lax.broadcasted_iota(jnp.int32, (C, C), 1)
    tril = ci <= ri
    tril_b = jnp.where(tril, 1.0, 0.0).astype(BF16)
    nw = nw_ref[...]
    for c in range(tc // C):
        sl = slice(c * C, (c + 1) * C)
        la_hi, la_lo = _split_bf16(log_a[sl])
        cum = _dot(tril_b, la_hi) + _dot(tril_b, la_lo)
        last = cum[C - 1:C, :]
        qc = q_ref[sl, :].astype(F32) * (B_DK ** -0.5)
        kc = k_ref[sl, :].astype(F32)
        q_dec = (qc * jnp.exp(cum)).astype(BF16)
        k_inv = (kc * jnp.exp(-cum)).astype(BF16)
        k_state = (kc * jnp.exp(last - cum)).astype(BF16)
        dec = jnp.exp(last)
        for h in range(B_HEADS):
            ks = slice(h * B_DK, (h + 1) * B_DK)
            vs = slice(h * B_DV, (h + 1) * B_DV)
            vc = v_ref[sl, vs]
            att = jnp.where(tril, _dot_nt(q_dec[:, ks], k_inv[:, ks]), 0.0)
            st = st_ref[h]
            o = _dot(att.astype(BF16), vc) + _dot_nt(q_dec[:, ks], st.astype(BF16))
            st_ref[h] = st * dec[:, ks] + _dot_tn(vc, k_state[:, ks])
            o = o * lax.rsqrt(jnp.mean(o * o, axis=-1, keepdims=True) + NORM_EPS) * nw
            o_ref[sl, vs] = o.astype(o_ref.dtype)


def _gla(cols2, alr, aw, ab, nw, tc=512):
    T = cols2.shape[0]
    tc = min(tc, T)
    kw = B_HEADS * B_DK
    vw = B_HEADS * B_DV
    return pl.pallas_call(
        _gla_kernel,
        grid=(T // tc,),
        in_specs=[
            pl.BlockSpec((tc, kw), lambda i: (i, R2_Q_B // kw)),
            pl.BlockSpec((tc, kw), lambda i: (i, R2_K_B // kw)),
            pl.BlockSpec((tc, vw), lambda i: (i, R2_V_B // vw)),
            pl.BlockSpec((tc, LANES), lambda i: (i, 0)),
            pl.BlockSpec((LANES, kw), lambda i: (0, 0)),
            pl.BlockSpec((1, kw), lambda i: (0, 0)),
            pl.BlockSpec((1, B_DV), lambda i: (0, 0)),
        ],
        out_specs=pl.BlockSpec((tc, vw), lambda i: (i, 0)),
        out_shape=jax.ShapeDtypeStruct((T, vw), BF16),
        scratch_shapes=[pltpu.VMEM((B_HEADS, B_DV, B_DK), F32)],
        compiler_params=_cparams(("arbitrary",)),
        name="gla",
    )(cols2, cols2, cols2, alr, aw, ab, nw)


def _silu(x):
    return x * jax.nn.sigmoid(x)


def _merge_a_kernel(ocmp_ref, osel_ref, owin_ref, gbr_ref, za_ref, ga_ref, pa_ref, y_ref, ua_ref):
    sg = jax.nn.sigmoid(gbr_ref[...].astype(F32))
    for hh in range(A_HEADS):
        cs = slice(hh * A_DH, (hh + 1) * A_DH)
        oa = (sg[:, 3 * hh:3 * hh + 1] * ocmp_ref[:, cs].astype(F32)
              + sg[:, 3 * hh + 1:3 * hh + 2] * osel_ref[:, cs].astype(F32)
              + sg[:, 3 * hh + 2:3 * hh + 3] * owin_ref[:, cs].astype(F32))
        ua_ref[:, cs] = (oa * _silu(za_ref[:, cs].astype(F32))).astype(BF16)
    ya = _dot(ua_ref[...], pa_ref[...])
    y_ref[...] = (jax.nn.sigmoid(ga_ref[...].astype(F32)) * ya).astype(y_ref.dtype)


def _merge_b_kernel(ya_ref, ob_ref, zb_ref, gb_ref, pb_ref, y_ref):
    ub = (ob_ref[...].astype(F32) * _silu(zb_ref[...].astype(F32))).astype(BF16)
    yb = _dot(ub, pb_ref[...])
    y = ya_ref[...].astype(F32) + jax.nn.sigmoid(gb_ref[...].astype(F32)) * yb
    y_ref[...] = y.astype(y_ref.dtype)


def _merge(o_cmp, o_sel, o_win, gbr, cols2, o_b, cols3, pa, pb, tm=512):
    T = gbr.shape[0]
    D = D_MODEL
    tm = min(tm, T)
    row = lambda c: pl.BlockSpec((tm, D), lambda i: (i, c))
    const = lambda: pl.BlockSpec((D, D), lambda i: (0, 0), pipeline_mode=pl.Buffered(1))
    ya = pl.pallas_call(
        _merge_a_kernel,
        grid=(T // tm,),
        in_specs=[row(0), row(0), row(0),
                  pl.BlockSpec((tm, LANES), lambda i: (i, 0)),
                  row(R2_Z_A // D), row(R3_GATE_A // D), const()],
        out_specs=row(0),
        out_shape=jax.ShapeDtypeStruct((T, D), BF16),
        scratch_shapes=[pltpu.VMEM((tm, D), BF16)],
        compiler_params=_cparams(("parallel",)),
        name="merge_a",
    )(o_cmp, o_sel, o_win, gbr, cols2, cols3, pa)
    return pl.pallas_call(
        _merge_b_kernel,
        grid=(T // tm,),
        in_specs=[row(0), row(0), row(R3_Z_B // D), row(R3_GATE_B // D), const()],
        out_specs=row(0),
        out_shape=jax.ShapeDtypeStruct((T, D), BF16),
        compiler_params=_cparams(("parallel",)),
        name="merge_b",
    )(ya, o_b, cols3, cols3, pb)


def _outproj_kernel(x_ref, y_ref, w_ref, nw_ref, *out_refs, last):
    xn = x_ref[...] + _dot(y_ref[...], w_ref[...])
    ms = jnp.mean(xn * xn, axis=-1, keepdims=True)
    hn = xn * lax.rsqrt(ms + NORM_EPS) * nw_ref[...]
    if last:
        out_refs[0][...] = hn
    else:
        out_refs[0][...] = xn
        out_refs[1][...] = hn.astype(BF16)


def _outproj(x, y, w, nw, last, tm=512):
    T, D = x.shape
    tm = min(tm, T)
    row = pl.BlockSpec((tm, D), lambda i: (i, 0))
    if last:
        out_specs, out_shape = row, jax.ShapeDtypeStruct((T, D), F32)
    else:
        out_specs = [row, row]
        out_shape = [jax.ShapeDtypeStruct((T, D), F32), jax.ShapeDtypeStruct((T, D), BF16)]
    return pl.pallas_call(
        functools.partial(_outproj_kernel, last=last),
        grid=(T // tm,),
        in_specs=[
            row, row,
            pl.BlockSpec((D, D), lambda i: (0, 0), pipeline_mode=pl.Buffered(1)),
            pl.BlockSpec((1, D), lambda i: (0, 0)),
        ],
        out_specs=out_specs,
        out_shape=out_shape,
        compiler_params=_cparams(("parallel",)),
        name="outproj",
    )(x, y, w, nw)


def kernel(x, norm_w, w_in, cmp_pe, cmp_w1, cmp_b1, cmp_w2, cmp_b2, gla_alpha_w, gla_alpha_b, gla_norm_w,
           p_a, p_b, w_out, final_norm_w):
    B, T, D = x.shape
    L = norm_w.shape[0]
    assert B == 1 and D == D_MODEL and T % 512 == 0 and T // SEL_BLOCK <= NB_PAD
    assert w_in.shape[2] == SRC_END
    w_t = jnp.swapaxes(w_in, 1, 2)

    w1_b = cmp_w1.astype(BF16)
    w2_b = cmp_w2.astype(BF16)
    b1_r = cmp_b1.reshape(L, 2, 1, CMP_HIDDEN)
    b2_r = cmp_b2.reshape(L, 2, 1, A_DH)
    aw_p = jnp.pad(gla_alpha_w, ((0, 0), (0, LANES - B_RANK), (0, 0))).astype(BF16)
    ab_r = gla_alpha_b.reshape(L, 1, B_HEADS * B_DK)
    gnw_r = gla_norm_w.reshape(L, 1, B_DV)
    pa_b = p_a.astype(BF16)
    pb_b = p_b.astype(BF16)
    wo_b = w_out.astype(BF16)
    nw_next = jnp.concatenate([norm_w[1:], final_norm_w[None, :]], axis=0).reshape(L, 1, D)

    xs = x.reshape(T, D)
    h = _rmsnorm(xs, norm_w[0].reshape(1, D))
    for l in range(L):
        qa = _proj(h, w_t, l, SRC_Q_A, A_HEADS * A_DH, 512, BF16, scale=A_DH ** -0.5 * LOG2E)
        kv_cmp = _proj(h, w_t, l, SRC_KV_CMP, 2 * A_KVW, 512, F32)
        kvsw = _proj(h, w_t, l, SRC_KV_SW, 4 * A_KVW, 512, BF16)
        gbr = _proj(h, w_t, l, SRC_G_BR, LANES, LANES, BF16, valid=G_BR_COLS)
        cols2 = _proj(h, w_t, l, SRC_R2, SRC_A_LR - SRC_R2, 512, BF16)
        alr = _proj(h, w_t, l, SRC_A_LR, LANES, LANES, BF16, valid=B_RANK)
        cols3 = _proj(h, w_t, l, SRC_R3, SRC_END - SRC_R3, 512, BF16)
        kvc = _compress(kv_cmp, cmp_pe[l], w1_b[l], b1_r[l], w2_b[l], b2_r[l])
        o_cmp, qsel = _cmp_attn(qa, kvc)
        o_sel = _sel_attn(qa, kvsw, qsel)
        o_win = _win_attn(qa, kvsw)
        o_b = _gla(cols2, alr, aw_p[l], ab_r[l], gnw_r[l])
        y = _merge(o_cmp, o_sel, o_win, gbr, cols2, o_b, cols3, pa_b[l], pb_b[l])
        if l == L - 1:
            xs = _outproj(xs, y, wo_b[l], nw_next[l], last=True)
        else:
            xs, h = _outproj(xs, y, wo_b[l], nw_next[l], last=False)
    return xs.reshape(B, T, D)
```

```python
import functools

import jax
import jax.numpy as jnp
import numpy as np
from jax import lax
from jax.experimental import pallas as pl
from jax.experimental.pallas import tpu as pltpu

F32 = jnp.float32
BF16 = jnp.bfloat16

D_MODEL = 2048
A_HEADS = 16
A_GROUPS = 4
A_HPG = A_HEADS // A_GROUPS
A_DH = 128
A_KVW = A_GROUPS * A_DH
CMP_BLOCK = 32
CMP_STRIDE = 16
CMP_HIDDEN = 256
SEL_BLOCK = 64
SEL_SHIFT = 6
SEL_TOP_N = 16
WIN_SIZE = 512
B_HEADS = 4
B_DK = 256
B_DV = 512
B_RANK = 16
B_GATE_TEMP = 16.0
B_CHUNK = 64
NORM_EPS = 1e-6

LANES = 128
NB_PAD = LANES
MASK_BIG = 2.0 ** 100
LOG2E = float(np.log2(np.e))

SRC_Q_A = 0
SRC_KV_CMP = 2048
SRC_KV_SW = 3072
SRC_G_BR = 5120
SRC_R2 = 5168
SRC_A_LR = 11312
SRC_R3 = 11328
SRC_END = 17472
G_BR_COLS = 48
KVSW_K_SEL, KVSW_V_SEL, KVSW_K_WIN, KVSW_V_WIN = 0, 512, 1024, 1536
R2_Z_A, R2_Q_B, R2_K_B, R2_V_B = 0, 2048, 3072, 4096
R3_Z_B, R3_GATE_A, R3_GATE_B = 0, 2048, 4096

VMEM_LIMIT = 56 * 1024 * 1024


def _cparams(sem):
    return pltpu.CompilerParams(dimension_semantics=sem, vmem_limit_bytes=VMEM_LIMIT)


def _dot(a, b):
    return jnp.dot(a, b, preferred_element_type=F32)


def _dot_nt(a, b):
    return lax.dot_general(a, b, (((1,), (1,)), ((), ())), preferred_element_type=F32)


def _dot_tn(a, b):
    return lax.dot_general(a, b, (((0,), (0,)), ((), ())), preferred_element_type=F32)


def _split_bf16(x):
    hi = x.astype(BF16)
    lo = (x - hi.astype(F32)).astype(BF16)
    return hi, lo


def _rmsnorm_kernel(x_ref, w_ref, o_ref):
    x = x_ref[...]
    ms = jnp.mean(x * x, axis=-1, keepdims=True)
    o_ref[...] = (x * lax.rsqrt(ms + NORM_EPS) * w_ref[...]).astype(o_ref.dtype)


def _rmsnorm(x, w, tm=512):
    T, D = x.shape
    tm = min(tm, T)
    return pl.pallas_call(
        _rmsnorm_kernel,
        grid=(T // tm,),
        in_specs=[pl.BlockSpec((tm, D), lambda i: (i, 0)), pl.BlockSpec((1, D), lambda i: (0, 0))],
        out_specs=pl.BlockSpec((tm, D), lambda i: (i, 0)),
        out_shape=jax.ShapeDtypeStruct((T, D), BF16),
        compiler_params=_cparams(("parallel",)),
        name="rmsnorm",
    )(x, w)


def _proj_kernel(h_ref, *refs, shift, valid, scale):
    if shift:
        wa_ref, wb_ref, o_ref, wc_ref = refs
    else:
        wa_ref, o_ref, wc_ref = refs
    tn = o_ref.shape[1]

    @pl.when(pl.program_id(1) == 0)
    def _():
        if shift:
            wc_ref[0:tn - shift, :] = wa_ref[shift:tn, :].astype(BF16)
            wc_ref[tn - shift:tn, :] = wb_ref[0:shift, :].astype(BF16)
        else:
            wc_ref[...] = wa_ref[...].astype(BF16)
        if valid < tn:
            wc_ref[valid:tn, :] = jnp.zeros((tn - valid, wc_ref.shape[1]), BF16)

    r = _dot_nt(h_ref[...], wc_ref[...])
    if scale != 1.0:
        r = r * scale
    o_ref[...] = r.astype(o_ref.dtype)


def _proj(h, w_t, layer, col0, width, tn, out_dtype, scale=1.0, valid=None, tm=2048):
    T, D = h.shape
    tm = min(tm, T)
    shift = col0 % LANES
    a0 = col0 - shift
    valid = tn if valid is None else valid
    assert a0 % tn == 0 and width % tn == 0 and (valid == tn or width == tn)
    assert shift % 16 == 0 and valid % 16 == 0
    in_specs = [
        pl.BlockSpec((tm, D), lambda j, i: (i, 0)),
        pl.BlockSpec((None, tn, D), lambda j, i: (layer, a0 // tn + j, 0)),
    ]
    args = [h, w_t]
    if shift:
        in_specs.append(pl.BlockSpec((None, LANES, D), lambda j, i: (layer, (a0 + (j + 1) * tn) // LANES, 0)))
        args.append(w_t)
    return pl.pallas_call(
        functools.partial(_proj_kernel, shift=shift, valid=valid, scale=scale),
        grid=(width // tn, T // tm),
        in_specs=in_specs,
        out_specs=pl.BlockSpec((tm, tn), lambda j, i: (i, j)),
        out_shape=jax.ShapeDtypeStruct((T, width), out_dtype),
        scratch_shapes=[pltpu.VMEM((tn, D), BF16)],
        compiler_params=_cparams(("arbitrary", "arbitrary")),
        name="proj",
    )(*args)


def _gelu_tanh(x):
    c = np.float32(np.sqrt(2.0 / np.pi))
    return 0.5 * x * (1.0 + jnp.tanh(c * (x + np.float32(0.044715) * (x * x * x))))


def _compress_kernel(x_ref, pe_ref, w1_ref, b1_ref, w2_ref, b2_ref, o_ref, xlo_ref, xhi_ref):
    nch = o_ref.shape[2]
    half = CMP_STRIDE * A_DH
    for p in range(CMP_STRIDE):
        tok = x_ref[pl.ds(p, nch, stride=CMP_STRIDE), :]
        cs = slice(p * A_DH, (p + 1) * A_DH)
        xlo_ref[:, cs] = (tok + pe_ref[0, p:p + 1, :]).astype(BF16)
        xhi_ref[:, cs] = (tok + pe_ref[0, CMP_STRIDE + p:CMP_STRIDE + p + 1, :]).astype(BF16)
    a = _dot(xlo_ref[...], w1_ref[0, 0:half, :])
    b = _dot(xhi_ref[...], w1_ref[0, half:2 * half, :])
    b_next = pltpu.roll(b, nch - 1, axis=0)
    row = lax.broadcasted_iota(jnp.int32, b.shape, 0)
    b_next = jnp.where(row < nch - 1, b_next, 0.0)
    hid = _gelu_tanh(a + b_next + b1_ref[0])
    out = _dot(hid.astype(BF16), w2_ref[0]) + b2_ref[0]
    o_ref[0, 0] = out.astype(o_ref.dtype)


def _compress(kv, pe, w1, b1, w2, b2):
    T = kv.shape[0]
    G = A_GROUPS
    nch = T // CMP_STRIDE
    width = CMP_STRIDE * A_DH
    return pl.pallas_call(
        _compress_kernel,
        grid=(2, G),
        in_specs=[
            pl.BlockSpec((T, A_DH), lambda s, g: (0, s * G + g)),
            pl.BlockSpec((1, CMP_BLOCK, A_DH), lambda s, g: (s, 0, 0)),
            pl.BlockSpec((1, 2 * width, CMP_HIDDEN), lambda s, g: (s, 0, 0)),
            pl.BlockSpec((1, 1, CMP_HIDDEN), lambda s, g: (s, 0, 0)),
            pl.BlockSpec((1, CMP_HIDDEN, A_DH), lambda s, g: (s, 0, 0)),
            pl.BlockSpec((1, 1, A_DH), lambda s, g: (s, 0, 0)),
        ],
        out_specs=pl.BlockSpec((1, 1, nch, A_DH), lambda s, g: (s, g, 0, 0)),
        out_shape=jax.ShapeDtypeStruct((2, G, nch, A_DH), BF16),
        scratch_shapes=[pltpu.VMEM((nch, width), BF16), pltpu.VMEM((nch, width), BF16)],
        compiler_params=_cparams(("parallel", "parallel")),
        name="compress",
    )(kv, pe, w1, b1, w2, b2)


def _stack_heads(q_ref, qs_ref):
    tq = q_ref.shape[0]
    for h in range(A_HPG):
        qs_ref[h * tq:(h + 1) * tq, :] = q_ref[:, h * A_DH:(h + 1) * A_DH]


def _masked_chunks(s, masks, tq):
    out = []
    for c, mk in enumerate(masks):
        sc = s[:, c * LANES:(c + 1) * LANES]
        out.append(jnp.concatenate(
            [jnp.where(mk, sc[h * tq:(h + 1) * tq], -jnp.inf) for h in range(A_HPG)], axis=0))
    return out


def _chunk_softmax(sc):
    mx = sc[0]
    for x in sc[1:]:
        mx = jnp.maximum(mx, x)
    m = jnp.max(mx, axis=-1, keepdims=True)
    m = jnp.where(m > -jnp.inf, m, 0.0)
    pc = [jnp.exp2(x - m) for x in sc]
    ps = pc[0]
    for x in pc[1:]:
        ps = ps + x
    return pc, jnp.sum(ps, axis=-1, keepdims=True)


def _cmp_attn_kernel(q_ref, kc_ref, vc_ref, o_ref, qsel_ref, qs_ref):
    tq = q_ref.shape[0]
    nch = kc_ref.shape[2]
    t0 = pl.program_id(1) * tq
    _stack_heads(q_ref, qs_ref)
    s = _dot_nt(qs_ref[...], kc_ref[0, 0])
    tpos = t0 + lax.broadcasted_iota(jnp.int32, (tq, LANES), 0)
    lane = lax.broadcasted_iota(jnp.int32, (tq, LANES), 1)
    masks = [(lane + c * LANES) * CMP_STRIDE + (CMP_BLOCK - 1) <= tpos for c in range(nch // LANES)]
    pc, denom = _chunk_softmax(_masked_chunks(s, masks, tq))
    inv = 1.0 / jnp.where(denom > 0, denom, 1.0)
    o = _dot(jnp.concatenate([x.astype(BF16) for x in pc], axis=1), vc_ref[0, 0]) * inv
    for h in range(A_HPG):
        o_ref[:, h * A_DH:(h + 1) * A_DH] = o[h * tq:(h + 1) * tq].astype(o_ref.dtype)
    pn = [x * inv for x in pc]
    psum = jnp.concatenate(
        [sum(x[h * tq:(h + 1) * tq] for h in range(A_HPG)) for x in pn], axis=1)

    jj = lax.broadcasted_iota(jnp.int32, (NB_PAD, nch), 0)
    cs = lax.broadcasted_iota(jnp.int32, (NB_PAD, nch), 1) * CMP_STRIDE
    ov = ((cs <= jj * SEL_BLOCK + SEL_BLOCK - 1) & (cs + CMP_BLOCK - 1 >= jj * SEL_BLOCK))
    ov = jnp.where(ov, 1.0, 0.0).astype(BF16)
    p_hi, p_lo = _split_bf16(psum)
    pslc = _dot_nt(ov, p_hi) + _dot_nt(ov, p_lo)

    jf = lax.broadcasted_iota(jnp.int32, (NB_PAD, tq), 0)
    tl = t0 + lax.broadcasted_iota(jnp.int32, (NB_PAD, tq), 1)
    cur = jnp.right_shift(tl, SEL_SHIFT)
    valid = jf <= cur
    forced = (jf == 0) | (jf == cur) | (jf == cur - 1)
    score = jnp.where(forced, jnp.inf, jnp.where(valid, pslc, -jnp.inf))
    jff = jf.astype(F32)
    sel = jnp.zeros((NB_PAD, tq), F32)
    for _ in range(SEL_TOP_N):
        mx = jnp.max(score, axis=0, keepdims=True)
        first = jnp.min(jnp.where(score == mx, jff, float(NB_PAD)), axis=0, keepdims=True)
        pick = jff == first
        sel = jnp.where(pick, 1.0, sel)
        score = jnp.where(pick, -jnp.inf, score)
    neg = jnp.where((sel > 0.5) & valid, 0.0, -MASK_BIG)
    qsel_ref[...] = neg.T.astype(qsel_ref.dtype)


def _cmp_attn(qa, kvc, tq=256):
    T = qa.shape[0]
    G = A_GROUPS
    nch = kvc.shape[2]
    tq = min(tq, T)
    qw = A_HPG * A_DH
    return pl.pallas_call(
        _cmp_attn_kernel,
        grid=(G, T // tq),
        in_specs=[
            pl.BlockSpec((tq, qw), lambda g, i: (i, g)),
            pl.BlockSpec((1, 1, nch, A_DH), lambda g, i: (0, g, 0, 0)),
            pl.BlockSpec((1, 1, nch, A_DH), lambda g, i: (1, g, 0, 0)),
        ],
        out_specs=[
            pl.BlockSpec((tq, qw), lambda g, i: (i, g)),
            pl.BlockSpec((tq, NB_PAD), lambda g, i: (i, g)),
        ],
        out_shape=[
            jax.ShapeDtypeStruct((T, A_HEADS * A_DH), BF16),
            jax.ShapeDtypeStruct((T, G * NB_PAD), BF16),
        ],
        scratch_shapes=[pltpu.VMEM((A_HPG * tq, A_DH), BF16)],
        compiler_params=_cparams(("parallel", "parallel")),
        name="cmp_attn",
    )(qa, kvc, kvc)


def _sel_attn_kernel(q_ref, qsel_ref, k_ref, vp_ref, vc_ref, oh_ref, o_ref, qa_ref, p_ref, m_ref, l_ref, acc_ref):
    tq = q_ref.shape[0]
    tk = k_ref.shape[0]
    i = pl.program_id(1)
    kt = pl.program_id(2)
    last_kt = (i * tq + tq - 1) // tk
    nlc = tk // LANES

    @pl.when(kt == 0)
    def _():
        for h in range(A_HPG):
            qa_ref[h * tq:(h + 1) * tq, 0:A_DH] = q_ref[:, h * A_DH:(h + 1) * A_DH]
            qa_ref[h * tq:(h + 1) * tq, A_DH:2 * A_DH] = qsel_ref[...]
        m_ref[...] = jnp.full(m_ref.shape, -jnp.inf, F32)
        l_ref[...] = jnp.zeros(l_ref.shape, F32)
        acc_ref[...] = jnp.zeros(acc_ref.shape, F32)
        p_ref[1] = jnp.zeros(p_ref.shape[1:], BF16)

    def step(diagonal, slot):
        ka = jnp.concatenate([k_ref[...], oh_ref[...]], axis=1)
        s = _dot_nt(qa_ref[...], ka)
        sc = [s[:, c * LANES:(c + 1) * LANES] for c in range(nlc)]
        if diagonal:
            tpos = i * tq + lax.broadcasted_iota(jnp.int32, (tq, LANES), 0)
            lane = kt * tk + lax.broadcasted_iota(jnp.int32, (tq, LANES), 1)
            for c in range(nlc):
                causal = lane + c * LANES <= tpos
                sc[c] = jnp.concatenate(
                    [jnp.where(causal, sc[c][h * tq:(h + 1) * tq], -MASK_BIG) for h in range(A_HPG)], axis=0)
        mx = sc[0]
        for c in range(1, nlc):
            mx = jnp.maximum(mx, sc[c])
        m_old = m_ref[...]
        m_new = jnp.maximum(m_old, jnp.max(mx, axis=-1, keepdims=True))
        alpha = jnp.exp2(m_old - m_new)
        pc = [jnp.exp2(sc[c] - m_new) for c in range(nlc)]
        ps = pc[0]
        for c in range(1, nlc):
            ps = ps + pc[c]
        l_ref[...] = alpha * l_ref[...] + ps
        p = jnp.concatenate([x.astype(BF16) for x in pc], axis=1)
        acc = alpha * (acc_ref[...] + _dot(p_ref[1 - slot], vp_ref[...]))
        if diagonal:
            acc = acc + _dot(p, vc_ref[...])
        else:
            p_ref[slot] = p
        acc_ref[...] = acc
        m_ref[...] = m_new

    for diagonal in (False, True):
        for slot in (0, 1):
            on_tile = (kt == last_kt) if diagonal else (kt < last_kt)
            pl.when(on_tile & (kt % 2 == slot))(functools.partial(step, diagonal, slot))

    @pl.when(kt == pl.num_programs(2) - 1)
    def _():
        for h in range(A_HPG):
            l = jnp.sum(l_ref[h * tq:(h + 1) * tq], axis=-1, keepdims=True)
            o = acc_ref[h * tq:(h + 1) * tq] / l
            o_ref[:, h * A_DH:(h + 1) * A_DH] = o.astype(o_ref.dtype)


def _sel_attn(qa, kvsw, qsel, tq=512, tk=512):
    T = qa.shape[0]
    G = A_GROUPS
    tq = min(tq, T)
    tk = min(tk, T)
    qw = A_HPG * A_DH
    rows = A_HPG * tq

    def kv_map(col, per_group, back=0):
        def f(g, i, kt):
            tile = jnp.minimum(jnp.maximum(kt - back, 0), (i * tq + tq - 1) // tk)
            return (tile, col // A_DH + g * per_group)
        return f

    onehot = jnp.asarray(np.arange(T)[:, None] // SEL_BLOCK == np.arange(NB_PAD)[None, :], dtype=BF16)

    return pl.pallas_call(
        _sel_attn_kernel,
        grid=(G, T // tq, T // tk),
        in_specs=[
            pl.BlockSpec((tq, qw), lambda g, i, kt: (i, g)),
            pl.BlockSpec((tq, NB_PAD), lambda g, i, kt: (i, g)),
            pl.BlockSpec((tk, A_DH), kv_map(KVSW_K_SEL, 1)),
            pl.BlockSpec((tk, A_DH), kv_map(KVSW_V_SEL, 1, back=1)),
            pl.BlockSpec((tk, A_DH), kv_map(KVSW_V_SEL, 1)),
            pl.BlockSpec((tk, NB_PAD), kv_map(0, 0)),
        ],
        out_specs=pl.BlockSpec((tq, qw), lambda g, i, kt: (i, g)),
        out_shape=jax.ShapeDtypeStruct((T, A_HEADS * A_DH), BF16),
        scratch_shapes=[
            pltpu.VMEM((rows, 2 * A_DH), BF16),
            pltpu.VMEM((2, rows, tk), BF16),
            pltpu.VMEM((rows, LANES), F32),
            pltpu.VMEM((rows, LANES), F32),
            pltpu.VMEM((rows, A_DH), F32),
        ],
        compiler_params=_cparams(("parallel", "parallel", "arbitrary")),
        name="sel_attn",
    )(qa, qsel, kvsw, kvsw, kvsw, onehot)


def _win_attn_kernel(q_ref, k0_ref, k1_ref, k2_ref, v0_ref, v1_ref, v2_ref, o_ref, qs_ref, p_ref, inv_ref):
    tq = q_ref.shape[0]
    i = pl.program_id(1)
    ia = jnp.minimum(i, pl.num_programs(1) - 2)

    @pl.when(i == 0)
    def _():
        p_ref[1] = jnp.zeros(p_ref.shape[1:], BF16)
        inv_ref[1] = jnp.ones(inv_ref.shape[1:], F32)

    def step(slot):
        k = jnp.concatenate([k0_ref[...], k1_ref[...], k2_ref[...]], axis=0)
        _stack_heads(q_ref, qs_ref)
        s = _dot_nt(qs_ref[...], k)
        r = lax.broadcasted_iota(jnp.int32, (tq, LANES), 0)
        lane = lax.broadcasted_iota(jnp.int32, (tq, LANES), 1)
        masks = []
        for c in range(3 * tq // LANES):
            col = lane + c * LANES
            rel = 2 * tq + r - col
            masks.append((rel >= 0) & (rel < WIN_SIZE) & ((ia - 2) * tq + col >= 0))
        pc, denom = _chunk_softmax(_masked_chunks(s, masks, tq))

        v = jnp.concatenate([v0_ref[...], v1_ref[...], v2_ref[...]], axis=0)
        o = _dot(p_ref[1 - slot], v) * inv_ref[1 - slot]
        for h in range(A_HPG):
            o_ref[:, h * A_DH:(h + 1) * A_DH] = o[h * tq:(h + 1) * tq].astype(o_ref.dtype)

        p_ref[slot] = jnp.concatenate([x.astype(BF16) for x in pc], axis=1)
        inv_ref[slot] = jnp.broadcast_to(1.0 / denom, inv_ref.shape[1:])

    for slot in (0, 1):
        pl.when(i % 2 == slot)(functools.partial(step, slot))


def _win_attn(qa, kvsw, tq=256):
    T = qa.shape[0]
    G = A_GROUPS
    tq = min(tq, T)
    assert WIN_SIZE <= 2 * tq
    qw = A_HPG * A_DH
    n = T // tq

    def kv_spec(col, back, lag):
        def f(g, i):
            tile = jnp.clip(i - lag, 0, n - 1)
            return (jnp.maximum(tile - back, 0), col // A_DH + g)
        return pl.BlockSpec((tq, A_DH), f)

    return pl.pallas_call(
        _win_attn_kernel,
        grid=(G, n + 1),
        in_specs=[pl.BlockSpec((tq, qw), lambda g, i: (jnp.minimum(i, n - 1), g)),
                  kv_spec(KVSW_K_WIN, 2, 0), kv_spec(KVSW_K_WIN, 1, 0), kv_spec(KVSW_K_WIN, 0, 0),
                  kv_spec(KVSW_V_WIN, 2, 1), kv_spec(KVSW_V_WIN, 1, 1), kv_spec(KVSW_V_WIN, 0, 1)],
        out_specs=pl.BlockSpec((tq, qw), lambda g, i: (jnp.maximum(i - 1, 0), g)),
        out_shape=jax.ShapeDtypeStruct((T, A_HEADS * A_DH), BF16),
        scratch_shapes=[pltpu.VMEM((A_HPG * tq, A_DH), BF16),
                        pltpu.VMEM((2, A_HPG * tq, 3 * tq), BF16),
                        pltpu.VMEM((2, A_HPG * tq, LANES), F32)],
        compiler_params=_cparams(("parallel", "arbitrary")),
        name="win_attn",
    )(qa, kvsw, kvsw, kvsw, kvsw, kvsw, kvsw)


def _gla_kernel(q_ref, k_ref, v_ref, a_ref, aw_ref, ab_ref, nw_ref, o_ref, st_ref):
    tc = q_ref.shape[0]
    C = B_CHUNK

    @pl.when(pl.program_id(0) == 0)
    def _():
        st_ref[...] = jnp.zeros(st_ref.shape, F32)

    logits = _dot(a_ref[...], aw_ref[...]) + ab_ref[...]
    log_a = jax.nn.log_sigmoid(logits) * (1.0 / B_GATE_TEMP)
    ri = lax.broadcasted_iota(jnp.int32, (C, C), 0)
    ci = lax.broadcasted_iota(jnp.int32, (C, C), 1)
    tril = ci <= ri
    tril_b = jnp.where(tril, 1.0, 0.0).astype(BF16)
    nw = nw_ref[...]
    for c in range(tc // C):
        sl = slice(c * C, (c + 1) * C)
        la_hi, la_lo = _split_bf16(log_a[sl])
        cum = _dot(tril_b, la_hi) + _dot(tril_b, la_lo)
        last = cum[C - 1:C, :]
        qc = q_ref[sl, :].astype(F32) * (B_DK ** -0.5)
        kc = k_ref[sl, :].astype(F32)
        q_dec = (qc * jnp.exp(cum)).astype(BF16)
        k_inv = (kc * jnp.exp(-cum)).astype(BF16)
        k_state = (kc * jnp.exp(last - cum)).astype(BF16)
        dec = jnp.exp(last)
        for h in range(B_HEADS):
            ks = slice(h * B_DK, (h + 1) * B_DK)
            vs = slice(h * B_DV, (h + 1) * B_DV)
            vc = v_ref[sl, vs]
            att = jnp.where(tril, _dot_nt(q_dec[:, ks], k_inv[:, ks]), 0.0)
            st = st_ref[h]
            o = _dot(att.astype(BF16), vc) + _dot_nt(q_dec[:, ks], st.astype(BF16))
            st_ref[h] = st * dec[:, ks] + _dot_tn(vc, k_state[:, ks])
            o = o * lax.rsqrt(jnp.mean(o * o, axis=-1, keepdims=True) + NORM_EPS) * nw
            o_ref[sl, vs] = o.astype(o_ref.dtype)


def _gla(cols2, alr, aw, ab, nw, tc=512):
    T = cols2.shape[0]
    tc = min(tc, T)
    kw = B_HEADS * B_DK
    vw = B_HEADS * B_DV
    return pl.pallas_call(
        _gla_kernel,
        grid=(T // tc,),
        in_specs=[
            pl.BlockSpec((tc, kw), lambda i: (i, R2_Q_B // kw)),
            pl.BlockSpec((tc, kw), lambda i: (i, R2_K_B // kw)),
            pl.BlockSpec((tc, vw), lambda i: (i, R2_V_B // vw)),
            pl.BlockSpec((tc, LANES), lambda i: (i, 0)),
            pl.BlockSpec((LANES, kw), lambda i: (0, 0)),
            pl.BlockSpec((1, kw), lambda i: (0, 0)),
            pl.BlockSpec((1, B_DV), lambda i: (0, 0)),
        ],
        out_specs=pl.BlockSpec((tc, vw), lambda i: (i, 0)),
        out_shape=jax.ShapeDtypeStruct((T, vw), BF16),
        scratch_shapes=[pltpu.VMEM((B_HEADS, B_DV, B_DK), F32)],
        compiler_params=_cparams(("arbitrary",)),
        name="gla",
    )(cols2, cols2, cols2, alr, aw, ab, nw)


def _silu(x):
    return x * jax.nn.sigmoid(x)


def _merge_a_kernel(ocmp_ref, osel_ref, owin_ref, gbr_ref, za_ref, ga_ref, pa_ref, y_ref, ua_ref):
    i = pl.program_id(0)

    @pl.when(i == 0)
    def _():
        ua_ref[1] = jnp.zeros(ua_ref.shape[1:], BF16)

    def step(slot):
        sg = jax.nn.sigmoid(gbr_ref[...].astype(F32))
        for hh in range(A_HEADS):
            cs = slice(hh * A_DH, (hh + 1) * A_DH)
            oa = (sg[:, 3 * hh:3 * hh + 1] * ocmp_ref[:, cs].astype(F32)
                  + sg[:, 3 * hh + 1:3 * hh + 2] * osel_ref[:, cs].astype(F32)
                  + sg[:, 3 * hh + 2:3 * hh + 3] * owin_ref[:, cs].astype(F32))
            ua_ref[slot, :, cs] = (oa * _silu(za_ref[:, cs].astype(F32))).astype(BF16)
        ya = _dot(ua_ref[1 - slot], pa_ref[...])
        y_ref[...] = (jax.nn.sigmoid(ga_ref[...].astype(F32)) * ya).astype(y_ref.dtype)

    for slot in (0, 1):
        pl.when(i % 2 == slot)(functools.partial(step, slot))


def _merge_b_kernel(ya_ref, ob_ref, zb_ref, gb_ref, pb_ref, y_ref):
    ub = (ob_ref[...].astype(F32) * _silu(zb_ref[...].astype(F32))).astype(BF16)
    yb = _dot(ub, pb_ref[...])
    y = ya_ref[...].astype(F32) + jax.nn.sigmoid(gb_ref[...].astype(F32)) * yb
    y_ref[...] = y.astype(y_ref.dtype)


def _merge(o_cmp, o_sel, o_win, gbr, cols2, o_b, cols3, pa, pb, tm=512):
    T = gbr.shape[0]
    D = D_MODEL
    tm = min(tm, T)
    row = lambda c: pl.BlockSpec((tm, D), lambda i: (i, c))
    const = lambda: pl.BlockSpec((D, D), lambda i: (0, 0), pipeline_mode=pl.Buffered(1))
    n = T // tm
    cur = lambda c: pl.BlockSpec((tm, D), lambda i: (jnp.minimum(i, n - 1), c))
    prev = lambda c: pl.BlockSpec((tm, D), lambda i: (jnp.maximum(i - 1, 0), c))
    ya = pl.pallas_call(
        _merge_a_kernel,
        grid=(n + 1,),
        in_specs=[cur(0), cur(0), cur(0),
                  pl.BlockSpec((tm, LANES), lambda i: (jnp.minimum(i, n - 1), 0)),
                  cur(R2_Z_A // D), prev(R3_GATE_A // D), const()],
        out_specs=prev(0),
        out_shape=jax.ShapeDtypeStruct((T, D), BF16),
        scratch_shapes=[pltpu.VMEM((2, tm, D), BF16)],
        compiler_params=_cparams(("arbitrary",)),
        name="merge_a",
    )(o_cmp, o_sel, o_win, gbr, cols2, cols3, pa)
    return pl.pallas_call(
        _merge_b_kernel,
        grid=(T // tm,),
        in_specs=[row(0), row(0), row(R3_Z_B // D), row(R3_GATE_B // D), const()],
        out_specs=row(0),
        out_shape=jax.ShapeDtypeStruct((T, D), BF16),
        compiler_params=_cparams(("parallel",)),
        name="merge_b",
    )(ya, o_b, cols3, cols3, pb)


def _outproj_kernel(x_ref, y_ref, w_ref, nw_ref, *out_refs, last):
    xn = x_ref[...] + _dot(y_ref[...], w_ref[...])
    ms = jnp.mean(xn * xn, axis=-1, keepdims=True)
    hn = xn * lax.rsqrt(ms + NORM_EPS) * nw_ref[...]
    if last:
        out_refs[0][...] = hn
    else:
        out_refs[0][...] = xn
        out_refs[1][...] = hn.astype(BF16)


def _outproj(x, y, w, nw, last, tm=512):
    T, D = x.shape
    tm = min(tm, T)
    row = pl.BlockSpec((tm, D), lambda i: (i, 0))
    if last:
        out_specs, out_shape = row, jax.ShapeDtypeStruct((T, D), F32)
    else:
        out_specs = [row, row]
        out_shape = [jax.ShapeDtypeStruct((T, D), F32), jax.ShapeDtypeStruct((T, D), BF16)]
    return pl.pallas_call(
        functools.partial(_outproj_kernel, last=last),
        grid=(T // tm,),
        in_specs=[
            row, row,
            pl.BlockSpec((D, D), lambda i: (0, 0), pipeline_mode=pl.Buffered(1)),
            pl.BlockSpec((1, D), lambda i: (0, 0)),
        ],
        out_specs=out_specs,
        out_shape=out_shape,
        compiler_params=_cparams(("parallel",)),
        name="outproj",
    )(x, y, w, nw)


def kernel(x, norm_w, w_in, cmp_pe, cmp_w1, cmp_b1, cmp_w2, cmp_b2, gla_alpha_w, gla_alpha_b, gla_norm_w,
           p_a, p_b, w_out, final_norm_w):
    B, T, D = x.shape
    L = norm_w.shape[0]
    assert B == 1 and D == D_MODEL and T % 512 == 0 and T // SEL_BLOCK <= NB_PAD
    assert w_in.shape[2] == SRC_END
    w_t = jnp.swapaxes(w_in, 1, 2)

    w1_b = cmp_w1.astype(BF16)
    w2_b = cmp_w2.astype(BF16)
    b1_r = cmp_b1.reshape(L, 2, 1, CMP_HIDDEN)
    b2_r = cmp_b2.reshape(L, 2, 1, A_DH)
    aw_p = jnp.pad(gla_alpha_w, ((0, 0), (0, LANES - B_RANK), (0, 0))).astype(BF16)
    ab_r = gla_alpha_b.reshape(L, 1, B_HEADS * B_DK)
    gnw_r = gla_norm_w.reshape(L, 1, B_DV)
    pa_b = p_a.astype(BF16)
    pb_b = p_b.astype(BF16)
    wo_b = w_out.astype(BF16)
    nw_next = jnp.concatenate([norm_w[1:], final_norm_w[None, :]], axis=0).reshape(L, 1, D)

    xs = x.reshape(T, D)
    h = _rmsnorm(xs, norm_w[0].reshape(1, D))
    for l in range(L):
        qa = _proj(h, w_t, l, SRC_Q_A, A_HEADS * A_DH, 512, BF16, scale=A_DH ** -0.5 * LOG2E)
        kv_cmp = _proj(h, w_t, l, SRC_KV_CMP, 2 * A_KVW, 512, F32)
        kvsw = _proj(h, w_t, l, SRC_KV_SW, 4 * A_KVW, 512, BF16)
        gbr = _proj(h, w_t, l, SRC_G_BR, LANES, LANES, BF16, valid=G_BR_COLS)
        cols2 = _proj(h, w_t, l, SRC_R2, SRC_A_LR - SRC_R2, 512, BF16)
        alr = _proj(h, w_t, l, SRC_A_LR, LANES, LANES, BF16, valid=B_RANK)
        cols3 = _proj(h, w_t, l, SRC_R3, SRC_END - SRC_R3, 512, BF16)
        kvc = _compress(kv_cmp, cmp_pe[l], w1_b[l], b1_r[l], w2_b[l], b2_r[l])
        o_cmp, qsel = _cmp_attn(qa, kvc)
        o_sel = _sel_attn(qa, kvsw, qsel)
        o_win = _win_attn(qa, kvsw)
        o_b = _gla(cols2, alr, aw_p[l], ab_r[l], gnw_r[l])
        y = _merge(o_cmp, o_sel, o_win, gbr, cols2, o_b, cols3, pa_b[l], pb_b[l])
        if l == L - 1:
            xs = _outproj(xs, y, wo_b[l], nw_next[l], last=True)
        else:
            xs, h = _outproj(xs, y, wo_b[l], nw_next[l], last=False)
    return xs.reshape(B, T, D)
```

```python
import functools

import jax
import jax.numpy as jnp
import numpy as np
from jax import lax
from jax.experimental import pallas as pl
from jax.experimental.pallas import tpu as pltpu

F32 = jnp.float32
BF16 = jnp.bfloat16

D_MODEL = 2048
A_HEADS = 16
A_GROUPS = 4
A_HPG = A_HEADS // A_GROUPS
A_DH = 128
A_KVW = A_GROUPS * A_DH
CMP_BLOCK = 32
CMP_STRIDE = 16
CMP_HIDDEN = 256
SEL_BLOCK = 64
SEL_SHIFT = 6
SEL_TOP_N = 16
WIN_SIZE = 512
B_HEADS = 4
B_DK = 256
B_DV = 512
B_RANK = 16
B_GATE_TEMP = 16.0
B_CHUNK = 64
NORM_EPS = 1e-6

LANES = 128
NB_PAD = LANES
MASK_BIG = 2.0 ** 100
LOG2E = float(np.log2(np.e))

SRC_Q_A = 0
SRC_KV_CMP = 2048
SRC_KV_SW = 3072
SRC_G_BR = 5120
SRC_R2 = 5168
SRC_A_LR = 11312
SRC_R3 = 11328
SRC_END = 17472
G_BR_COLS = 48
KVSW_K_SEL, KVSW_V_SEL, KVSW_K_WIN, KVSW_V_WIN = 0, 512, 1024, 1536
R2_Z_A, R2_Q_B, R2_K_B, R2_V_B = 0, 2048, 3072, 4096
R3_Z_B, R3_GATE_A, R3_GATE_B = 0, 2048, 4096

VMEM_LIMIT = 56 * 1024 * 1024


def _cparams(sem):
    return pltpu.CompilerParams(dimension_semantics=sem, vmem_limit_bytes=VMEM_LIMIT)


def _dot(a, b):
    return jnp.dot(a, b, preferred_element_type=F32)


def _dot_nt(a, b):
    return lax.dot_general(a, b, (((1,), (1,)), ((), ())), preferred_element_type=F32)


def _dot_tn(a, b):
    return lax.dot_general(a, b, (((0,), (0,)), ((), ())), preferred_element_type=F32)


def _split_bf16(x):
    hi = x.astype(BF16)
    lo = (x - hi.astype(F32)).astype(BF16)
    return hi, lo


def _rmsnorm_kernel(x_ref, w_ref, o_ref):
    x = x_ref[...]
    ms = jnp.mean(x * x, axis=-1, keepdims=True)
    o_ref[...] = (x * lax.rsqrt(ms + NORM_EPS) * w_ref[...]).astype(o_ref.dtype)


def _rmsnorm(x, w, tm=512):
    T, D = x.shape
    tm = min(tm, T)
    return pl.pallas_call(
        _rmsnorm_kernel,
        grid=(T // tm,),
        in_specs=[pl.BlockSpec((tm, D), lambda i: (i, 0)), pl.BlockSpec((1, D), lambda i: (0, 0))],
        out_specs=pl.BlockSpec((tm, D), lambda i: (i, 0)),
        out_shape=jax.ShapeDtypeStruct((T, D), BF16),
        compiler_params=_cparams(("parallel",)),
        name="rmsnorm",
    )(x, w)


def _proj_kernel(h_ref, *refs, shift, valid, scale):
    if shift:
        wa_ref, wb_ref, o_ref, wc_ref = refs
    else:
        wa_ref, o_ref, wc_ref = refs
    tn = o_ref.shape[1]

    @pl.when(pl.program_id(1) == 0)
    def _():
        if shift:
            wc_ref[0:tn - shift, :] = wa_ref[shift:tn, :].astype(BF16)
            wc_ref[tn - shift:tn, :] = wb_ref[0:shift, :].astype(BF16)
        else:
            wc_ref[...] = wa_ref[...].astype(BF16)
        if valid < tn:
            wc_ref[valid:tn, :] = jnp.zeros((tn - valid, wc_ref.shape[1]), BF16)

    r = _dot_nt(h_ref[...], wc_ref[...])
    if scale != 1.0:
        r = r * scale
    o_ref[...] = r.astype(o_ref.dtype)


def _proj(h, w_t, layer, col0, width, tn, out_dtype, scale=1.0, valid=None, tm=2048):
    T, D = h.shape
    tm = min(tm, T)
    shift = col0 % LANES
    a0 = col0 - shift
    valid = tn if valid is None else valid
    assert a0 % tn == 0 and width % tn == 0 and (valid == tn or width == tn)
    assert shift % 16 == 0 and valid % 16 == 0
    in_specs = [
        pl.BlockSpec((tm, D), lambda j, i: (i, 0)),
        pl.BlockSpec((None, tn, D), lambda j, i: (layer, a0 // tn + j, 0)),
    ]
    args = [h, w_t]
    if shift:
        in_specs.append(pl.BlockSpec((None, LANES, D), lambda j, i: (layer, (a0 + (j + 1) * tn) // LANES, 0)))
        args.append(w_t)
    return pl.pallas_call(
        functools.partial(_proj_kernel, shift=shift, valid=valid, scale=scale),
        grid=(width // tn, T // tm),
        in_specs=in_specs,
        out_specs=pl.BlockSpec((tm, tn), lambda j, i: (i, j)),
        out_shape=jax.ShapeDtypeStruct((T, width), out_dtype),
        scratch_shapes=[pltpu.VMEM((tn, D), BF16)],
        compiler_params=_cparams(("arbitrary", "arbitrary")),
        name="proj",
    )(*args)


def _gelu_tanh(x):
    c = np.float32(np.sqrt(2.0 / np.pi))
    return 0.5 * x * (1.0 + jnp.tanh(c * (x + np.float32(0.044715) * (x * x * x))))


def _compress_kernel(x_ref, pe_ref, w1_ref, b1_ref, w2_ref, b2_ref, o_ref, xlo_ref, xhi_ref):
    nch = o_ref.shape[2]
    half = CMP_STRIDE * A_DH
    for p in range(CMP_STRIDE):
        tok = x_ref[pl.ds(p, nch, stride=CMP_STRIDE), :]
        cs = slice(p * A_DH, (p + 1) * A_DH)
        xlo_ref[:, cs] = (tok + pe_ref[0, p:p + 1, :]).astype(BF16)
        xhi_ref[:, cs] = (tok + pe_ref[0, CMP_STRIDE + p:CMP_STRIDE + p + 1, :]).astype(BF16)
    a = _dot(xlo_ref[...], w1_ref[0, 0:half, :])
    b = _dot(xhi_ref[...], w1_ref[0, half:2 * half, :])
    b_next = pltpu.roll(b, nch - 1, axis=0)
    row = lax.broadcasted_iota(jnp.int32, b.shape, 0)
    b_next = jnp.where(row < nch - 1, b_next, 0.0)
    hid = _gelu_tanh(a + b_next + b1_ref[0])
    out = _dot(hid.astype(BF16), w2_ref[0]) + b2_ref[0]
    o_ref[0, 0] = out.astype(o_ref.dtype)


def _compress(kv, pe, w1, b1, w2, b2):
    T = kv.shape[0]
    G = A_GROUPS
    nch = T // CMP_STRIDE
    width = CMP_STRIDE * A_DH
    return pl.pallas_call(
        _compress_kernel,
        grid=(2, G),
        in_specs=[
            pl.BlockSpec((T, A_DH), lambda s, g: (0, s * G + g)),
            pl.BlockSpec((1, CMP_BLOCK, A_DH), lambda s, g: (s, 0, 0)),
            pl.BlockSpec((1, 2 * width, CMP_HIDDEN), lambda s, g: (s, 0, 0)),
            pl.BlockSpec((1, 1, CMP_HIDDEN), lambda s, g: (s, 0, 0)),
            pl.BlockSpec((1, CMP_HIDDEN, A_DH), lambda s, g: (s, 0, 0)),
            pl.BlockSpec((1, 1, A_DH), lambda s, g: (s, 0, 0)),
        ],
        out_specs=pl.BlockSpec((1, 1, nch, A_DH), lambda s, g: (s, g, 0, 0)),
        out_shape=jax.ShapeDtypeStruct((2, G, nch, A_DH), BF16),
        scratch_shapes=[pltpu.VMEM((nch, width), BF16), pltpu.VMEM((nch, width), BF16)],
        compiler_params=_cparams(("parallel", "parallel")),
        name="compress",
    )(kv, pe, w1, b1, w2, b2)


def _stack_heads(q_ref, qs_ref):
    tq = q_ref.shape[0]
    for h in range(A_HPG):
        qs_ref[h * tq:(h + 1) * tq, :] = q_ref[:, h * A_DH:(h + 1) * A_DH]


def _masked_chunks(s, masks, tq):
    out = []
    for c, mk in enumerate(masks):
        sc = s[:, c * LANES:(c + 1) * LANES]
        out.append(jnp.concatenate(
            [jnp.where(mk, sc[h * tq:(h + 1) * tq], -jnp.inf) for h in range(A_HPG)], axis=0))
    return out


def _chunk_softmax(sc):
    mx = sc[0]
    for x in sc[1:]:
        mx = jnp.maximum(mx, x)
    m = jnp.max(mx, axis=-1, keepdims=True)
    m = jnp.where(m > -jnp.inf, m, 0.0)
    pc = [jnp.exp2(x - m) for x in sc]
    ps = pc[0]
    for x in pc[1:]:
        ps = ps + x
    return pc, jnp.sum(ps, axis=-1, keepdims=True)


def _cmp_attn_kernel(q_ref, kc_ref, vc_ref, o_ref, qsel_ref, qs_ref, imp_ref):
    i = pl.program_id(1)

    @pl.when(i == 0)
    def _():
        imp_ref[1] = jnp.zeros(imp_ref.shape[1:], F32)

    for slot in (0, 1):
        pl.when(i % 2 == slot)(functools.partial(
            _cmp_attn_step, q_ref, kc_ref, vc_ref, o_ref, qsel_ref, qs_ref, imp_ref, slot))


def _cmp_attn_step(q_ref, kc_ref, vc_ref, o_ref, qsel_ref, qs_ref, imp_ref, slot):
    tq = q_ref.shape[0]
    nch = kc_ref.shape[2]
    i = pl.program_id(1)
    t0 = jnp.minimum(i, pl.num_programs(1) - 2) * tq
    _stack_heads(q_ref, qs_ref)
    s = _dot_nt(qs_ref[...], kc_ref[0, 0])
    tpos = t0 + lax.broadcasted_iota(jnp.int32, (tq, LANES), 0)
    lane = lax.broadcasted_iota(jnp.int32, (tq, LANES), 1)
    masks = [(lane + c * LANES) * CMP_STRIDE + (CMP_BLOCK - 1) <= tpos for c in range(nch // LANES)]
    pc, denom = _chunk_softmax(_masked_chunks(s, masks, tq))
    inv = 1.0 / jnp.where(denom > 0, denom, 1.0)
    o = _dot(jnp.concatenate([x.astype(BF16) for x in pc], axis=1), vc_ref[0, 0]) * inv
    for h in range(A_HPG):
        o_ref[:, h * A_DH:(h + 1) * A_DH] = o[h * tq:(h + 1) * tq].astype(o_ref.dtype)
    pn = [x * inv for x in pc]
    psum = jnp.concatenate(
        [sum(x[h * tq:(h + 1) * tq] for h in range(A_HPG)) for x in pn], axis=1)

    jj = lax.broadcasted_iota(jnp.int32, (NB_PAD, nch), 0)
    cs = lax.broadcasted_iota(jnp.int32, (NB_PAD, nch), 1) * CMP_STRIDE
    ov = ((cs <= jj * SEL_BLOCK + SEL_BLOCK - 1) & (cs + CMP_BLOCK - 1 >= jj * SEL_BLOCK))
    ov = jnp.where(ov, 1.0, 0.0).astype(BF16)
    p_hi, p_lo = _split_bf16(psum)
    imp_ref[slot] = _dot_nt(ov, p_hi) + _dot_nt(ov, p_lo)

    pslc = imp_ref[1 - slot]
    jf = lax.broadcasted_iota(jnp.int32, (NB_PAD, tq), 0)
    tl = jnp.maximum(i - 1, 0) * tq + lax.broadcasted_iota(jnp.int32, (NB_PAD, tq), 1)
    cur = jnp.right_shift(tl, SEL_SHIFT)
    valid = jf <= cur
    forced = (jf == 0) | (jf == cur) | (jf == cur - 1)
    score = jnp.where(valid & ~forced, pslc, -jnp.inf)
    jff = jf.astype(F32)
    for _ in range(SEL_TOP_N - 3):
        mx = jnp.max(score, axis=0, keepdims=True)
        first = jnp.min(jnp.where(score == mx, jff, float(NB_PAD)), axis=0, keepdims=True)
        score = jnp.where(jff == first, -jnp.inf, score)
    neg = jnp.where(valid & (score == -jnp.inf), 0.0, -MASK_BIG)
    qsel_ref[...] = neg.T.astype(qsel_ref.dtype)


def _cmp_attn(qa, kvc, tq=256):
    T = qa.shape[0]
    G = A_GROUPS
    nch = kvc.shape[2]
    tq = min(tq, T)
    qw = A_HPG * A_DH
    n = T // tq
    return pl.pallas_call(
        _cmp_attn_kernel,
        grid=(G, n + 1),
        in_specs=[
            pl.BlockSpec((tq, qw), lambda g, i: (jnp.minimum(i, n - 1), g)),
            pl.BlockSpec((1, 1, nch, A_DH), lambda g, i: (0, g, 0, 0)),
            pl.BlockSpec((1, 1, nch, A_DH), lambda g, i: (1, g, 0, 0)),
        ],
        out_specs=[
            pl.BlockSpec((tq, qw), lambda g, i: (jnp.minimum(i, n - 1), g)),
            pl.BlockSpec((tq, NB_PAD), lambda g, i: (jnp.maximum(i - 1, 0), g)),
        ],
        out_shape=[
            jax.ShapeDtypeStruct((T, A_HEADS * A_DH), BF16),
            jax.ShapeDtypeStruct((T, G * NB_PAD), BF16),
        ],
        scratch_shapes=[pltpu.VMEM((A_HPG * tq, A_DH), BF16), pltpu.VMEM((2, NB_PAD, tq), F32)],
        compiler_params=_cparams(("parallel", "arbitrary")),
        name="cmp_attn",
    )(qa, kvc, kvc)


def _sel_attn_kernel(q_ref, qsel_ref, k_ref, v_ref, oh_ref, o_ref, qa_ref, m_ref, l_ref, acc_ref):
    tq = q_ref.shape[0]
    tk = k_ref.shape[0]
    i = pl.program_id(1)
    kt = pl.program_id(2)
    last_kt = (i * tq + tq - 1) // tk
    nlc = tk // LANES

    @pl.when(kt == 0)
    def _():
        for h in range(A_HPG):
            qa_ref[h * tq:(h + 1) * tq, 0:A_DH] = q_ref[:, h * A_DH:(h + 1) * A_DH]
            qa_ref[h * tq:(h + 1) * tq, A_DH:2 * A_DH] = qsel_ref[...]
        m_ref[...] = jnp.full(m_ref.shape, -jnp.inf, F32)
        l_ref[...] = jnp.zeros(l_ref.shape, F32)
        acc_ref[...] = jnp.zeros(acc_ref.shape, F32)

    def step(diagonal):
        ka = jnp.concatenate([k_ref[...], oh_ref[...]], axis=1)
        s = _dot_nt(qa_ref[...], ka)
        sc = [s[:, c * LANES:(c + 1) * LANES] for c in range(nlc)]
        if diagonal:
            tpos = i * tq + lax.broadcasted_iota(jnp.int32, (tq, LANES), 0)
            lane = kt * tk + lax.broadcasted_iota(jnp.int32, (tq, LANES), 1)
            for c in range(nlc):
                causal = lane + c * LANES <= tpos
                sc[c] = jnp.concatenate(
                    [jnp.where(causal, sc[c][h * tq:(h + 1) * tq], -MASK_BIG) for h in range(A_HPG)], axis=0)
        mx = sc[0]
        for c in range(1, nlc):
            mx = jnp.maximum(mx, sc[c])
        m_old = m_ref[...]
        m_new = jnp.maximum(m_old, jnp.max(mx, axis=-1, keepdims=True))
        alpha = jnp.exp2(m_old - m_new)
        pc = [jnp.exp2(sc[c] - m_new) for c in range(nlc)]
        ps = pc[0]
        for c in range(1, nlc):
            ps = ps + pc[c]
        l_ref[...] = alpha * l_ref[...] + ps
        p = jnp.concatenate([x.astype(BF16) for x in pc], axis=1)
        acc_ref[...] = alpha * acc_ref[...] + _dot(p, v_ref[...])
        m_ref[...] = m_new

    pl.when(kt < last_kt)(functools.partial(step, False))
    pl.when(kt == last_kt)(functools.partial(step, True))

    @pl.when(kt == pl.num_programs(2) - 1)
    def _():
        for h in range(A_HPG):
            l = jnp.sum(l_ref[h * tq:(h + 1) * tq], axis=-1, keepdims=True)
            o = acc_ref[h * tq:(h + 1) * tq] / l
            o_ref[:, h * A_DH:(h + 1) * A_DH] = o.astype(o_ref.dtype)


def _sel_attn(qa, kvsw, qsel, tq=512, tk=512):
    T = qa.shape[0]
    G = A_GROUPS
    tq = min(tq, T)
    tk = min(tk, T)
    qw = A_HPG * A_DH
    rows = A_HPG * tq

    def kv_map(col, per_group):
        def f(g, i, kt):
            return (jnp.minimum(kt, (i * tq + tq - 1) // tk), col // A_DH + g * per_group)
        return f

    onehot = jnp.asarray(np.arange(T)[:, None] // SEL_BLOCK == np.arange(NB_PAD)[None, :], dtype=BF16)

    return pl.pallas_call(
        _sel_attn_kernel,
        grid=(G, T // tq, T // tk),
        in_specs=[
            pl.BlockSpec((tq, qw), lambda g, i, kt: (i, g)),
            pl.BlockSpec((tq, NB_PAD), lambda g, i, kt: (i, g)),
            pl.BlockSpec((tk, A_DH), kv_map(KVSW_K_SEL, 1)),
            pl.BlockSpec((tk, A_DH), kv_map(KVSW_V_SEL, 1)),
            pl.BlockSpec((tk, NB_PAD), kv_map(0, 0)),
        ],
        out_specs=pl.BlockSpec((tq, qw), lambda g, i, kt: (i, g)),
        out_shape=jax.ShapeDtypeStruct((T, A_HEADS * A_DH), BF16),
        scratch_shapes=[
            pltpu.VMEM((rows, 2 * A_DH), BF16),
            pltpu.VMEM((rows, LANES), F32),
            pltpu.VMEM((rows, LANES), F32),
            pltpu.VMEM((rows, A_DH), F32),
        ],
        compiler_params=_cparams(("parallel", "parallel", "arbitrary")),
        name="sel_attn",
    )(qa, qsel, kvsw, kvsw, onehot)


def _win_attn_kernel(q_ref, k0_ref, k1_ref, k2_ref, v0_ref, v1_ref, v2_ref, o_ref, qs_ref, p_ref, inv_ref):
    tq = q_ref.shape[0]
    i = pl.program_id(1)
    ia = jnp.minimum(i, pl.num_programs(1) - 2)

    @pl.when(i == 0)
    def _():
        p_ref[1] = jnp.zeros(p_ref.shape[1:], BF16)
        inv_ref[1] = jnp.ones(inv_ref.shape[1:], F32)

    def step(slot):
        k = jnp.concatenate([k0_ref[...], k1_ref[...], k2_ref[...]], axis=0)
        _stack_heads(q_ref, qs_ref)
        s = _dot_nt(qs_ref[...], k)
        r = lax.broadcasted_iota(jnp.int32, (tq, LANES), 0)
        lane = lax.broadcasted_iota(jnp.int32, (tq, LANES), 1)
        masks = []
        for c in range(3 * tq // LANES):
            col = lane + c * LANES
            rel = 2 * tq + r - col
            masks.append((rel >= 0) & (rel < WIN_SIZE) & ((ia - 2) * tq + col >= 0))
        pc, denom = _chunk_softmax(_masked_chunks(s, masks, tq))

        v = jnp.concatenate([v0_ref[...], v1_ref[...], v2_ref[...]], axis=0)
        o = _dot(p_ref[1 - slot], v) * inv_ref[1 - slot]
        for h in range(A_HPG):
            o_ref[:, h * A_DH:(h + 1) * A_DH] = o[h * tq:(h + 1) * tq].astype(o_ref.dtype)

        p_ref[slot] = jnp.concatenate([x.astype(BF16) for x in pc], axis=1)
        inv_ref[slot] = jnp.broadcast_to(1.0 / denom, inv_ref.shape[1:])

    for slot in (0, 1):
        pl.when(i % 2 == slot)(functools.partial(step, slot))


def _win_attn(qa, kvsw, tq=256):
    T = qa.shape[0]
    G = A_GROUPS
    tq = min(tq, T)
    assert WIN_SIZE <= 2 * tq
    qw = A_HPG * A_DH
    n = T // tq

    def kv_spec(col, back, lag):
        def f(g, i):
            tile = jnp.clip(i - lag, 0, n - 1)
            return (jnp.maximum(tile - back, 0), col // A_DH + g)
        return pl.BlockSpec((tq, A_DH), f)

    return pl.pallas_call(
        _win_attn_kernel,
        grid=(G, n + 1),
        in_specs=[pl.BlockSpec((tq, qw), lambda g, i: (jnp.minimum(i, n - 1), g)),
                  kv_spec(KVSW_K_WIN, 2, 0), kv_spec(KVSW_K_WIN, 1, 0), kv_spec(KVSW_K_WIN, 0, 0),
                  kv_spec(KVSW_V_WIN, 2, 1), kv_spec(KVSW_V_WIN, 1, 1), kv_spec(KVSW_V_WIN, 0, 1)],
        out_specs=pl.BlockSpec((tq, qw), lambda g, i: (jnp.maximum(i - 1, 0), g)),
        out_shape=jax.ShapeDtypeStruct((T, A_HEADS * A_DH), BF16),
        scratch_shapes=[pltpu.VMEM((A_HPG * tq, A_DH), BF16),
                        pltpu.VMEM((2, A_HPG * tq, 3 * tq), BF16),
                        pltpu.VMEM((2, A_HPG * tq, LANES), F32)],
        compiler_params=_cparams(("parallel", "arbitrary")),
        name="win_attn",
    )(qa, kvsw, kvsw, kvsw, kvsw, kvsw, kvsw)


def _gla_kernel(q_ref, k_ref, v_ref, a_ref, aw_ref, ab_ref, nw_ref, o_ref, st_ref):
    tc = q_ref.shape[0]
    C = B_CHUNK

    @pl.when(pl.program_id(0) == 0)
    def _():
        st_ref[...] = jnp.zeros(st_ref.shape, F32)

    logits = _dot(a_ref[...], aw_ref[...]) + ab_ref[...]
    log_a = jax.nn.log_sigmoid(logits) * (1.0 / B_GATE_TEMP)
    ri = lax.broadcasted_iota(jnp.int32, (C, C), 0)
    ci = lax.broadcasted_iota(jnp.int32, (C, C), 1)
    tril = ci <= ri
    tril_b = jnp.where(tril, 1.0, 0.0).astype(BF16)
    nw = nw_ref[...]
    for c in range(tc // C):
        sl = slice(c * C, (c + 1) * C)
        la_hi, la_lo = _split_bf16(log_a[sl])
        cum = _dot(tril_b, la_hi) + _dot(tril_b, la_lo)
        last = cum[C - 1:C, :]
        qc = q_ref[sl, :].astype(F32) * (B_DK ** -0.5)
        kc = k_ref[sl, :].astype(F32)
        q_dec = (qc * jnp.exp(cum)).astype(BF16)
        k_inv = (kc * jnp.exp(-cum)).astype(BF16)
        k_state = (kc * jnp.exp(last - cum)).astype(BF16)
        dec = jnp.exp(last)
        for h in range(B_HEADS):
            ks = slice(h * B_DK, (h + 1) * B_DK)
            vs = slice(h * B_DV, (h + 1) * B_DV)
            vc = v_ref[sl, vs]
            att = jnp.where(tril, _dot_nt(q_dec[:, ks], k_inv[:, ks]), 0.0)
            st = st_ref[h]
            o = _dot(att.astype(BF16), vc) + _dot_nt(q_dec[:, ks], st.astype(BF16))
            st_ref[h] = st * dec[:, ks] + _dot_tn(vc, k_state[:, ks])
            o = o * lax.rsqrt(jnp.mean(o * o, axis=-1, keepdims=True) + NORM_EPS) * nw
            o_ref[sl, vs] = o.astype(o_ref.dtype)


def _gla(cols2, alr, aw, ab, nw, tc=512):
    T = cols2.shape[0]
    tc = min(tc, T)
    kw = B_HEADS * B_DK
    vw = B_HEADS * B_DV
    return pl.pallas_call(
        _gla_kernel,
        grid=(T // tc,),
        in_specs=[
            pl.BlockSpec((tc, kw), lambda i: (i, R2_Q_B // kw)),
            pl.BlockSpec((tc, kw), lambda i: (i, R2_K_B // kw)),
            pl.BlockSpec((tc, vw), lambda i: (i, R2_V_B // vw)),
            pl.BlockSpec((tc, LANES), lambda i: (i, 0)),
            pl.BlockSpec((LANES, kw), lambda i: (0, 0)),
            pl.BlockSpec((1, kw), lambda i: (0, 0)),
            pl.BlockSpec((1, B_DV), lambda i: (0, 0)),
        ],
        out_specs=pl.BlockSpec((tc, vw), lambda i: (i, 0)),
        out_shape=jax.ShapeDtypeStruct((T, vw), BF16),
        scratch_shapes=[pltpu.VMEM((B_HEADS, B_DV, B_DK), F32)],
        compiler_params=_cparams(("arbitrary",)),
        name="gla",
    )(cols2, cols2, cols2, alr, aw, ab, nw)


def _silu(x):
    return x * jax.nn.sigmoid(x)


def _merge_a_kernel(ocmp_ref, osel_ref, owin_ref, gbr_ref, za_ref, ga_ref, pa_ref, y_ref, ua_ref):
    i = pl.program_id(0)

    @pl.when(i == 0)
    def _():
        ua_ref[1] = jnp.zeros(ua_ref.shape[1:], BF16)

    def step(slot):
        sg = jax.nn.sigmoid(gbr_ref[...].astype(F32))
        for hh in range(A_HEADS):
            cs = slice(hh * A_DH, (hh + 1) * A_DH)
            oa = (sg[:, 3 * hh:3 * hh + 1] * ocmp_ref[:, cs].astype(F32)
                  + sg[:, 3 * hh + 1:3 * hh + 2] * osel_ref[:, cs].astype(F32)
                  + sg[:, 3 * hh + 2:3 * hh + 3] * owin_ref[:, cs].astype(F32))
            ua_ref[slot, :, cs] = (oa * _silu(za_ref[:, cs].astype(F32))).astype(BF16)
        ya = _dot(ua_ref[1 - slot], pa_ref[...])
        y_ref[...] = (jax.nn.sigmoid(ga_ref[...].astype(F32)) * ya).astype(y_ref.dtype)

    for slot in (0, 1):
        pl.when(i % 2 == slot)(functools.partial(step, slot))


def _merge_b_kernel(ya_ref, ob_ref, zb_ref, gb_ref, pb_ref, y_ref):
    ub = (ob_ref[...].astype(F32) * _silu(zb_ref[...].astype(F32))).astype(BF16)
    yb = _dot(ub, pb_ref[...])
    y = ya_ref[...].astype(F32) + jax.nn.sigmoid(gb_ref[...].astype(F32)) * yb
    y_ref[...] = y.astype(y_ref.dtype)


def _merge(o_cmp, o_sel, o_win, gbr, cols2, o_b, cols3, pa, pb, tm=512):
    T = gbr.shape[0]
    D = D_MODEL
    tm = min(tm, T)
    row = lambda c: pl.BlockSpec((tm, D), lambda i: (i, c))
    const = lambda: pl.BlockSpec((D, D), lambda i: (0, 0), pipeline_mode=pl.Buffered(1))
    n = T // tm
    cur = lambda c: pl.BlockSpec((tm, D), lambda i: (jnp.minimum(i, n - 1), c))
    prev = lambda c: pl.BlockSpec((tm, D), lambda i: (jnp.maximum(i - 1, 0), c))
    ya = pl.pallas_call(
        _merge_a_kernel,
        grid=(n + 1,),
        in_specs=[cur(0), cur(0), cur(0),
                  pl.BlockSpec((tm, LANES), lambda i: (jnp.minimum(i, n - 1), 0)),
                  cur(R2_Z_A // D), prev(R3_GATE_A // D), const()],
        out_specs=prev(0),
        out_shape=jax.ShapeDtypeStruct((T, D), BF16),
        scratch_shapes=[pltpu.VMEM((2, tm, D), BF16)],
        compiler_params=_cparams(("arbitrary",)),
        name="merge_a",
    )(o_cmp, o_sel, o_win, gbr, cols2, cols3, pa)
    return pl.pallas_call(
        _merge_b_kernel,
        grid=(T // tm,),
        in_specs=[row(0), row(0), row(R3_Z_B // D), row(R3_GATE_B // D), const()],
        out_specs=row(0),
        out_shape=jax.ShapeDtypeStruct((T, D), BF16),
        compiler_params=_cparams(("parallel",)),
        name="merge_b",
    )(ya, o_b, cols3, cols3, pb)


def _outproj_kernel(x_ref, y_ref, w_ref, nw_ref, *out_refs, last):
    xn = x_ref[...] + _dot(y_ref[...], w_ref[...])
    ms = jnp.mean(xn * xn, axis=-1, keepdims=True)
    hn = xn * lax.rsqrt(ms + NORM_EPS) * nw_ref[...]
    if last:
        out_refs[0][...] = hn
    else:
        out_refs[0][...] = xn
        out_refs[1][...] = hn.astype(BF16)


def _outproj(x, y, w, nw, last, tm=512):
    T, D = x.shape
    tm = min(tm, T)
    row = pl.BlockSpec((tm, D), lambda i: (i, 0))
    if last:
        out_specs, out_shape = row, jax.ShapeDtypeStruct((T, D), F32)
    else:
        out_specs = [row, row]
        out_shape = [jax.ShapeDtypeStruct((T, D), F32), jax.ShapeDtypeStruct((T, D), BF16)]
    return pl.pallas_call(
        functools.partial(_outproj_kernel, last=last),
        grid=(T // tm,),
        in_specs=[
            row, row,
            pl.BlockSpec((D, D), lambda i: (0, 0), pipeline_mode=pl.Buffered(1)),
            pl.BlockSpec((1, D), lambda i: (0, 0)),
        ],
        out_specs=out_specs,
        out_shape=out_shape,
        compiler_params=_cparams(("parallel",)),
        name="outproj",
    )(x, y, w, nw)


def kernel(x, norm_w, w_in, cmp_pe, cmp_w1, cmp_b1, cmp_w2, cmp_b2, gla_alpha_w, gla_alpha_b, gla_norm_w,
           p_a, p_b, w_out, final_norm_w):
    B, T, D = x.shape
    L = norm_w.shape[0]
    assert B == 1 and D == D_MODEL and T % 512 == 0 and T // SEL_BLOCK <= NB_PAD
    assert w_in.shape[2] == SRC_END
    w_t = jnp.swapaxes(w_in, 1, 2)

    w1_b = cmp_w1.astype(BF16)
    w2_b = cmp_w2.astype(BF16)
    b1_r = cmp_b1.reshape(L, 2, 1, CMP_HIDDEN)
    b2_r = cmp_b2.reshape(L, 2, 1, A_DH)
    aw_p = jnp.pad(gla_alpha_w, ((0, 0), (0, LANES - B_RANK), (0, 0))).astype(BF16)
    ab_r = gla_alpha_b.reshape(L, 1, B_HEADS * B_DK)
    gnw_r = gla_norm_w.reshape(L, 1, B_DV)
    pa_b = p_a.astype(BF16)
    pb_b = p_b.astype(BF16)
    wo_b = w_out.astype(BF16)
    nw_next = jnp.concatenate([norm_w[1:], final_norm_w[None, :]], axis=0).reshape(L, 1, D)

    xs = x.reshape(T, D)
    h = _rmsnorm(xs, norm_w[0].reshape(1, D))
    for l in range(L):
        qa = _proj(h, w_t, l, SRC_Q_A, A_HEADS * A_DH, 512, BF16, scale=A_DH ** -0.5 * LOG2E)
        kv_cmp = _proj(h, w_t, l, SRC_KV_CMP, 2 * A_KVW, 512, F32)
        kvsw = _proj(h, w_t, l, SRC_KV_SW, 4 * A_KVW, 512, BF16)
        gbr = _proj(h, w_t, l, SRC_G_BR, LANES, LANES, BF16, valid=G_BR_COLS)
        cols2 = _proj(h, w_t, l, SRC_R2, SRC_A_LR - SRC_R2, 512, BF16)
        alr = _proj(h, w_t, l, SRC_A_LR, LANES, LANES, BF16, valid=B_RANK)
        cols3 = _proj(h, w_t, l, SRC_R3, SRC_END - SRC_R3, 512, BF16)
        kvc = _compress(kv_cmp, cmp_pe[l], w1_b[l], b1_r[l], w2_b[l], b2_r[l])
        o_cmp, qsel = _cmp_attn(qa, kvc)
        o_sel = _sel_attn(qa, kvsw, qsel)
        o_win = _win_attn(qa, kvsw)
        o_b = _gla(cols2, alr, aw_p[l], ab_r[l], gnw_r[l])
        y = _merge(o_cmp, o_sel, o_win, gbr, cols2, o_b, cols3, pa_b[l], pb_b[l])
        if l == L - 1:
            xs = _outproj(xs, y, wo_b[l], nw_next[l], last=True)
        else:
            xs, h = _outproj(xs, y, wo_b[l], nw_next[l], last=False)
    return xs.reshape(B, T, D)
```

```python
import functools

import jax
import jax.numpy as jnp
import numpy as np
from jax import lax
from jax.experimental import pallas as pl
from jax.experimental.pallas import tpu as pltpu

F32 = jnp.float32
BF16 = jnp.bfloat16

D_MODEL = 2048
A_HEADS = 16
A_GROUPS = 4
A_HPG = A_HEADS // A_GROUPS
A_DH = 128
A_KVW = A_GROUPS * A_DH
CMP_BLOCK = 32
CMP_STRIDE = 16
CMP_HIDDEN = 256
SEL_BLOCK = 64
SEL_SHIFT = 6
SEL_TOP_N = 16
WIN_SIZE = 512
B_HEADS = 4
B_DK = 256
B_DV = 512
B_RANK = 16
B_GATE_TEMP = 16.0
B_CHUNK = 64
NORM_EPS = 1e-6

LANES = 128
NB_PAD = LANES
MASK_BIG = 2.0 ** 100
LOG2E = float(np.log2(np.e))

SRC_Q_A = 0
SRC_KV_CMP = 2048
SRC_KV_SW = 3072
SRC_G_BR = 5120
SRC_R2 = 5168
SRC_A_LR = 11312
SRC_R3 = 11328
SRC_END = 17472
G_BR_COLS = 48
KVSW_K_SEL, KVSW_V_SEL, KVSW_K_WIN, KVSW_V_WIN = 0, 512, 1024, 1536
R2_Z_A, R2_Q_B, R2_K_B, R2_V_B = 0, 2048, 3072, 4096
R3_Z_B, R3_GATE_A, R3_GATE_B = 0, 2048, 4096

VMEM_LIMIT = 56 * 1024 * 1024


def _cparams(sem):
    return pltpu.CompilerParams(dimension_semantics=sem, vmem_limit_bytes=VMEM_LIMIT)


def _dot(a, b):
    return jnp.dot(a, b, preferred_element_type=F32)


def _dot_nt(a, b):
    return lax.dot_general(a, b, (((1,), (1,)), ((), ())), preferred_element_type=F32)


def _dot_tn(a, b):
    return lax.dot_general(a, b, (((0,), (0,)), ((), ())), preferred_element_type=F32)


def _split_bf16(x):
    hi = x.astype(BF16)
    lo = (x - hi.astype(F32)).astype(BF16)
    return hi, lo


def _rmsnorm_kernel(x_ref, w_ref, o_ref):
    x = x_ref[...]
    ms = jnp.mean(x * x, axis=-1, keepdims=True)
    o_ref[...] = (x * lax.rsqrt(ms + NORM_EPS) * w_ref[...]).astype(o_ref.dtype)


def _rmsnorm(x, w, tm=512):
    T, D = x.shape
    tm = min(tm, T)
    return pl.pallas_call(
        _rmsnorm_kernel,
        grid=(T // tm,),
        in_specs=[pl.BlockSpec((tm, D), lambda i: (i, 0)), pl.BlockSpec((1, D), lambda i: (0, 0))],
        out_specs=pl.BlockSpec((tm, D), lambda i: (i, 0)),
        out_shape=jax.ShapeDtypeStruct((T, D), BF16),
        compiler_params=_cparams(("parallel",)),
        name="rmsnorm",
    )(x, w)


def _proj_kernel(h_ref, *refs, shift, valid, scale):
    if shift:
        wa_ref, wb_ref, o_ref, wc_ref = refs
    else:
        wa_ref, o_ref, wc_ref = refs
    tn = o_ref.shape[1]

    @pl.when(pl.program_id(1) == 0)
    def _():
        if shift:
            wc_ref[0:tn - shift, :] = wa_ref[shift:tn, :].astype(BF16)
            wc_ref[tn - shift:tn, :] = wb_ref[0:shift, :].astype(BF16)
        else:
            wc_ref[...] = wa_ref[...].astype(BF16)
        if valid < tn:
            wc_ref[valid:tn, :] = jnp.zeros((tn - valid, wc_ref.shape[1]), BF16)

    r = _dot_nt(h_ref[...], wc_ref[...])
    if scale != 1.0:
        r = r * scale
    o_ref[...] = r.astype(o_ref.dtype)


def _proj(h, w_t, layer, col0, width, tn, out_dtype, scale=1.0, valid=None, tm=2048):
    T, D = h.shape
    tm = min(tm, T)
    shift = col0 % LANES
    a0 = col0 - shift
    valid = tn if valid is None else valid
    assert a0 % tn == 0 and width % tn == 0 and (valid == tn or width == tn)
    assert shift % 16 == 0 and valid % 16 == 0
    in_specs = [
        pl.BlockSpec((tm, D), lambda j, i: (i, 0)),
        pl.BlockSpec((None, tn, D), lambda j, i: (layer, a0 // tn + j, 0)),
    ]
    args = [h, w_t]
    if shift:
        in_specs.append(pl.BlockSpec((None, LANES, D), lambda j, i: (layer, (a0 + (j + 1) * tn) // LANES, 0)))
        args.append(w_t)
    return pl.pallas_call(
        functools.partial(_proj_kernel, shift=shift, valid=valid, scale=scale),
        grid=(width // tn, T // tm),
        in_specs=in_specs,
        out_specs=pl.BlockSpec((tm, tn), lambda j, i: (i, j)),
        out_shape=jax.ShapeDtypeStruct((T, width), out_dtype),
        scratch_shapes=[pltpu.VMEM((tn, D), BF16)],
        compiler_params=_cparams(("arbitrary", "arbitrary")),
        name="proj",
    )(*args)


def _gelu_tanh(x):
    c = np.float32(np.sqrt(2.0 / np.pi))
    return 0.5 * x * (1.0 + jnp.tanh(c * (x + np.float32(0.044715) * (x * x * x))))


def _compress_kernel(x_ref, pe_ref, w1_ref, b1_ref, w2_ref, b2_ref, o_ref, xlo_ref, xhi_ref):
    nch = o_ref.shape[2]
    half = CMP_STRIDE * A_DH
    for p in range(CMP_STRIDE):
        tok = x_ref[pl.ds(p, nch, stride=CMP_STRIDE), :]
        cs = slice(p * A_DH, (p + 1) * A_DH)
        xlo_ref[:, cs] = (tok + pe_ref[0, p:p + 1, :]).astype(BF16)
        xhi_ref[:, cs] = (tok + pe_ref[0, CMP_STRIDE + p:CMP_STRIDE + p + 1, :]).astype(BF16)
    a = _dot(xlo_ref[...], w1_ref[0, 0:half, :])
    b = _dot(xhi_ref[...], w1_ref[0, half:2 * half, :])
    b_next = pltpu.roll(b, nch - 1, axis=0)
    row = lax.broadcasted_iota(jnp.int32, b.shape, 0)
    b_next = jnp.where(row < nch - 1, b_next, 0.0)
    hid = _gelu_tanh(a + b_next + b1_ref[0])
    out = _dot(hid.astype(BF16), w2_ref[0]) + b2_ref[0]
    o_ref[0, 0] = out.astype(o_ref.dtype)


def _compress(kv, pe, w1, b1, w2, b2):
    T = kv.shape[0]
    G = A_GROUPS
    nch = T // CMP_STRIDE
    width = CMP_STRIDE * A_DH
    return pl.pallas_call(
        _compress_kernel,
        grid=(2, G),
        in_specs=[
            pl.BlockSpec((T, A_DH), lambda s, g: (0, s * G + g)),
            pl.BlockSpec((1, CMP_BLOCK, A_DH), lambda s, g: (s, 0, 0)),
            pl.BlockSpec((1, 2 * width, CMP_HIDDEN), lambda s, g: (s, 0, 0)),
            pl.BlockSpec((1, 1, CMP_HIDDEN), lambda s, g: (s, 0, 0)),
            pl.BlockSpec((1, CMP_HIDDEN, A_DH), lambda s, g: (s, 0, 0)),
            pl.BlockSpec((1, 1, A_DH), lambda s, g: (s, 0, 0)),
        ],
        out_specs=pl.BlockSpec((1, 1, nch, A_DH), lambda s, g: (s, g, 0, 0)),
        out_shape=jax.ShapeDtypeStruct((2, G, nch, A_DH), BF16),
        scratch_shapes=[pltpu.VMEM((nch, width), BF16), pltpu.VMEM((nch, width), BF16)],
        compiler_params=_cparams(("parallel", "parallel")),
        name="compress",
    )(kv, pe, w1, b1, w2, b2)


def _stack_heads(q_ref, qs_ref):
    tq = q_ref.shape[0]
    for h in range(A_HPG):
        qs_ref[h * tq:(h + 1) * tq, :] = q_ref[:, h * A_DH:(h + 1) * A_DH]


def _masked_chunks(s, masks, tq):
    out = []
    for c, mk in enumerate(masks):
        sc = s[:, c * LANES:(c + 1) * LANES]
        out.append(jnp.concatenate(
            [jnp.where(mk, sc[h * tq:(h + 1) * tq], -jnp.inf) for h in range(A_HPG)], axis=0))
    return out


def _chunk_softmax(sc):
    mx = sc[0]
    for x in sc[1:]:
        mx = jnp.maximum(mx, x)
    m = jnp.max(mx, axis=-1, keepdims=True)
    m = jnp.where(m > -jnp.inf, m, 0.0)
    pc = [jnp.exp2(x - m) for x in sc]
    ps = pc[0]
    for x in pc[1:]:
        ps = ps + x
    return pc, jnp.sum(ps, axis=-1, keepdims=True)


def _cmp_attn_kernel(q_ref, kc_ref, vc_ref, o_ref, qsel_ref, qs_ref, imp_ref):
    i = pl.program_id(1)

    @pl.when(i == 0)
    def _():
        imp_ref[1] = jnp.zeros(imp_ref.shape[1:], F32)

    for slot in (0, 1):
        pl.when(i % 2 == slot)(functools.partial(
            _cmp_attn_step, q_ref, kc_ref, vc_ref, o_ref, qsel_ref, qs_ref, imp_ref, slot))


def _cmp_attn_step(q_ref, kc_ref, vc_ref, o_ref, qsel_ref, qs_ref, imp_ref, slot):
    tq = q_ref.shape[0]
    nch = kc_ref.shape[2]
    i = pl.program_id(1)
    t0 = jnp.minimum(i, pl.num_programs(1) - 2) * tq
    _stack_heads(q_ref, qs_ref)
    s = _dot_nt(qs_ref[...], kc_ref[0, 0])
    tpos = t0 + lax.broadcasted_iota(jnp.int32, (tq, LANES), 0)
    lane = lax.broadcasted_iota(jnp.int32, (tq, LANES), 1)
    masks = [(lane + c * LANES) * CMP_STRIDE + (CMP_BLOCK - 1) <= tpos for c in range(nch // LANES)]
    pc, denom = _chunk_softmax(_masked_chunks(s, masks, tq))
    inv = 1.0 / jnp.where(denom > 0, denom, 1.0)
    o = _dot(jnp.concatenate([x.astype(BF16) for x in pc], axis=1), vc_ref[0, 0]) * inv
    for h in range(A_HPG):
        o_ref[:, h * A_DH:(h + 1) * A_DH] = o[h * tq:(h + 1) * tq].astype(o_ref.dtype)
    pn = [x * inv for x in pc]
    psum = jnp.concatenate(
        [sum(x[h * tq:(h + 1) * tq] for h in range(A_HPG)) for x in pn], axis=1)

    jj = lax.broadcasted_iota(jnp.int32, (NB_PAD, nch), 0)
    cs = lax.broadcasted_iota(jnp.int32, (NB_PAD, nch), 1) * CMP_STRIDE
    ov = ((cs <= jj * SEL_BLOCK + SEL_BLOCK - 1) & (cs + CMP_BLOCK - 1 >= jj * SEL_BLOCK))
    ov = jnp.where(ov, 1.0, 0.0).astype(BF16)
    p_hi, p_lo = _split_bf16(psum)
    imp_ref[slot] = _dot_nt(ov, p_hi) + _dot_nt(ov, p_lo)

    pslc = imp_ref[1 - slot]
    jf = lax.broadcasted_iota(jnp.int32, (NB_PAD, tq), 0)
    tl = jnp.maximum(i - 1, 0) * tq + lax.broadcasted_iota(jnp.int32, (NB_PAD, tq), 1)
    cur = jnp.right_shift(tl, SEL_SHIFT)
    valid = jf <= cur
    forced = (jf == 0) | (jf == cur) | (jf == cur - 1)
    score = jnp.where(valid & ~forced, pslc, -jnp.inf)
    jff = jf.astype(F32)
    for _ in range(SEL_TOP_N - 3):
        mx = jnp.max(score, axis=0, keepdims=True)
        first = jnp.min(jnp.where(score == mx, jff, float(NB_PAD)), axis=0, keepdims=True)
        score = jnp.where(jff == first, -jnp.inf, score)
    neg = jnp.where(valid & (score == -jnp.inf), 0.0, -MASK_BIG)
    qsel_ref[...] = neg.T.astype(qsel_ref.dtype)


def _cmp_attn(qa, kvc, tq=256):
    T = qa.shape[0]
    G = A_GROUPS
    nch = kvc.shape[2]
    tq = min(tq, T)
    qw = A_HPG * A_DH
    n = T // tq
    return pl.pallas_call(
        _cmp_attn_kernel,
        grid=(G, n + 1),
        in_specs=[
            pl.BlockSpec((tq, qw), lambda g, i: (jnp.minimum(i, n - 1), g)),
            pl.BlockSpec((1, 1, nch, A_DH), lambda g, i: (0, g, 0, 0)),
            pl.BlockSpec((1, 1, nch, A_DH), lambda g, i: (1, g, 0, 0)),
        ],
        out_specs=[
            pl.BlockSpec((tq, qw), lambda g, i: (jnp.minimum(i, n - 1), g)),
            pl.BlockSpec((tq, NB_PAD), lambda g, i: (jnp.maximum(i - 1, 0), g)),
        ],
        out_shape=[
            jax.ShapeDtypeStruct((T, A_HEADS * A_DH), BF16),
            jax.ShapeDtypeStruct((T, G * NB_PAD), BF16),
        ],
        scratch_shapes=[pltpu.VMEM((A_HPG * tq, A_DH), BF16), pltpu.VMEM((2, NB_PAD, tq), F32)],
        compiler_params=_cparams(("parallel", "arbitrary")),
        name="cmp_attn",
    )(qa, kvc, kvc)


def _sel_attn_kernel(q_ref, qsel_ref, k_ref, v_ref, oh_ref, o_ref, qa_ref, m_ref, l_ref, acc_ref, *, tk):
    tq = q_ref.shape[0]
    nsub = k_ref.shape[0] // tk
    i = pl.program_id(1)
    kp = pl.program_id(2)
    last_kt = (i * tq + tq - 1) // tk
    nlc = tk // LANES

    @pl.when(kp == 0)
    def _():
        for h in range(A_HPG):
            qa_ref[h * tq:(h + 1) * tq, 0:A_DH] = q_ref[:, h * A_DH:(h + 1) * A_DH]
            qa_ref[h * tq:(h + 1) * tq, A_DH:2 * A_DH] = qsel_ref[...]
        m_ref[...] = jnp.full(m_ref.shape, -jnp.inf, F32)
        l_ref[...] = jnp.zeros(l_ref.shape, F32)
        acc_ref[...] = jnp.zeros(acc_ref.shape, F32)

    def step(diagonal, kt, off):
        rows = pl.ds(off, tk)
        ka = jnp.concatenate([k_ref[rows, :], oh_ref[rows, :]], axis=1)
        s = _dot_nt(qa_ref[...], ka)
        sc = [s[:, c * LANES:(c + 1) * LANES] for c in range(nlc)]
        if diagonal:
            tpos = i * tq + lax.broadcasted_iota(jnp.int32, (tq, LANES), 0)
            lane = kt * tk + lax.broadcasted_iota(jnp.int32, (tq, LANES), 1)
            for c in range(nlc):
                causal = lane + c * LANES <= tpos
                sc[c] = jnp.concatenate(
                    [jnp.where(causal, sc[c][h * tq:(h + 1) * tq], -MASK_BIG) for h in range(A_HPG)], axis=0)
        mx = sc[0]
        for c in range(1, nlc):
            mx = jnp.maximum(mx, sc[c])
        m_old = m_ref[...]
        m_new = jnp.maximum(m_old, jnp.max(mx, axis=-1, keepdims=True))
        alpha = jnp.exp2(m_old - m_new)
        pc = [jnp.exp2(sc[c] - m_new) for c in range(nlc)]
        ps = pc[0]
        for c in range(1, nlc):
            ps = ps + pc[c]
        l_ref[...] = alpha * l_ref[...] + ps
        p = jnp.concatenate([x.astype(BF16) for x in pc], axis=1)
        acc_ref[...] = alpha * acc_ref[...] + _dot(p, v_ref[rows, :])
        m_ref[...] = m_new

    def sub_tile(sub, carry):
        kt = kp * nsub + sub
        off = pl.multiple_of(sub * tk, tk)
        pl.when(kt < last_kt)(functools.partial(step, False, kt, off))
        pl.when(kt == last_kt)(functools.partial(step, True, kt, off))
        return carry

    lax.fori_loop(0, nsub, sub_tile, 0)

    @pl.when(kp == pl.num_programs(2) - 1)
    def _():
        for h in range(A_HPG):
            l = jnp.sum(l_ref[h * tq:(h + 1) * tq], axis=-1, keepdims=True)
            o = acc_ref[h * tq:(h + 1) * tq] / l
            o_ref[:, h * A_DH:(h + 1) * A_DH] = o.astype(o_ref.dtype)


def _sel_attn(qa, kvsw, qsel, tq=512, tk=512, kb=2048):
    T = qa.shape[0]
    G = A_GROUPS
    tq = min(tq, T)
    tk = min(tk, T)
    kb = min(kb, T)
    qw = A_HPG * A_DH
    rows = A_HPG * tq

    def kv_map(col, per_group):
        def f(g, i, kp):
            return (jnp.minimum(kp, (i * tq + tq - 1) // kb), col // A_DH + g * per_group)
        return f

    onehot = jnp.asarray(np.arange(T)[:, None] // SEL_BLOCK == np.arange(NB_PAD)[None, :], dtype=BF16)

    return pl.pallas_call(
        functools.partial(_sel_attn_kernel, tk=tk),
        grid=(G, T // tq, T // kb),
        in_specs=[
            pl.BlockSpec((tq, qw), lambda g, i, kp: (i, g)),
            pl.BlockSpec((tq, NB_PAD), lambda g, i, kp: (i, g)),
            pl.BlockSpec((kb, A_DH), kv_map(KVSW_K_SEL, 1)),
            pl.BlockSpec((kb, A_DH), kv_map(KVSW_V_SEL, 1)),
            pl.BlockSpec((kb, NB_PAD), kv_map(0, 0)),
        ],
        out_specs=pl.BlockSpec((tq, qw), lambda g, i, kp: (i, g)),
        out_shape=jax.ShapeDtypeStruct((T, A_HEADS * A_DH), BF16),
        scratch_shapes=[
            pltpu.VMEM((rows, 2 * A_DH), BF16),
            pltpu.VMEM((rows, LANES), F32),
            pltpu.VMEM((rows, LANES), F32),
            pltpu.VMEM((rows, A_DH), F32),
        ],
        compiler_params=_cparams(("parallel", "parallel", "arbitrary")),
        name="sel_attn",
    )(qa, qsel, kvsw, kvsw, onehot)


def _win_attn_kernel(q_ref, k0_ref, k1_ref, k2_ref, v0_ref, v1_ref, v2_ref, o_ref, qs_ref, p_ref, inv_ref):
    tq = q_ref.shape[0]
    i = pl.program_id(1)
    ia = jnp.minimum(i, pl.num_programs(1) - 2)

    @pl.when(i == 0)
    def _():
        p_ref[1] = jnp.zeros(p_ref.shape[1:], BF16)
        inv_ref[1] = jnp.ones(inv_ref.shape[1:], F32)

    def step(slot):
        k = jnp.concatenate([k0_ref[...], k1_ref[...], k2_ref[...]], axis=0)
        _stack_heads(q_ref, qs_ref)
        s = _dot_nt(qs_ref[...], k)
        r = lax.broadcasted_iota(jnp.int32, (tq, LANES), 0)
        lane = lax.broadcasted_iota(jnp.int32, (tq, LANES), 1)
        masks = []
        for c in range(3 * tq // LANES):
            col = lane + c * LANES
            rel = 2 * tq + r - col
            masks.append((rel >= 0) & (rel < WIN_SIZE) & ((ia - 2) * tq + col >= 0))
        pc, denom = _chunk_softmax(_masked_chunks(s, masks, tq))

        v = jnp.concatenate([v0_ref[...], v1_ref[...], v2_ref[...]], axis=0)
        o = _dot(p_ref[1 - slot], v) * inv_ref[1 - slot]
        for h in range(A_HPG):
            o_ref[:, h * A_DH:(h + 1) * A_DH] = o[h * tq:(h + 1) * tq].astype(o_ref.dtype)

        p_ref[slot] = jnp.concatenate([x.astype(BF16) for x in pc], axis=1)
        inv_ref[slot] = jnp.broadcast_to(1.0 / denom, inv_ref.shape[1:])

    for slot in (0, 1):
        pl.when(i % 2 == slot)(functools.partial(step, slot))


def _win_attn(qa, kvsw, tq=256):
    T = qa.shape[0]
    G = A_GROUPS
    tq = min(tq, T)
    assert WIN_SIZE <= 2 * tq
    qw = A_HPG * A_DH
    n = T // tq

    def kv_spec(col, back, lag):
        def f(g, i):
            tile = jnp.clip(i - lag, 0, n - 1)
            return (jnp.maximum(tile - back, 0), col // A_DH + g)
        return pl.BlockSpec((tq, A_DH), f)

    return pl.pallas_call(
        _win_attn_kernel,
        grid=(G, n + 1),
        in_specs=[pl.BlockSpec((tq, qw), lambda g, i: (jnp.minimum(i, n - 1), g)),
                  kv_spec(KVSW_K_WIN, 2, 0), kv_spec(KVSW_K_WIN, 1, 0), kv_spec(KVSW_K_WIN, 0, 0),
                  kv_spec(KVSW_V_WIN, 2, 1), kv_spec(KVSW_V_WIN, 1, 1), kv_spec(KVSW_V_WIN, 0, 1)],
        out_specs=pl.BlockSpec((tq, qw), lambda g, i: (jnp.maximum(i - 1, 0), g)),
        out_shape=jax.ShapeDtypeStruct((T, A_HEADS * A_DH), BF16),
        scratch_shapes=[pltpu.VMEM((A_HPG * tq, A_DH), BF16),
                        pltpu.VMEM((2, A_HPG * tq, 3 * tq), BF16),
                        pltpu.VMEM((2, A_HPG * tq, LANES), F32)],
        compiler_params=_cparams(("parallel", "arbitrary")),
        name="win_attn",
    )(qa, kvsw, kvsw, kvsw, kvsw, kvsw, kvsw)


def _gla_kernel(q_ref, k_ref, v_ref, a_ref, aw_ref, ab_ref, nw_ref, o_ref, st_ref):
    tc = q_ref.shape[0]
    C = B_CHUNK

    @pl.when(pl.program_id(0) == 0)
    def _():
        st_ref[...] = jnp.zeros(st_ref.shape, F32)

    logits = _dot(a_ref[...], aw_ref[...]) + ab_ref[...]
    log_a = jax.nn.log_sigmoid(logits) * (1.0 / B_GATE_TEMP)
    ri = lax.broadcasted_iota(jnp.int32, (C, C), 0)
    ci = lax.broadcasted_iota(jnp.int32, (C, C), 1)
    tril = ci <= ri
    tril_b = jnp.where(tril, 1.0, 0.0).astype(BF16)
    nw = nw_ref[...]
    for c in range(tc // C):
        sl = slice(c * C, (c + 1) * C)
        la_hi, la_lo = _split_bf16(log_a[sl])
        cum = _dot(tril_b, la_hi) + _dot(tril_b, la_lo)
        last = cum[C - 1:C, :]
        qc = q_ref[sl, :].astype(F32) * (B_DK ** -0.5)
        kc = k_ref[sl, :].astype(F32)
        q_dec = (qc * jnp.exp(cum)).astype(BF16)
        k_inv = (kc * jnp.exp(-cum)).astype(BF16)
        k_state = (kc * jnp.exp(last - cum)).astype(BF16)
        dec = jnp.exp(last)
        for h in range(B_HEADS):
            ks = slice(h * B_DK, (h + 1) * B_DK)
            vs = slice(h * B_DV, (h + 1) * B_DV)
            vc = v_ref[sl, vs]
            att = jnp.where(tril, _dot_nt(q_dec[:, ks], k_inv[:, ks]), 0.0)
            st = st_ref[h]
            o = _dot(att.astype(BF16), vc) + _dot_nt(q_dec[:, ks], st.astype(BF16))
            st_ref[h] = st * dec[:, ks] + _dot_tn(vc, k_state[:, ks])
            o = o * lax.rsqrt(jnp.mean(o * o, axis=-1, keepdims=True) + NORM_EPS) * nw
            o_ref[sl, vs] = o.astype(o_ref.dtype)


def _gla(cols2, alr, aw, ab, nw, tc=512):
    T = cols2.shape[0]
    tc = min(tc, T)
    kw = B_HEADS * B_DK
    vw = B_HEADS * B_DV
    return pl.pallas_call(
        _gla_kernel,
        grid=(T // tc,),
        in_specs=[
            pl.BlockSpec((tc, kw), lambda i: (i, R2_Q_B // kw)),
            pl.BlockSpec((tc, kw), lambda i: (i, R2_K_B // kw)),
            pl.BlockSpec((tc, vw), lambda i: (i, R2_V_B // vw)),
            pl.BlockSpec((tc, LANES), lambda i: (i, 0)),
            pl.BlockSpec((LANES, kw), lambda i: (0, 0)),
            pl.BlockSpec((1, kw), lambda i: (0, 0)),
            pl.BlockSpec((1, B_DV), lambda i: (0, 0)),
        ],
        out_specs=pl.BlockSpec((tc, vw), lambda i: (i, 0)),
        out_shape=jax.ShapeDtypeStruct((T, vw), BF16),
        scratch_shapes=[pltpu.VMEM((B_HEADS, B_DV, B_DK), F32)],
        compiler_params=_cparams(("arbitrary",)),
        name="gla",
    )(cols2, cols2, cols2, alr, aw, ab, nw)


def _silu(x):
    return x * jax.nn.sigmoid(x)


def _merge_a_kernel(ocmp_ref, osel_ref, owin_ref, gbr_ref, za_ref, ga_ref, pa_ref, y_ref, ua_ref):
    i = pl.program_id(0)

    @pl.when(i == 0)
    def _():
        ua_ref[1] = jnp.zeros(ua_ref.shape[1:], BF16)

    def step(slot):
        sg = jax.nn.sigmoid(gbr_ref[...].astype(F32))
        for hh in range(A_HEADS):
            cs = slice(hh * A_DH, (hh + 1) * A_DH)
            oa = (sg[:, 3 * hh:3 * hh + 1] * ocmp_ref[:, cs].astype(F32)
                  + sg[:, 3 * hh + 1:3 * hh + 2] * osel_ref[:, cs].astype(F32)
                  + sg[:, 3 * hh + 2:3 * hh + 3] * owin_ref[:, cs].astype(F32))
            ua_ref[slot, :, cs] = (oa * _silu(za_ref[:, cs].astype(F32))).astype(BF16)
        ya = _dot(ua_ref[1 - slot], pa_ref[...])
        y_ref[...] = (jax.nn.sigmoid(ga_ref[...].astype(F32)) * ya).astype(y_ref.dtype)

    for slot in (0, 1):
        pl.when(i % 2 == slot)(functools.partial(step, slot))


def _merge_b_kernel(ya_ref, ob_ref, zb_ref, gb_ref, pb_ref, y_ref):
    ub = (ob_ref[...].astype(F32) * _silu(zb_ref[...].astype(F32))).astype(BF16)
    yb = _dot(ub, pb_ref[...])
    y = ya_ref[...].astype(F32) + jax.nn.sigmoid(gb_ref[...].astype(F32)) * yb
    y_ref[...] = y.astype(y_ref.dtype)


def _merge(o_cmp, o_sel, o_win, gbr, cols2, o_b, cols3, pa, pb, tm=512):
    T = gbr.shape[0]
    D = D_MODEL
    tm = min(tm, T)
    row = lambda c: pl.BlockSpec((tm, D), lambda i: (i, c))
    const = lambda: pl.BlockSpec((D, D), lambda i: (0, 0), pipeline_mode=pl.Buffered(1))
    n = T // tm
    cur = lambda c: pl.BlockSpec((tm, D), lambda i: (jnp.minimum(i, n - 1), c))
    prev = lambda c: pl.BlockSpec((tm, D), lambda i: (jnp.maximum(i - 1, 0), c))
    ya = pl.pallas_call(
        _merge_a_kernel,
        grid=(n + 1,),
        in_specs=[cur(0), cur(0), cur(0),
                  pl.BlockSpec((tm, LANES), lambda i: (jnp.minimum(i, n - 1), 0)),
                  cur(R2_Z_A // D), prev(R3_GATE_A // D), const()],
        out_specs=prev(0),
        out_shape=jax.ShapeDtypeStruct((T, D), BF16),
        scratch_shapes=[pltpu.VMEM((2, tm, D), BF16)],
        compiler_params=_cparams(("arbitrary",)),
        name="merge_a",
    )(o_cmp, o_sel, o_win, gbr, cols2, cols3, pa)
    return pl.pallas_call(
        _merge_b_kernel,
        grid=(T // tm,),
        in_specs=[row(0), row(0), row(R3_Z_B // D), row(R3_GATE_B // D), const()],
        out_specs=row(0),
        out_shape=jax.ShapeDtypeStruct((T, D), BF16),
        compiler_params=_cparams(("parallel",)),
        name="merge_b",
    )(ya, o_b, cols3, cols3, pb)


def _outproj_kernel(x_ref, y_ref, w_ref, nw_ref, *out_refs, last):
    xn = x_ref[...] + _dot(y_ref[...], w_ref[...])
    ms = jnp.mean(xn * xn, axis=-1, keepdims=True)
    hn = xn * lax.rsqrt(ms + NORM_EPS) * nw_ref[...]
    if last:
        out_refs[0][...] = hn
    else:
        out_refs[0][...] = xn
        out_refs[1][...] = hn.astype(BF16)


def _outproj(x, y, w, nw, last, tm=512):
    T, D = x.shape
    tm = min(tm, T)
    row = pl.BlockSpec((tm, D), lambda i: (i, 0))
    if last:
        out_specs, out_shape = row, jax.ShapeDtypeStruct((T, D), F32)
    else:
        out_specs = [row, row]
        out_shape = [jax.ShapeDtypeStruct((T, D), F32), jax.ShapeDtypeStruct((T, D), BF16)]
    return pl.pallas_call(
        functools.partial(_outproj_kernel, last=last),
        grid=(T // tm,),
        in_specs=[
            row, row,
            pl.BlockSpec((D, D), lambda i: (0, 0), pipeline_mode=pl.Buffered(1)),
            pl.BlockSpec((1, D), lambda i: (0, 0)),
        ],
        out_specs=out_specs,
        out_shape=out_shape,
        compiler_params=_cparams(("parallel",)),
        name="outproj",
    )(x, y, w, nw)


def kernel(x, norm_w, w_in, cmp_pe, cmp_w1, cmp_b1, cmp_w2, cmp_b2, gla_alpha_w, gla_alpha_b, gla_norm_w,
           p_a, p_b, w_out, final_norm_w):
    B, T, D = x.shape
    L = norm_w.shape[0]
    assert B == 1 and D == D_MODEL and T % 512 == 0 and T // SEL_BLOCK <= NB_PAD
    assert w_in.shape[2] == SRC_END
    w_t = jnp.swapaxes(w_in, 1, 2)

    w1_b = cmp_w1.astype(BF16)
    w2_b = cmp_w2.astype(BF16)
    b1_r = cmp_b1.reshape(L, 2, 1, CMP_HIDDEN)
    b2_r = cmp_b2.reshape(L, 2, 1, A_DH)
    aw_p = jnp.pad(gla_alpha_w, ((0, 0), (0, LANES - B_RANK), (0, 0))).astype(BF16)
    ab_r = gla_alpha_b.reshape(L, 1, B_HEADS * B_DK)
    gnw_r = gla_norm_w.reshape(L, 1, B_DV)
    pa_b = p_a.astype(BF16)
    pb_b = p_b.astype(BF16)
    wo_b = w_out.astype(BF16)
    nw_next = jnp.concatenate([norm_w[1:], final_norm_w[None, :]], axis=0).reshape(L, 1, D)

    xs = x.reshape(T, D)
    h = _rmsnorm(xs, norm_w[0].reshape(1, D))
    for l in range(L):
        qa = _proj(h, w_t, l, SRC_Q_A, A_HEADS * A_DH, 512, BF16, scale=A_DH ** -0.5 * LOG2E)
        kv_cmp = _proj(h, w_t, l, SRC_KV_CMP, 2 * A_KVW, 512, F32)
        kvsw = _proj(h, w_t, l, SRC_KV_SW, 4 * A_KVW, 512, BF16)
        gbr = _proj(h, w_t, l, SRC_G_BR, LANES, LANES, BF16, valid=G_BR_COLS)
        cols2 = _proj(h, w_t, l, SRC_R2, SRC_A_LR - SRC_R2, 512, BF16)
        alr = _proj(h, w_t, l, SRC_A_LR, LANES, LANES, BF16, valid=B_RANK)
        cols3 = _proj(h, w_t, l, SRC_R3, SRC_END - SRC_R3, 512, BF16)
        kvc = _compress(kv_cmp, cmp_pe[l], w1_b[l], b1_r[l], w2_b[l], b2_r[l])
        o_cmp, qsel = _cmp_attn(qa, kvc)
        o_sel = _sel_attn(qa, kvsw, qsel)
        o_win = _win_attn(qa, kvsw)
        o_b = _gla(cols2, alr, aw_p[l], ab_r[l], gnw_r[l])
        y = _merge(o_cmp, o_sel, o_win, gbr, cols2, o_b, cols3, pa_b[l], pb_b[l])
        if l == L - 1:
            xs = _outproj(xs, y, wo_b[l], nw_next[l], last=True)
        else:
            xs, h = _outproj(xs, y, wo_b[l], nw_next[l], last=False)
    return xs.reshape(B, T, D)
```

```python
import functools

import jax
import jax.numpy as jnp
import numpy as np
from jax import lax
from jax.experimental import pallas as pl
from jax.experimental.pallas import tpu as pltpu

F32 = jnp.float32
BF16 = jnp.bfloat16

D_MODEL = 2048
A_HEADS = 16
A_GROUPS = 4
A_HPG = A_HEADS // A_GROUPS
A_DH = 128
A_KVW = A_GROUPS * A_DH
CMP_BLOCK = 32
CMP_STRIDE = 16
CMP_HIDDEN = 256
SEL_BLOCK = 64
SEL_SHIFT = 6
SEL_TOP_N = 16
WIN_SIZE = 512
B_HEADS = 4
B_DK = 256
B_DV = 512
B_RANK = 16
B_GATE_TEMP = 16.0
B_CHUNK = 64
NORM_EPS = 1e-6

LANES = 128
NB_PAD = LANES
MASK_BIG = 2.0 ** 100
LOG2E = float(np.log2(np.e))

SRC_Q_A = 0
SRC_KV_CMP = 2048
SRC_KV_SW = 3072
SRC_G_BR = 5120
SRC_R2 = 5168
SRC_A_LR = 11312
SRC_R3 = 11328
SRC_END = 17472
G_BR_COLS = 48
KVSW_K_SEL, KVSW_V_SEL, KVSW_K_WIN, KVSW_V_WIN = 0, 512, 1024, 1536
R2_Z_A, R2_Q_B, R2_K_B, R2_V_B = 0, 2048, 3072, 4096
R3_Z_B, R3_GATE_A, R3_GATE_B = 0, 2048, 4096

VMEM_LIMIT = 56 * 1024 * 1024


def _cparams(sem):
    return pltpu.CompilerParams(dimension_semantics=sem, vmem_limit_bytes=VMEM_LIMIT)


def _dot(a, b):
    return jnp.dot(a, b, preferred_element_type=F32)


def _dot_nt(a, b):
    return lax.dot_general(a, b, (((1,), (1,)), ((), ())), preferred_element_type=F32)


def _dot_tn(a, b):
    return lax.dot_general(a, b, (((0,), (0,)), ((), ())), preferred_element_type=F32)


def _split_bf16(x):
    hi = x.astype(BF16)
    lo = (x - hi.astype(F32)).astype(BF16)
    return hi, lo


def _rmsnorm_kernel(x_ref, w_ref, o_ref):
    x = x_ref[...]
    ms = jnp.mean(x * x, axis=-1, keepdims=True)
    o_ref[...] = (x * lax.rsqrt(ms + NORM_EPS) * w_ref[...]).astype(o_ref.dtype)


def _rmsnorm(x, w, tm=512):
    T, D = x.shape
    tm = min(tm, T)
    return pl.pallas_call(
        _rmsnorm_kernel,
        grid=(T // tm,),
        in_specs=[pl.BlockSpec((tm, D), lambda i: (i, 0)), pl.BlockSpec((1, D), lambda i: (0, 0))],
        out_specs=pl.BlockSpec((tm, D), lambda i: (i, 0)),
        out_shape=jax.ShapeDtypeStruct((T, D), BF16),
        compiler_params=_cparams(("parallel",)),
        name="rmsnorm",
    )(x, w)


def _proj_kernel(h_ref, *refs, shift, valid, scale):
    if shift:
        wa_ref, wb_ref, o_ref, wc_ref = refs
    else:
        wa_ref, o_ref, wc_ref = refs
    tn = o_ref.shape[1]

    @pl.when(pl.program_id(1) == 0)
    def _():
        if shift:
            wc_ref[0:tn - shift, :] = wa_ref[shift:tn, :].astype(BF16)
            wc_ref[tn - shift:tn, :] = wb_ref[0:shift, :].astype(BF16)
        else:
            wc_ref[...] = wa_ref[...].astype(BF16)
        if valid < tn:
            wc_ref[valid:tn, :] = jnp.zeros((tn - valid, wc_ref.shape[1]), BF16)

    r = _dot_nt(h_ref[...], wc_ref[...])
    if scale != 1.0:
        r = r * scale
    o_ref[...] = r.astype(o_ref.dtype)


def _proj(h, w_t, layer, col0, width, tn, out_dtype, scale=1.0, valid=None, tm=2048):
    T, D = h.shape
    tm = min(tm, T)
    shift = col0 % LANES
    a0 = col0 - shift
    valid = tn if valid is None else valid
    assert a0 % tn == 0 and width % tn == 0 and (valid == tn or width == tn)
    assert shift % 16 == 0 and valid % 16 == 0
    in_specs = [
        pl.BlockSpec((tm, D), lambda j, i: (i, 0)),
        pl.BlockSpec((None, tn, D), lambda j, i: (layer, a0 // tn + j, 0)),
    ]
    args = [h, w_t]
    if shift:
        in_specs.append(pl.BlockSpec((None, LANES, D), lambda j, i: (layer, (a0 + (j + 1) * tn) // LANES, 0)))
        args.append(w_t)
    return pl.pallas_call(
        functools.partial(_proj_kernel, shift=shift, valid=valid, scale=scale),
        grid=(width // tn, T // tm),
        in_specs=in_specs,
        out_specs=pl.BlockSpec((tm, tn), lambda j, i: (i, j)),
        out_shape=jax.ShapeDtypeStruct((T, width), out_dtype),
        scratch_shapes=[pltpu.VMEM((tn, D), BF16)],
        compiler_params=_cparams(("arbitrary", "arbitrary")),
        name="proj",
    )(*args)


def _gelu_tanh(x):
    c = np.float32(np.sqrt(2.0 / np.pi))
    return 0.5 * x * (1.0 + jnp.tanh(c * (x + np.float32(0.044715) * (x * x * x))))


def _compress_kernel(x_ref, pe_ref, w1_ref, b1_ref, w2_ref, b2_ref, o_ref, xlo_ref, xhi_ref):
    nch = o_ref.shape[2]
    half = CMP_STRIDE * A_DH
    for p in range(CMP_STRIDE):
        tok = x_ref[pl.ds(p, nch, stride=CMP_STRIDE), :]
        cs = slice(p * A_DH, (p + 1) * A_DH)
        xlo_ref[:, cs] = (tok + pe_ref[0, p:p + 1, :]).astype(BF16)
        xhi_ref[:, cs] = (tok + pe_ref[0, CMP_STRIDE + p:CMP_STRIDE + p + 1, :]).astype(BF16)
    a = _dot(xlo_ref[...], w1_ref[0, 0:half, :])
    b = _dot(xhi_ref[...], w1_ref[0, half:2 * half, :])
    b_next = pltpu.roll(b, nch - 1, axis=0)
    row = lax.broadcasted_iota(jnp.int32, b.shape, 0)
    b_next = jnp.where(row < nch - 1, b_next, 0.0)
    hid = _gelu_tanh(a + b_next + b1_ref[0])
    out = _dot(hid.astype(BF16), w2_ref[0]) + b2_ref[0]
    o_ref[0, 0] = out.astype(o_ref.dtype)


def _compress(kv, pe, w1, b1, w2, b2):
    T = kv.shape[0]
    G = A_GROUPS
    nch = T // CMP_STRIDE
    width = CMP_STRIDE * A_DH
    return pl.pallas_call(
        _compress_kernel,
        grid=(2, G),
        in_specs=[
            pl.BlockSpec((T, A_DH), lambda s, g: (0, s * G + g)),
            pl.BlockSpec((1, CMP_BLOCK, A_DH), lambda s, g: (s, 0, 0)),
            pl.BlockSpec((1, 2 * width, CMP_HIDDEN), lambda s, g: (s, 0, 0)),
            pl.BlockSpec((1, 1, CMP_HIDDEN), lambda s, g: (s, 0, 0)),
            pl.BlockSpec((1, CMP_HIDDEN, A_DH), lambda s, g: (s, 0, 0)),
            pl.BlockSpec((1, 1, A_DH), lambda s, g: (s, 0, 0)),
        ],
        out_specs=pl.BlockSpec((1, 1, nch, A_DH), lambda s, g: (s, g, 0, 0)),
        out_shape=jax.ShapeDtypeStruct((2, G, nch, A_DH), BF16),
        scratch_shapes=[pltpu.VMEM((nch, width), BF16), pltpu.VMEM((nch, width), BF16)],
        compiler_params=_cparams(("parallel", "parallel")),
        name="compress",
    )(kv, pe, w1, b1, w2, b2)


def _stack_heads(q_ref, qs_ref):
    tq = q_ref.shape[0]
    for h in range(A_HPG):
        qs_ref[h * tq:(h + 1) * tq, :] = q_ref[:, h * A_DH:(h + 1) * A_DH]


def _masked_chunks(s, masks, tq):
    out = []
    for c, mk in enumerate(masks):
        sc = s[:, c * LANES:(c + 1) * LANES]
        out.append(jnp.concatenate(
            [jnp.where(mk, sc[h * tq:(h + 1) * tq], -jnp.inf) for h in range(A_HPG)], axis=0))
    return out


def _chunk_softmax(sc):
    mx = sc[0]
    for x in sc[1:]:
        mx = jnp.maximum(mx, x)
    m = jnp.max(mx, axis=-1, keepdims=True)
    m = jnp.where(m > -jnp.inf, m, 0.0)
    pc = [jnp.exp2(x - m) for x in sc]
    ps = pc[0]
    for x in pc[1:]:
        ps = ps + x
    return pc, jnp.sum(ps, axis=-1, keepdims=True)


def _cmp_attn_kernel(q_ref, kc_ref, vc_ref, o_ref, qsel_ref, qs_ref, imp_ref):
    i = pl.program_id(1)

    @pl.when(i == 0)
    def _():
        imp_ref[1] = jnp.zeros(imp_ref.shape[1:], F32)

    for slot in (0, 1):
        pl.when(i % 2 == slot)(functools.partial(
            _cmp_attn_step, q_ref, kc_ref, vc_ref, o_ref, qsel_ref, qs_ref, imp_ref, slot))


def _cmp_attn_step(q_ref, kc_ref, vc_ref, o_ref, qsel_ref, qs_ref, imp_ref, slot):
    tq = q_ref.shape[0]
    nch = kc_ref.shape[2]
    i = pl.program_id(1)
    t0 = jnp.minimum(i, pl.num_programs(1) - 2) * tq
    _stack_heads(q_ref, qs_ref)
    s = _dot_nt(qs_ref[...], kc_ref[0, 0])
    tpos = t0 + lax.broadcasted_iota(jnp.int32, (tq, LANES), 0)
    lane = lax.broadcasted_iota(jnp.int32, (tq, LANES), 1)
    masks = [(lane + c * LANES) * CMP_STRIDE + (CMP_BLOCK - 1) <= tpos for c in range(nch // LANES)]
    pc, denom = _chunk_softmax(_masked_chunks(s, masks, tq))
    inv = 1.0 / jnp.where(denom > 0, denom, 1.0)
    o = _dot(jnp.concatenate([x.astype(BF16) for x in pc], axis=1), vc_ref[0, 0]) * inv
    for h in range(A_HPG):
        o_ref[:, h * A_DH:(h + 1) * A_DH] = o[h * tq:(h + 1) * tq].astype(o_ref.dtype)
    pn = [x * inv for x in pc]
    psum = jnp.concatenate(
        [sum(x[h * tq:(h + 1) * tq] for h in range(A_HPG)) for x in pn], axis=1)

    jj = lax.broadcasted_iota(jnp.int32, (NB_PAD, nch), 0)
    cs = lax.broadcasted_iota(jnp.int32, (NB_PAD, nch), 1) * CMP_STRIDE
    ov = ((cs <= jj * SEL_BLOCK + SEL_BLOCK - 1) & (cs + CMP_BLOCK - 1 >= jj * SEL_BLOCK))
    ov = jnp.where(ov, 1.0, 0.0).astype(BF16)
    p_hi, p_lo = _split_bf16(psum)
    imp_ref[slot] = _dot_nt(ov, p_hi) + _dot_nt(ov, p_lo)

    pslc = imp_ref[1 - slot]
    jf = lax.broadcasted_iota(jnp.int32, (NB_PAD, tq), 0)
    tl = jnp.maximum(i - 1, 0) * tq + lax.broadcasted_iota(jnp.int32, (NB_PAD, tq), 1)
    cur = jnp.right_shift(tl, SEL_SHIFT)
    valid = jf <= cur
    forced = (jf == 0) | (jf == cur) | (jf == cur - 1)
    score = jnp.where(valid & ~forced, pslc, -jnp.inf)
    jff = jf.astype(F32)
    for _ in range(SEL_TOP_N - 3):
        mx = jnp.max(score, axis=0, keepdims=True)
        first = jnp.min(jnp.where(score == mx, jff, float(NB_PAD)), axis=0, keepdims=True)
        score = jnp.where(jff == first, -jnp.inf, score)
    neg = jnp.where(valid & (score == -jnp.inf), 0.0, -MASK_BIG)
    qsel_ref[...] = neg.T.astype(qsel_ref.dtype)


def _cmp_attn(qa, kvc, tq=256):
    T = qa.shape[0]
    G = A_GROUPS
    nch = kvc.shape[2]
    tq = min(tq, T)
    qw = A_HPG * A_DH
    n = T // tq
    return pl.pallas_call(
        _cmp_attn_kernel,
        grid=(G, n + 1),
        in_specs=[
            pl.BlockSpec((tq, qw), lambda g, i: (jnp.minimum(i, n - 1), g)),
            pl.BlockSpec((1, 1, nch, A_DH), lambda g, i: (0, g, 0, 0)),
            pl.BlockSpec((1, 1, nch, A_DH), lambda g, i: (1, g, 0, 0)),
        ],
        out_specs=[
            pl.BlockSpec((tq, qw), lambda g, i: (jnp.minimum(i, n - 1), g)),
            pl.BlockSpec((tq, NB_PAD), lambda g, i: (jnp.maximum(i - 1, 0), g)),
        ],
        out_shape=[
            jax.ShapeDtypeStruct((T, A_HEADS * A_DH), BF16),
            jax.ShapeDtypeStruct((T, G * NB_PAD), BF16),
        ],
        scratch_shapes=[pltpu.VMEM((A_HPG * tq, A_DH), BF16), pltpu.VMEM((2, NB_PAD, tq), F32)],
        compiler_params=_cparams(("parallel", "arbitrary")),
        name="cmp_attn",
    )(qa, kvc, kvc)


def _sel_attn_kernel(q_ref, qsel_ref, k_ref, v_ref, oh_ref, o_ref, qa_ref, m_ref, l_ref, acc_ref, *, tk):
    tq = q_ref.shape[0]
    nsub = k_ref.shape[0] // tk
    i = pl.program_id(1)
    kp = pl.program_id(2)
    last_kt = (i * tq + tq - 1) // tk
    nlc = tk // LANES

    @pl.when(kp == 0)
    def _():
        for h in range(A_HPG):
            qa_ref[h * tq:(h + 1) * tq, 0:A_DH] = q_ref[:, h * A_DH:(h + 1) * A_DH]
            qa_ref[h * tq:(h + 1) * tq, A_DH:2 * A_DH] = qsel_ref[...]
        m_ref[...] = jnp.full(m_ref.shape, -jnp.inf, F32)
        l_ref[...] = jnp.zeros(l_ref.shape, F32)
        acc_ref[...] = jnp.zeros(acc_ref.shape, F32)

    def step(diagonal, kt, off):
        rows = pl.ds(off, tk)
        ka = jnp.concatenate([k_ref[rows, :], oh_ref[rows, :]], axis=1)
        s = _dot_nt(qa_ref[...], ka)
        sc = [s[:, c * LANES:(c + 1) * LANES] for c in range(nlc)]
        if diagonal:
            tpos = i * tq + lax.broadcasted_iota(jnp.int32, (tq, LANES), 0)
            lane = kt * tk + lax.broadcasted_iota(jnp.int32, (tq, LANES), 1)
            for c in range(nlc):
                causal = lane + c * LANES <= tpos
                sc[c] = jnp.concatenate(
                    [jnp.where(causal, sc[c][h * tq:(h + 1) * tq], -MASK_BIG) for h in range(A_HPG)], axis=0)
        mx = sc[0]
        for c in range(1, nlc):
            mx = jnp.maximum(mx, sc[c])
        m_old = m_ref[...]
        m_new = jnp.maximum(m_old, jnp.max(mx, axis=-1, keepdims=True))
        alpha = jnp.exp2(m_old - m_new)
        pc = [jnp.exp2(sc[c] - m_new) for c in range(nlc)]
        ps = pc[0]
        for c in range(1, nlc):
            ps = ps + pc[c]
        l_ref[...] = alpha * l_ref[...] + ps
        p = jnp.concatenate([x.astype(BF16) for x in pc], axis=1)
        acc_ref[...] = alpha * acc_ref[...] + _dot(p, v_ref[rows, :])
        m_ref[...] = m_new

    def sub_tile(sub, carry):
        kt = kp * nsub + sub
        off = pl.multiple_of(sub * tk, tk)
        pl.when(kt < last_kt)(functools.partial(step, False, kt, off))
        pl.when(kt == last_kt)(functools.partial(step, True, kt, off))
        return carry

    all_interior = kp * nsub + nsub - 1 < last_kt

    @pl.when(all_interior)
    def _():
        for sub in range(nsub):
            step(False, kp * nsub + sub, sub * tk)

    @pl.when(jnp.logical_not(all_interior))
    def _():
        lax.fori_loop(0, nsub, sub_tile, 0)

    @pl.when(kp == pl.num_programs(2) - 1)
    def _():
        for h in range(A_HPG):
            l = jnp.sum(l_ref[h * tq:(h + 1) * tq], axis=-1, keepdims=True)
            o = acc_ref[h * tq:(h + 1) * tq] / l
            o_ref[:, h * A_DH:(h + 1) * A_DH] = o.astype(o_ref.dtype)


def _sel_attn(qa, kvsw, qsel, tq=512, tk=512, kb=2048):
    T = qa.shape[0]
    G = A_GROUPS
    tq = min(tq, T)
    tk = min(tk, T)
    kb = min(kb, T)
    qw = A_HPG * A_DH
    rows = A_HPG * tq

    def kv_map(col, per_group):
        def f(g, i, kp):
            return (jnp.minimum(kp, (i * tq + tq - 1) // kb), col // A_DH + g * per_group)
        return f

    onehot = jnp.asarray(np.arange(T)[:, None] // SEL_BLOCK == np.arange(NB_PAD)[None, :], dtype=BF16)

    return pl.pallas_call(
        functools.partial(_sel_attn_kernel, tk=tk),
        grid=(G, T // tq, T // kb),
        in_specs=[
            pl.BlockSpec((tq, qw), lambda g, i, kp: (i, g)),
            pl.BlockSpec((tq, NB_PAD), lambda g, i, kp: (i, g)),
            pl.BlockSpec((kb, A_DH), kv_map(KVSW_K_SEL, 1)),
            pl.BlockSpec((kb, A_DH), kv_map(KVSW_V_SEL, 1)),
            pl.BlockSpec((kb, NB_PAD), kv_map(0, 0)),
        ],
        out_specs=pl.BlockSpec((tq, qw), lambda g, i, kp: (i, g)),
        out_shape=jax.ShapeDtypeStruct((T, A_HEADS * A_DH), BF16),
        scratch_shapes=[
            pltpu.VMEM((rows, 2 * A_DH), BF16),
            pltpu.VMEM((rows, LANES), F32),
            pltpu.VMEM((rows, LANES), F32),
            pltpu.VMEM((rows, A_DH), F32),
        ],
        compiler_params=_cparams(("parallel", "parallel", "arbitrary")),
        name="sel_attn",
    )(qa, qsel, kvsw, kvsw, onehot)


def _win_attn_kernel(q_ref, k0_ref, k1_ref, k2_ref, v0_ref, v1_ref, v2_ref, o_ref, qs_ref, p_ref, inv_ref):
    tq = q_ref.shape[0]
    i = pl.program_id(1)
    ia = jnp.minimum(i, pl.num_programs(1) - 2)

    @pl.when(i == 0)
    def _():
        p_ref[1] = jnp.zeros(p_ref.shape[1:], BF16)
        inv_ref[1] = jnp.ones(inv_ref.shape[1:], F32)

    def step(slot):
        k = jnp.concatenate([k0_ref[...], k1_ref[...], k2_ref[...]], axis=0)
        _stack_heads(q_ref, qs_ref)
        s = _dot_nt(qs_ref[...], k)
        r = lax.broadcasted_iota(jnp.int32, (tq, LANES), 0)
        lane = lax.broadcasted_iota(jnp.int32, (tq, LANES), 1)
        masks = []
        for c in range(3 * tq // LANES):
            col = lane + c * LANES
            rel = 2 * tq + r - col
            masks.append((rel >= 0) & (rel < WIN_SIZE) & ((ia - 2) * tq + col >= 0))
        pc, denom = _chunk_softmax(_masked_chunks(s, masks, tq))

        v = jnp.concatenate([v0_ref[...], v1_ref[...], v2_ref[...]], axis=0)
        o = _dot(p_ref[1 - slot], v) * inv_ref[1 - slot]
        for h in range(A_HPG):
            o_ref[:, h * A_DH:(h + 1) * A_DH] = o[h * tq:(h + 1) * tq].astype(o_ref.dtype)

        p_ref[slot] = jnp.concatenate([x.astype(BF16) for x in pc], axis=1)
        inv_ref[slot] = jnp.broadcast_to(1.0 / denom, inv_ref.shape[1:])

    for slot in (0, 1):
        pl.when(i % 2 == slot)(functools.partial(step, slot))


def _win_attn(qa, kvsw, tq=256):
    T = qa.shape[0]
    G = A_GROUPS
    tq = min(tq, T)
    assert WIN_SIZE <= 2 * tq
    qw = A_HPG * A_DH
    n = T // tq

    def kv_spec(col, back, lag):
        def f(g, i):
            tile = jnp.clip(i - lag, 0, n - 1)
            return (jnp.maximum(tile - back, 0), col // A_DH + g)
        return pl.BlockSpec((tq, A_DH), f)

    return pl.pallas_call(
        _win_attn_kernel,
        grid=(G, n + 1),
        in_specs=[pl.BlockSpec((tq, qw), lambda g, i: (jnp.minimum(i, n - 1), g)),
                  kv_spec(KVSW_K_WIN, 2, 0), kv_spec(KVSW_K_WIN, 1, 0), kv_spec(KVSW_K_WIN, 0, 0),
                  kv_spec(KVSW_V_WIN, 2, 1), kv_spec(KVSW_V_WIN, 1, 1), kv_spec(KVSW_V_WIN, 0, 1)],
        out_specs=pl.BlockSpec((tq, qw), lambda g, i: (jnp.maximum(i - 1, 0), g)),
        out_shape=jax.ShapeDtypeStruct((T, A_HEADS * A_DH), BF16),
        scratch_shapes=[pltpu.VMEM((A_HPG * tq, A_DH), BF16),
                        pltpu.VMEM((2, A_HPG * tq, 3 * tq), BF16),
                        pltpu.VMEM((2, A_HPG * tq, LANES), F32)],
        compiler_params=_cparams(("parallel", "arbitrary")),
        name="win_attn",
    )(qa, kvsw, kvsw, kvsw, kvsw, kvsw, kvsw)


def _gla_kernel(q_ref, k_ref, v_ref, a_ref, aw_ref, ab_ref, nw_ref, o_ref, st_ref):
    tc = q_ref.shape[0]
    C = B_CHUNK

    @pl.when(pl.program_id(0) == 0)
    def _():
        st_ref[...] = jnp.zeros(st_ref.shape, F32)

    logits = _dot(a_ref[...], aw_ref[...]) + ab_ref[...]
    log_a = jax.nn.log_sigmoid(logits) * (1.0 / B_GATE_TEMP)
    ri = lax.broadcasted_iota(jnp.int32, (C, C), 0)
    ci = lax.broadcasted_iota(jnp.int32, (C, C), 1)
    tril = ci <= ri
    tril_b = jnp.where(tril, 1.0, 0.0).astype(BF16)
    nw = nw_ref[...]
    for c in range(tc // C):
        sl = slice(c * C, (c + 1) * C)
        la_hi, la_lo = _split_bf16(log_a[sl])
        cum = _dot(tril_b, la_hi) + _dot(tril_b, la_lo)
        last = cum[C - 1:C, :]
        qc = q_ref[sl, :].astype(F32) * (B_DK ** -0.5)
        kc = k_ref[sl, :].astype(F32)
        q_dec = (qc * jnp.exp(cum)).astype(BF16)
        k_inv = (kc * jnp.exp(-cum)).astype(BF16)
        k_state = (kc * jnp.exp(last - cum)).astype(BF16)
        dec = jnp.exp(last)
        for h in range(B_HEADS):
            ks = slice(h * B_DK, (h + 1) * B_DK)
            vs = slice(h * B_DV, (h + 1) * B_DV)
            vc = v_ref[sl, vs]
            att = jnp.where(tril, _dot_nt(q_dec[:, ks], k_inv[:, ks]), 0.0)
            st = st_ref[h]
            o = _dot(att.astype(BF16), vc) + _dot_nt(q_dec[:, ks], st.astype(BF16))
            st_ref[h] = st * dec[:, ks] + _dot_tn(vc, k_state[:, ks])
            o = o * lax.rsqrt(jnp.mean(o * o, axis=-1, keepdims=True) + NORM_EPS) * nw
            o_ref[sl, vs] = o.astype(o_ref.dtype)


def _gla(cols2, alr, aw, ab, nw, tc=512):
    T = cols2.shape[0]
    tc = min(tc, T)
    kw = B_HEADS * B_DK
    vw = B_HEADS * B_DV
    return pl.pallas_call(
        _gla_kernel,
        grid=(T // tc,),
        in_specs=[
            pl.BlockSpec((tc, kw), lambda i: (i, R2_Q_B // kw)),
            pl.BlockSpec((tc, kw), lambda i: (i, R2_K_B // kw)),
            pl.BlockSpec((tc, vw), lambda i: (i, R2_V_B // vw)),
            pl.BlockSpec((tc, LANES), lambda i: (i, 0)),
            pl.BlockSpec((LANES, kw), lambda i: (0, 0)),
            pl.BlockSpec((1, kw), lambda i: (0, 0)),
            pl.BlockSpec((1, B_DV), lambda i: (0, 0)),
        ],
        out_specs=pl.BlockSpec((tc, vw), lambda i: (i, 0)),
        out_shape=jax.ShapeDtypeStruct((T, vw), BF16),
        scratch_shapes=[pltpu.VMEM((B_HEADS, B_DV, B_DK), F32)],
        compiler_params=_cparams(("arbitrary",)),
        name="gla",
    )(cols2, cols2, cols2, alr, aw, ab, nw)


def _silu(x):
    return x * jax.nn.sigmoid(x)


def _merge_a_kernel(ocmp_ref, osel_ref, owin_ref, gbr_ref, za_ref, ga_ref, pa_ref, y_ref, ua_ref):
    i = pl.program_id(0)

    @pl.when(i == 0)
    def _():
        ua_ref[1] = jnp.zeros(ua_ref.shape[1:], BF16)

    def step(slot):
        sg = jax.nn.sigmoid(gbr_ref[...].astype(F32))
        for hh in range(A_HEADS):
            cs = slice(hh * A_DH, (hh + 1) * A_DH)
            oa = (sg[:, 3 * hh:3 * hh + 1] * ocmp_ref[:, cs].astype(F32)
                  + sg[:, 3 * hh + 1:3 * hh + 2] * osel_ref[:, cs].astype(F32)
                  + sg[:, 3 * hh + 2:3 * hh + 3] * owin_ref[:, cs].astype(F32))
            ua_ref[slot, :, cs] = (oa * _silu(za_ref[:, cs].astype(F32))).astype(BF16)
        ya = _dot(ua_ref[1 - slot], pa_ref[...])
        y_ref[...] = (jax.nn.sigmoid(ga_ref[...].astype(F32)) * ya).astype(y_ref.dtype)

    for slot in (0, 1):
        pl.when(i % 2 == slot)(functools.partial(step, slot))


def _merge_b_kernel(ya_ref, ob_ref, zb_ref, gb_ref, pb_ref, y_ref):
    ub = (ob_ref[...].astype(F32) * _silu(zb_ref[...].astype(F32))).astype(BF16)
    yb = _dot(ub, pb_ref[...])
    y = ya_ref[...].astype(F32) + jax.nn.sigmoid(gb_ref[...].astype(F32)) * yb
    y_ref[...] = y.astype(y_ref.dtype)


def _merge(o_cmp, o_sel, o_win, gbr, cols2, o_b, cols3, pa, pb, tm=512):
    T = gbr.shape[0]
    D = D_MODEL
    tm = min(tm, T)
    row = lambda c: pl.BlockSpec((tm, D), lambda i: (i, c))
    const = lambda: pl.BlockSpec((D, D), lambda i: (0, 0), pipeline_mode=pl.Buffered(1))
    n = T // tm
    cur = lambda c: pl.BlockSpec((tm, D), lambda i: (jnp.minimum(i, n - 1), c))
    prev = lambda c: pl.BlockSpec((tm, D), lambda i: (jnp.maximum(i - 1, 0), c))
    ya = pl.pallas_call(
        _merge_a_kernel,
        grid=(n + 1,),
        in_specs=[cur(0), cur(0), cur(0),
                  pl.BlockSpec((tm, LANES), lambda i: (jnp.minimum(i, n - 1), 0)),
                  cur(R2_Z_A // D), prev(R3_GATE_A // D), const()],
        out_specs=prev(0),
        out_shape=jax.ShapeDtypeStruct((T, D), BF16),
        scratch_shapes=[pltpu.VMEM((2, tm, D), BF16)],
        compiler_params=_cparams(("arbitrary",)),
        name="merge_a",
    )(o_cmp, o_sel, o_win, gbr, cols2, cols3, pa)
    return pl.pallas_call(
        _merge_b_kernel,
        grid=(T // tm,),
        in_specs=[row(0), row(0), row(R3_Z_B // D), row(R3_GATE_B // D), const()],
        out_specs=row(0),
        out_shape=jax.ShapeDtypeStruct((T, D), BF16),
        compiler_params=_cparams(("parallel",)),
        name="merge_b",
    )(ya, o_b, cols3, cols3, pb)


def _outproj_kernel(x_ref, y_ref, w_ref, nw_ref, *out_refs, last):
    xn = x_ref[...] + _dot(y_ref[...], w_ref[...])
    ms = jnp.mean(xn * xn, axis=-1, keepdims=True)
    hn = xn * lax.rsqrt(ms + NORM_EPS) * nw_ref[...]
    if last:
        out_refs[0][...] = hn
    else:
        out_refs[0][...] = xn
        out_refs[1][...] = hn.astype(BF16)


def _outproj(x, y, w, nw, last, tm=512):
    T, D = x.shape
    tm = min(tm, T)
    row = pl.BlockSpec((tm, D), lambda i: (i, 0))
    if last:
        out_specs, out_shape = row, jax.ShapeDtypeStruct((T, D), F32)
    else:
        out_specs = [row, row]
        out_shape = [jax.ShapeDtypeStruct((T, D), F32), jax.ShapeDtypeStruct((T, D), BF16)]
    return pl.pallas_call(
        functools.partial(_outproj_kernel, last=last),
        grid=(T // tm,),
        in_specs=[
            row, row,
            pl.BlockSpec((D, D), lambda i: (0, 0), pipeline_mode=pl.Buffered(1)),
            pl.BlockSpec((1, D), lambda i: (0, 0)),
        ],
        out_specs=out_specs,
        out_shape=out_shape,
        compiler_params=_cparams(("parallel",)),
        name="outproj",
    )(x, y, w, nw)


def kernel(x, norm_w, w_in, cmp_pe, cmp_w1, cmp_b1, cmp_w2, cmp_b2, gla_alpha_w, gla_alpha_b, gla_norm_w,
           p_a, p_b, w_out, final_norm_w):
    B, T, D = x.shape
    L = norm_w.shape[0]
    assert B == 1 and D == D_MODEL and T % 512 == 0 and T // SEL_BLOCK <= NB_PAD
    assert w_in.shape[2] == SRC_END
    w_t = jnp.swapaxes(w_in, 1, 2)

    w1_b = cmp_w1.astype(BF16)
    w2_b = cmp_w2.astype(BF16)
    b1_r = cmp_b1.reshape(L, 2, 1, CMP_HIDDEN)
    b2_r = cmp_b2.reshape(L, 2, 1, A_DH)
    aw_p = jnp.pad(gla_alpha_w, ((0, 0), (0, LANES - B_RANK), (0, 0))).astype(BF16)
    ab_r = gla_alpha_b.reshape(L, 1, B_HEADS * B_DK)
    gnw_r = gla_norm_w.reshape(L, 1, B_DV)
    pa_b = p_a.astype(BF16)
    pb_b = p_b.astype(BF16)
    wo_b = w_out.astype(BF16)
    nw_next = jnp.concatenate([norm_w[1:], final_norm_w[None, :]], axis=0).reshape(L, 1, D)

    xs = x.reshape(T, D)
    h = _rmsnorm(xs, norm_w[0].reshape(1, D))
    for l in range(L):
        qa = _proj(h, w_t, l, SRC_Q_A, A_HEADS * A_DH, 512, BF16, scale=A_DH ** -0.5 * LOG2E)
        kv_cmp = _proj(h, w_t, l, SRC_KV_CMP, 2 * A_KVW, 512, F32)
        kvsw = _proj(h, w_t, l, SRC_KV_SW, 4 * A_KVW, 512, BF16)
        gbr = _proj(h, w_t, l, SRC_G_BR, LANES, LANES, BF16, valid=G_BR_COLS)
        cols2 = _proj(h, w_t, l, SRC_R2, SRC_A_LR - SRC_R2, 512, BF16)
        alr = _proj(h, w_t, l, SRC_A_LR, LANES, LANES, BF16, valid=B_RANK)
        cols3 = _proj(h, w_t, l, SRC_R3, SRC_END - SRC_R3, 512, BF16)
        kvc = _compress(kv_cmp, cmp_pe[l], w1_b[l], b1_r[l], w2_b[l], b2_r[l])
        o_cmp, qsel = _cmp_attn(qa, kvc)
        o_sel = _sel_attn(qa, kvsw, qsel)
        o_win = _win_attn(qa, kvsw)
        o_b = _gla(cols2, alr, aw_p[l], ab_r[l], gnw_r[l])
        y = _merge(o_cmp, o_sel, o_win, gbr, cols2, o_b, cols3, pa_b[l], pb_b[l])
        if l == L - 1:
            xs = _outproj(xs, y, wo_b[l], nw_next[l], last=True)
        else:
            xs, h = _outproj(xs, y, wo_b[l], nw_next[l], last=False)
    return xs.reshape(B, T, D)
```

```python
import functools

import jax
import jax.numpy as jnp
import numpy as np
from jax import lax
from jax.experimental import pallas as pl
from jax.experimental.pallas import tpu as pltpu

F32 = jnp.float32
BF16 = jnp.bfloat16

D_MODEL = 2048
A_HEADS = 16
A_GROUPS = 4
A_HPG = A_HEADS // A_GROUPS
A_DH = 128
A_KVW = A_GROUPS * A_DH
CMP_BLOCK = 32
CMP_STRIDE = 16
CMP_HIDDEN = 256
SEL_BLOCK = 64
SEL_SHIFT = 6
SEL_TOP_N = 16
WIN_SIZE = 512
B_HEADS = 4
B_DK = 256
B_DV = 512
B_RANK = 16
B_GATE_TEMP = 16.0
B_CHUNK = 64
NORM_EPS = 1e-6

LANES = 128
NB_PAD = LANES
MASK_BIG = 2.0 ** 100
LOG2E = float(np.log2(np.e))

SRC_Q_A = 0
SRC_KV_CMP = 2048
SRC_KV_SW = 3072
SRC_G_BR = 5120
SRC_R2 = 5168
SRC_A_LR = 11312
SRC_R3 = 11328
SRC_END = 17472
G_BR_COLS = 48
KVSW_K_SEL, KVSW_V_SEL, KVSW_K_WIN, KVSW_V_WIN = 0, 512, 1024, 1536
R2_Z_A, R2_Q_B, R2_K_B, R2_V_B = 0, 2048, 3072, 4096
R3_Z_B, R3_GATE_A, R3_GATE_B = 0, 2048, 4096

VMEM_LIMIT = 56 * 1024 * 1024


def _cparams(sem):
    return pltpu.CompilerParams(dimension_semantics=sem, vmem_limit_bytes=VMEM_LIMIT)


def _dot(a, b):
    return jnp.dot(a, b, preferred_element_type=F32)


def _dot_nt(a, b):
    return lax.dot_general(a, b, (((1,), (1,)), ((), ())), preferred_element_type=F32)


def _dot_tn(a, b):
    return lax.dot_general(a, b, (((0,), (0,)), ((), ())), preferred_element_type=F32)


def _split_bf16(x):
    hi = x.astype(BF16)
    lo = (x - hi.astype(F32)).astype(BF16)
    return hi, lo


def _rmsnorm_kernel(x_ref, w_ref, o_ref):
    x = x_ref[...]
    ms = jnp.mean(x * x, axis=-1, keepdims=True)
    o_ref[...] = (x * lax.rsqrt(ms + NORM_EPS) * w_ref[...]).astype(o_ref.dtype)


def _rmsnorm(x, w, tm=512):
    T, D = x.shape
    tm = min(tm, T)
    return pl.pallas_call(
        _rmsnorm_kernel,
        grid=(T // tm,),
        in_specs=[pl.BlockSpec((tm, D), lambda i: (i, 0)), pl.BlockSpec((1, D), lambda i: (0, 0))],
        out_specs=pl.BlockSpec((tm, D), lambda i: (i, 0)),
        out_shape=jax.ShapeDtypeStruct((T, D), BF16),
        compiler_params=_cparams(("parallel",)),
        name="rmsnorm",
    )(x, w)


def _proj_kernel(h_ref, *refs, shift, valid, scale):
    if shift:
        wa_ref, wb_ref, o_ref, wc_ref = refs
    else:
        wa_ref, o_ref, wc_ref = refs
    tn = o_ref.shape[1]

    @pl.when(pl.program_id(1) == 0)
    def _():
        if shift:
            wc_ref[0:tn - shift, :] = wa_ref[shift:tn, :].astype(BF16)
            wc_ref[tn - shift:tn, :] = wb_ref[0:shift, :].astype(BF16)
        else:
            wc_ref[...] = wa_ref[...].astype(BF16)
        if valid < tn:
            wc_ref[valid:tn, :] = jnp.zeros((tn - valid, wc_ref.shape[1]), BF16)

    r = _dot_nt(h_ref[...], wc_ref[...])
    if scale != 1.0:
        r = r * scale
    o_ref[...] = r.astype(o_ref.dtype)


def _proj(h, w_t, layer, col0, width, tn, out_dtype, scale=1.0, valid=None, tm=2048):
    T, D = h.shape
    tm = min(tm, T)
    shift = col0 % LANES
    a0 = col0 - shift
    valid = tn if valid is None else valid
    assert a0 % tn == 0 and width % tn == 0 and (valid == tn or width == tn)
    assert shift % 16 == 0 and valid % 16 == 0
    in_specs = [
        pl.BlockSpec((tm, D), lambda j, i: (i, 0)),
        pl.BlockSpec((None, tn, D), lambda j, i: (layer, a0 // tn + j, 0)),
    ]
    args = [h, w_t]
    if shift:
        in_specs.append(pl.BlockSpec((None, LANES, D), lambda j, i: (layer, (a0 + (j + 1) * tn) // LANES, 0)))
        args.append(w_t)
    return pl.pallas_call(
        functools.partial(_proj_kernel, shift=shift, valid=valid, scale=scale),
        grid=(width // tn, T // tm),
        in_specs=in_specs,
        out_specs=pl.BlockSpec((tm, tn), lambda j, i: (i, j)),
        out_shape=jax.ShapeDtypeStruct((T, width), out_dtype),
        scratch_shapes=[pltpu.VMEM((tn, D), BF16)],
        compiler_params=_cparams(("arbitrary", "arbitrary")),
        name="proj",
    )(*args)


def _gelu_tanh(x):
    c = np.float32(np.sqrt(2.0 / np.pi))
    return 0.5 * x * (1.0 + jnp.tanh(c * (x + np.float32(0.044715) * (x * x * x))))


def _compress_kernel(x_ref, pe_ref, w1_ref, b1_ref, w2_ref, b2_ref, o_ref, xlo_ref, xhi_ref):
    nch = o_ref.shape[2]
    half = CMP_STRIDE * A_DH
    for p in range(CMP_STRIDE):
        tok = x_ref[pl.ds(p, nch, stride=CMP_STRIDE), :]
        cs = slice(p * A_DH, (p + 1) * A_DH)
        xlo_ref[:, cs] = (tok + pe_ref[0, p:p + 1, :]).astype(BF16)
        xhi_ref[:, cs] = (tok + pe_ref[0, CMP_STRIDE + p:CMP_STRIDE + p + 1, :]).astype(BF16)
    a = _dot(xlo_ref[...], w1_ref[0, 0:half, :])
    b = _dot(xhi_ref[...], w1_ref[0, half:2 * half, :])
    b_next = pltpu.roll(b, nch - 1, axis=0)
    row = lax.broadcasted_iota(jnp.int32, b.shape, 0)
    b_next = jnp.where(row < nch - 1, b_next, 0.0)
    hid = _gelu_tanh(a + b_next + b1_ref[0])
    out = _dot(hid.astype(BF16), w2_ref[0]) + b2_ref[0]
    o_ref[0, 0] = out.astype(o_ref.dtype)


def _compress(kv, pe, w1, b1, w2, b2):
    T = kv.shape[0]
    G = A_GROUPS
    nch = T // CMP_STRIDE
    width = CMP_STRIDE * A_DH
    return pl.pallas_call(
        _compress_kernel,
        grid=(2, G),
        in_specs=[
            pl.BlockSpec((T, A_DH), lambda s, g: (0, s * G + g)),
            pl.BlockSpec((1, CMP_BLOCK, A_DH), lambda s, g: (s, 0, 0)),
            pl.BlockSpec((1, 2 * width, CMP_HIDDEN), lambda s, g: (s, 0, 0)),
            pl.BlockSpec((1, 1, CMP_HIDDEN), lambda s, g: (s, 0, 0)),
            pl.BlockSpec((1, CMP_HIDDEN, A_DH), lambda s, g: (s, 0, 0)),
            pl.BlockSpec((1, 1, A_DH), lambda s, g: (s, 0, 0)),
        ],
        out_specs=pl.BlockSpec((1, 1, nch, A_DH), lambda s, g: (s, g, 0, 0)),
        out_shape=jax.ShapeDtypeStruct((2, G, nch, A_DH), BF16),
        scratch_shapes=[pltpu.VMEM((nch, width), BF16), pltpu.VMEM((nch, width), BF16)],
        compiler_params=_cparams(("parallel", "parallel")),
        name="compress",
    )(kv, pe, w1, b1, w2, b2)


def _stack_heads(q_ref, qs_ref):
    tq = q_ref.shape[0]
    for h in range(A_HPG):
        qs_ref[h * tq:(h + 1) * tq, :] = q_ref[:, h * A_DH:(h + 1) * A_DH]


def _masked_chunks(s, masks, tq):
    out = []
    for c, mk in enumerate(masks):
        sc = s[:, c * LANES:(c + 1) * LANES]
        out.append(jnp.concatenate(
            [jnp.where(mk, sc[h * tq:(h + 1) * tq], -jnp.inf) for h in range(A_HPG)], axis=0))
    return out


def _chunk_softmax(sc):
    mx = sc[0]
    for x in sc[1:]:
        mx = jnp.maximum(mx, x)
    m = jnp.max(mx, axis=-1, keepdims=True)
    m = jnp.where(m > -jnp.inf, m, 0.0)
    pc = [jnp.exp2(x - m) for x in sc]
    ps = pc[0]
    for x in pc[1:]:
        ps = ps + x
    return pc, jnp.sum(ps, axis=-1, keepdims=True)


def _cmpwin_kernel(q_ref, kc_ref, vc_ref, k0_ref, k1_ref, k2_ref, v0_ref, v1_ref, v2_ref,
                   ocmp_ref, qsel_ref, owin_ref, qs_ref, imp_ref, p_ref, inv_ref):
    i = pl.program_id(1)

    @pl.when(i == 0)
    def _():
        imp_ref[1] = jnp.zeros(imp_ref.shape[1:], F32)
        p_ref[1] = jnp.zeros(p_ref.shape[1:], BF16)
        inv_ref[1] = jnp.ones(inv_ref.shape[1:], F32)

    def step(slot):
        _stack_heads(q_ref, qs_ref)
        _cmp_attn_step(kc_ref, vc_ref, ocmp_ref, qsel_ref, qs_ref, imp_ref, slot)
        _win_attn_step(k0_ref, k1_ref, k2_ref, v0_ref, v1_ref, v2_ref, owin_ref, qs_ref, p_ref, inv_ref, slot)

    for slot in (0, 1):
        pl.when(i % 2 == slot)(functools.partial(step, slot))


def _cmp_attn_step(kc_ref, vc_ref, o_ref, qsel_ref, qs_ref, imp_ref, slot):
    tq = qsel_ref.shape[0]
    nch = kc_ref.shape[2]
    i = pl.program_id(1)
    t0 = jnp.minimum(i, pl.num_programs(1) - 2) * tq
    s = _dot_nt(qs_ref[...], kc_ref[0, 0])
    tpos = t0 + lax.broadcasted_iota(jnp.int32, (tq, LANES), 0)
    lane = lax.broadcasted_iota(jnp.int32, (tq, LANES), 1)
    masks = [(lane + c * LANES) * CMP_STRIDE + (CMP_BLOCK - 1) <= tpos for c in range(nch // LANES)]
    pc, denom = _chunk_softmax(_masked_chunks(s, masks, tq))
    inv = 1.0 / jnp.where(denom > 0, denom, 1.0)
    o = _dot(jnp.concatenate([x.astype(BF16) for x in pc], axis=1), vc_ref[0, 0]) * inv
    for h in range(A_HPG):
        o_ref[:, h * A_DH:(h + 1) * A_DH] = o[h * tq:(h + 1) * tq].astype(o_ref.dtype)
    pn = [x * inv for x in pc]
    psum = jnp.concatenate(
        [sum(x[h * tq:(h + 1) * tq] for h in range(A_HPG)) for x in pn], axis=1)

    jj = lax.broadcasted_iota(jnp.int32, (NB_PAD, nch), 0)
    cs = lax.broadcasted_iota(jnp.int32, (NB_PAD, nch), 1) * CMP_STRIDE
    ov = ((cs <= jj * SEL_BLOCK + SEL_BLOCK - 1) & (cs + CMP_BLOCK - 1 >= jj * SEL_BLOCK))
    ov = jnp.where(ov, 1.0, 0.0).astype(BF16)
    p_hi, p_lo = _split_bf16(psum)
    imp_ref[slot] = _dot_nt(ov, p_hi) + _dot_nt(ov, p_lo)

    pslc = imp_ref[1 - slot]
    jf = lax.broadcasted_iota(jnp.int32, (NB_PAD, tq), 0)
    tl = jnp.maximum(i - 1, 0) * tq + lax.broadcasted_iota(jnp.int32, (NB_PAD, tq), 1)
    cur = jnp.right_shift(tl, SEL_SHIFT)
    valid = jf <= cur
    forced = (jf == 0) | (jf == cur) | (jf == cur - 1)
    score = jnp.where(valid & ~forced, pslc, -jnp.inf)
    jff = jf.astype(F32)
    for _ in range(SEL_TOP_N - 3):
        mx = jnp.max(score, axis=0, keepdims=True)
        first = jnp.min(jnp.where(score == mx, jff, float(NB_PAD)), axis=0, keepdims=True)
        score = jnp.where(jff == first, -jnp.inf, score)
    neg = jnp.where(valid & (score == -jnp.inf), 0.0, -MASK_BIG)
    qsel_ref[...] = neg.T.astype(qsel_ref.dtype)


def _win_attn_step(k0_ref, k1_ref, k2_ref, v0_ref, v1_ref, v2_ref, o_ref, qs_ref, p_ref, inv_ref, slot):
    tq = o_ref.shape[0]
    i = pl.program_id(1)
    ia = jnp.minimum(i, pl.num_programs(1) - 2)
    k = jnp.concatenate([k0_ref[...], k1_ref[...], k2_ref[...]], axis=0)
    s = _dot_nt(qs_ref[...], k)
    r = lax.broadcasted_iota(jnp.int32, (tq, LANES), 0)
    lane = lax.broadcasted_iota(jnp.int32, (tq, LANES), 1)
    masks = []
    for c in range(3 * tq // LANES):
        col = lane + c * LANES
        rel = 2 * tq + r - col
        masks.append((rel >= 0) & (rel < WIN_SIZE) & ((ia - 2) * tq + col >= 0))
    pc, denom = _chunk_softmax(_masked_chunks(s, masks, tq))

    v = jnp.concatenate([v0_ref[...], v1_ref[...], v2_ref[...]], axis=0)
    o = _dot(p_ref[1 - slot], v) * inv_ref[1 - slot]
    for h in range(A_HPG):
        o_ref[:, h * A_DH:(h + 1) * A_DH] = o[h * tq:(h + 1) * tq].astype(o_ref.dtype)

    p_ref[slot] = jnp.concatenate([x.astype(BF16) for x in pc], axis=1)
    inv_ref[slot] = jnp.broadcast_to(1.0 / denom, inv_ref.shape[1:])


def _cmpwin_attn(qa, kvc, kvsw, tq=256):
    T = qa.shape[0]
    G = A_GROUPS
    nch = kvc.shape[2]
    tq = min(tq, T)
    assert WIN_SIZE <= 2 * tq
    qw = A_HPG * A_DH
    n = T // tq
    cur = lambda g, i: (jnp.minimum(i, n - 1), g)
    prev = lambda g, i: (jnp.maximum(i - 1, 0), g)

    def kv_spec(col, back, lag):
        def f(g, i):
            tile = jnp.clip(i - lag, 0, n - 1)
            return (jnp.maximum(tile - back, 0), col // A_DH + g)
        return pl.BlockSpec((tq, A_DH), f)

    return pl.pallas_call(
        _cmpwin_kernel,
        grid=(G, n + 1),
        in_specs=[
            pl.BlockSpec((tq, qw), cur),
            pl.BlockSpec((1, 1, nch, A_DH), lambda g, i: (0, g, 0, 0)),
            pl.BlockSpec((1, 1, nch, A_DH), lambda g, i: (1, g, 0, 0)),
            kv_spec(KVSW_K_WIN, 2, 0), kv_spec(KVSW_K_WIN, 1, 0), kv_spec(KVSW_K_WIN, 0, 0),
            kv_spec(KVSW_V_WIN, 2, 1), kv_spec(KVSW_V_WIN, 1, 1), kv_spec(KVSW_V_WIN, 0, 1),
        ],
        out_specs=[
            pl.BlockSpec((tq, qw), cur),
            pl.BlockSpec((tq, NB_PAD), prev),
            pl.BlockSpec((tq, qw), prev),
        ],
        out_shape=[
            jax.ShapeDtypeStruct((T, A_HEADS * A_DH), BF16),
            jax.ShapeDtypeStruct((T, G * NB_PAD), BF16),
            jax.ShapeDtypeStruct((T, A_HEADS * A_DH), BF16),
        ],
        scratch_shapes=[pltpu.VMEM((A_HPG * tq, A_DH), BF16),
                        pltpu.VMEM((2, NB_PAD, tq), F32),
                        pltpu.VMEM((2, A_HPG * tq, 3 * tq), BF16),
                        pltpu.VMEM((2, A_HPG * tq, LANES), F32)],
        compiler_params=_cparams(("parallel", "arbitrary")),
        name="cmpwin_attn",
    )(qa, kvc, kvc, kvsw, kvsw, kvsw, kvsw, kvsw, kvsw)


def _sel_attn_kernel(q_ref, qsel_ref, k_ref, v_ref, oh_ref, o_ref, qa_ref, m_ref, l_ref, acc_ref, *, tk):
    tq = q_ref.shape[0]
    nsub = k_ref.shape[0] // tk
    i = pl.program_id(1)
    kp = pl.program_id(2)
    last_kt = (i * tq + tq - 1) // tk
    nlc = tk // LANES

    @pl.when(kp == 0)
    def _():
        for h in range(A_HPG):
            qa_ref[h * tq:(h + 1) * tq, 0:A_DH] = q_ref[:, h * A_DH:(h + 1) * A_DH]
            qa_ref[h * tq:(h + 1) * tq, A_DH:2 * A_DH] = qsel_ref[...]
        m_ref[...] = jnp.full(m_ref.shape, -jnp.inf, F32)
        l_ref[...] = jnp.zeros(l_ref.shape, F32)
        acc_ref[...] = jnp.zeros(acc_ref.shape, F32)

    def step(diagonal, kt, off):
        rows = pl.ds(off, tk)
        ka = jnp.concatenate([k_ref[rows, :], oh_ref[rows, :]], axis=1)
        s = _dot_nt(qa_ref[...], ka)
        sc = [s[:, c * LANES:(c + 1) * LANES] for c in range(nlc)]
        if diagonal:
            tpos = i * tq + lax.broadcasted_iota(jnp.int32, (tq, LANES), 0)
            lane = kt * tk + lax.broadcasted_iota(jnp.int32, (tq, LANES), 1)
            for c in range(nlc):
                causal = lane + c * LANES <= tpos
                sc[c] = jnp.concatenate(
                    [jnp.where(causal, sc[c][h * tq:(h + 1) * tq], -MASK_BIG) for h in range(A_HPG)], axis=0)
        mx = sc[0]
        for c in range(1, nlc):
            mx = jnp.maximum(mx, sc[c])
        m_old = m_ref[...]
        m_new = jnp.maximum(m_old, jnp.max(mx, axis=-1, keepdims=True))
        alpha = jnp.exp2(m_old - m_new)
        pc = [jnp.exp2(sc[c] - m_new) for c in range(nlc)]
        ps = pc[0]
        for c in range(1, nlc):
            ps = ps + pc[c]
        l_ref[...] = alpha * l_ref[...] + ps
        p = jnp.concatenate([x.astype(BF16) for x in pc], axis=1)
        acc_ref[...] = alpha * acc_ref[...] + _dot(p, v_ref[rows, :])
        m_ref[...] = m_new

    def sub_tile(sub, carry):
        kt = kp * nsub + sub
        off = pl.multiple_of(sub * tk, tk)
        pl.when(kt < last_kt)(functools.partial(step, False, kt, off))
        pl.when(kt == last_kt)(functools.partial(step, True, kt, off))
        return carry

    all_interior = kp * nsub + nsub - 1 < last_kt

    @pl.when(all_interior)
    def _():
        for sub in range(nsub):
            step(False, kp * nsub + sub, sub * tk)

    @pl.when(jnp.logical_not(all_interior))
    def _():
        lax.fori_loop(0, nsub, sub_tile, 0)

    @pl.when(kp == pl.num_programs(2) - 1)
    def _():
        for h in range(A_HPG):
            l = jnp.sum(l_ref[h * tq:(h + 1) * tq], axis=-1, keepdims=True)
            o = acc_ref[h * tq:(h + 1) * tq] / l
            o_ref[:, h * A_DH:(h + 1) * A_DH] = o.astype(o_ref.dtype)


def _sel_attn(qa, kvsw, qsel, tq=512, tk=512, kb=2048):
    T = qa.shape[0]
    G = A_GROUPS
    tq = min(tq, T)
    tk = min(tk, T)
    kb = min(kb, T)
    qw = A_HPG * A_DH
    rows = A_HPG * tq

    def kv_map(col, per_group):
        def f(g, i, kp):
            return (jnp.minimum(kp, (i * tq + tq - 1) // kb), col // A_DH + g * per_group)
        return f

    onehot = jnp.asarray(np.arange(T)[:, None] // SEL_BLOCK == np.arange(NB_PAD)[None, :], dtype=BF16)

    return pl.pallas_call(
        functools.partial(_sel_attn_kernel, tk=tk),
        grid=(G, T // tq, T // kb),
        in_specs=[
            pl.BlockSpec((tq, qw), lambda g, i, kp: (i, g)),
            pl.BlockSpec((tq, NB_PAD), lambda g, i, kp: (i, g)),
            pl.BlockSpec((kb, A_DH), kv_map(KVSW_K_SEL, 1)),
            pl.BlockSpec((kb, A_DH), kv_map(KVSW_V_SEL, 1)),
            pl.BlockSpec((kb, NB_PAD), kv_map(0, 0)),
        ],
        out_specs=pl.BlockSpec((tq, qw), lambda g, i, kp: (i, g)),
        out_shape=jax.ShapeDtypeStruct((T, A_HEADS * A_DH), BF16),
        scratch_shapes=[
            pltpu.VMEM((rows, 2 * A_DH), BF16),
            pltpu.VMEM((rows, LANES), F32),
            pltpu.VMEM((rows, LANES), F32),
            pltpu.VMEM((rows, A_DH), F32),
        ],
        compiler_params=_cparams(("parallel", "parallel", "arbitrary")),
        name="sel_attn",
    )(qa, qsel, kvsw, kvsw, onehot)


def _gla_kernel(q_ref, k_ref, v_ref, a_ref, aw_ref, ab_ref, nw_ref, o_ref, st_ref):
    tc = q_ref.shape[0]
    C = B_CHUNK

    @pl.when(pl.program_id(0) == 0)
    def _():
        st_ref[...] = jnp.zeros(st_ref.shape, F32)

    logits = _dot(a_ref[...], aw_ref[...]) + ab_ref[...]
    log_a = jax.nn.log_sigmoid(logits) * (1.0 / B_GATE_TEMP)
    ri = lax.broadcasted_iota(jnp.int32, (C, C), 0)
    ci = lax.broadcasted_iota(jnp.int32, (C, C), 1)
    tril = ci <= ri
    tril_b = jnp.where(tril, 1.0, 0.0).astype(BF16)
    nw = nw_ref[...]
    for c in range(tc // C):
        sl = slice(c * C, (c + 1) * C)
        la_hi, la_lo = _split_bf16(log_a[sl])
        cum = _dot(tril_b, la_hi) + _dot(tril_b, la_lo)
        last = cum[C - 1:C, :]
        qc = q_ref[sl, :].astype(F32) * (B_DK ** -0.5)
        kc = k_ref[sl, :].astype(F32)
        q_dec = (qc * jnp.exp(cum)).astype(BF16)
        k_inv = (kc * jnp.exp(-cum)).astype(BF16)
        k_state = (kc * jnp.exp(last - cum)).astype(BF16)
        dec = jnp.exp(last)
        for h in range(B_HEADS):
            ks = slice(h * B_DK, (h + 1) * B_DK)
            vs = slice(h * B_DV, (h + 1) * B_DV)
            vc = v_ref[sl, vs]
            att = jnp.where(tril, _dot_nt(q_dec[:, ks], k_inv[:, ks]), 0.0)
            st = st_ref[h]
            o = _dot(att.astype(BF16), vc) + _dot_nt(q_dec[:, ks], st.astype(BF16))
            st_ref[h] = st * dec[:, ks] + _dot_tn(vc, k_state[:, ks])
            o = o * lax.rsqrt(jnp.mean(o * o, axis=-1, keepdims=True) + NORM_EPS) * nw
            o_ref[sl, vs] = o.astype(o_ref.dtype)


def _gla(cols2, alr, aw, ab, nw, tc=512):
    T = cols2.shape[0]
    tc = min(tc, T)
    kw = B_HEADS * B_DK
    vw = B_HEADS * B_DV
    return pl.pallas_call(
        _gla_kernel,
        grid=(T // tc,),
        in_specs=[
            pl.BlockSpec((tc, kw), lambda i: (i, R2_Q_B // kw)),
            pl.BlockSpec((tc, kw), lambda i: (i, R2_K_B // kw)),
            pl.BlockSpec((tc, vw), lambda i: (i, R2_V_B // vw)),
            pl.BlockSpec((tc, LANES), lambda i: (i, 0)),
            pl.BlockSpec((LANES, kw), lambda i: (0, 0)),
            pl.BlockSpec((1, kw), lambda i: (0, 0)),
            pl.BlockSpec((1, B_DV), lambda i: (0, 0)),
        ],
        out_specs=pl.BlockSpec((tc, vw), lambda i: (i, 0)),
        out_shape=jax.ShapeDtypeStruct((T, vw), BF16),
        scratch_shapes=[pltpu.VMEM((B_HEADS, B_DV, B_DK), F32)],
        compiler_params=_cparams(("arbitrary",)),
        name="gla",
    )(cols2, cols2, cols2, alr, aw, ab, nw)


def _silu(x):
    return x * jax.nn.sigmoid(x)


def _merge_a_kernel(ocmp_ref, osel_ref, owin_ref, gbr_ref, za_ref, ga_ref, pa_ref, y_ref, ua_ref):
    i = pl.program_id(0)

    @pl.when(i == 0)
    def _():
        ua_ref[1] = jnp.zeros(ua_ref.shape[1:], BF16)

    def step(slot):
        sg = jax.nn.sigmoid(gbr_ref[...].astype(F32))
        for hh in range(A_HEADS):
            cs = slice(hh * A_DH, (hh + 1) * A_DH)
            oa = (sg[:, 3 * hh:3 * hh + 1] * ocmp_ref[:, cs].astype(F32)
                  + sg[:, 3 * hh + 1:3 * hh + 2] * osel_ref[:, cs].astype(F32)
                  + sg[:, 3 * hh + 2:3 * hh + 3] * owin_ref[:, cs].astype(F32))
            ua_ref[slot, :, cs] = (oa * _silu(za_ref[:, cs].astype(F32))).astype(BF16)
        ya = _dot(ua_ref[1 - slot], pa_ref[...])
        y_ref[...] = (jax.nn.sigmoid(ga_ref[...].astype(F32)) * ya).astype(y_ref.dtype)

    for slot in (0, 1):
        pl.when(i % 2 == slot)(functools.partial(step, slot))


def _merge_b_kernel(ya_ref, ob_ref, zb_ref, gb_ref, pb_ref, y_ref):
    ub = (ob_ref[...].astype(F32) * _silu(zb_ref[...].astype(F32))).astype(BF16)
    yb = _dot(ub, pb_ref[...])
    y = ya_ref[...].astype(F32) + jax.nn.sigmoid(gb_ref[...].astype(F32)) * yb
    y_ref[...] = y.astype(y_ref.dtype)


def _merge(o_cmp, o_sel, o_win, gbr, cols2, o_b, cols3, pa, pb, layer, tm=512):
    T = gbr.shape[0]
    D = D_MODEL
    tm = min(tm, T)
    row = lambda c: pl.BlockSpec((tm, D), lambda i: (i, c))
    const = lambda: pl.BlockSpec((None, D, D), lambda i: (layer, 0, 0), pipeline_mode=pl.Buffered(1))
    n = T // tm
    cur = lambda c: pl.BlockSpec((tm, D), lambda i: (jnp.minimum(i, n - 1), c))
    prev = lambda c: pl.BlockSpec((tm, D), lambda i: (jnp.maximum(i - 1, 0), c))
    ya = pl.pallas_call(
        _merge_a_kernel,
        grid=(n + 1,),
        in_specs=[cur(0), cur(0), cur(0),
                  pl.BlockSpec((tm, LANES), lambda i: (jnp.minimum(i, n - 1), 0)),
                  cur(R2_Z_A // D), prev(R3_GATE_A // D), const()],
        out_specs=prev(0),
        out_shape=jax.ShapeDtypeStruct((T, D), BF16),
        scratch_shapes=[pltpu.VMEM((2, tm, D), BF16)],
        compiler_params=_cparams(("arbitrary",)),
        name="merge_a",
    )(o_cmp, o_sel, o_win, gbr, cols2, cols3, pa)
    return pl.pallas_call(
        _merge_b_kernel,
        grid=(T // tm,),
        in_specs=[row(0), row(0), row(R3_Z_B // D), row(R3_GATE_B // D), const()],
        out_specs=row(0),
        out_shape=jax.ShapeDtypeStruct((T, D), BF16),
        compiler_params=_cparams(("parallel",)),
        name="merge_b",
    )(ya, o_b, cols3, cols3, pb)


def _outproj_kernel(x_ref, y_ref, w_ref, nw_ref, *out_refs, last):
    xn = x_ref[...] + _dot(y_ref[...], w_ref[...])
    ms = jnp.mean(xn * xn, axis=-1, keepdims=True)
    hn = xn * lax.rsqrt(ms + NORM_EPS) * nw_ref[...]
    if last:
        out_refs[0][...] = hn
    else:
        out_refs[0][...] = xn
        out_refs[1][...] = hn.astype(BF16)


def _outproj(x, y, w, layer, nw, last, tm=512):
    T, D = x.shape
    tm = min(tm, T)
    row = pl.BlockSpec((tm, D), lambda i: (i, 0))
    if last:
        out_specs, out_shape = row, jax.ShapeDtypeStruct((T, D), F32)
    else:
        out_specs = [row, row]
        out_shape = [jax.ShapeDtypeStruct((T, D), F32), jax.ShapeDtypeStruct((T, D), BF16)]
    return pl.pallas_call(
        functools.partial(_outproj_kernel, last=last),
        grid=(T // tm,),
        in_specs=[
            row, row,
            pl.BlockSpec((None, D, D), lambda i: (layer, 0, 0), pipeline_mode=pl.Buffered(1)),
            pl.BlockSpec((1, D), lambda i: (0, 0)),
        ],
        out_specs=out_specs,
        out_shape=out_shape,
        compiler_params=_cparams(("parallel",)),
        name="outproj",
    )(x, y, w, nw)


def kernel(x, norm_w, w_in, cmp_pe, cmp_w1, cmp_b1, cmp_w2, cmp_b2, gla_alpha_w, gla_alpha_b, gla_norm_w,
           p_a, p_b, w_out, final_norm_w):
    B, T, D = x.shape
    L = norm_w.shape[0]
    assert B == 1 and D == D_MODEL and T % 512 == 0 and T // SEL_BLOCK <= NB_PAD
    assert w_in.shape[2] == SRC_END
    w_t = jnp.swapaxes(w_in, 1, 2)

    w1_b = cmp_w1.astype(BF16)
    w2_b = cmp_w2.astype(BF16)
    b1_r = cmp_b1.reshape(L, 2, 1, CMP_HIDDEN)
    b2_r = cmp_b2.reshape(L, 2, 1, A_DH)
    aw_p = jnp.pad(gla_alpha_w, ((0, 0), (0, LANES - B_RANK), (0, 0))).astype(BF16)
    ab_r = gla_alpha_b.reshape(L, 1, B_HEADS * B_DK)
    gnw_r = gla_norm_w.reshape(L, 1, B_DV)
    pa_b = p_a.astype(BF16)
    pb_b = p_b.astype(BF16)
    wo_b = w_out.astype(BF16)
    nw_next = jnp.concatenate([norm_w[1:], final_norm_w[None, :]], axis=0).reshape(L, 1, D)

    xs = x.reshape(T, D)
    h = _rmsnorm(xs, norm_w[0].reshape(1, D))
    for l in range(L):
        qa = _proj(h, w_t, l, SRC_Q_A, A_HEADS * A_DH, 512, BF16, scale=A_DH ** -0.5 * LOG2E)
        kv_cmp = _proj(h, w_t, l, SRC_KV_CMP, 2 * A_KVW, 512, F32)
        kvsw = _proj(h, w_t, l, SRC_KV_SW, 4 * A_KVW, 512, BF16)
        gbr = _proj(h, w_t, l, SRC_G_BR, LANES, LANES, BF16, valid=G_BR_COLS)
        cols2 = _proj(h, w_t, l, SRC_R2, SRC_A_LR - SRC_R2, 512, BF16)
        alr = _proj(h, w_t, l, SRC_A_LR, LANES, LANES, BF16, valid=B_RANK)
        cols3 = _proj(h, w_t, l, SRC_R3, SRC_END - SRC_R3, 512, BF16)
        kvc = _compress(kv_cmp, cmp_pe[l], w1_b[l], b1_r[l], w2_b[l], b2_r[l])
        o_cmp, qsel, o_win = _cmpwin_attn(qa, kvc, kvsw)
        o_sel = _sel_attn(qa, kvsw, qsel)
        o_b = _gla(cols2, alr, aw_p[l], ab_r[l], gnw_r[l])
        y = _merge(o_cmp, o_sel, o_win, gbr, cols2, o_b, cols3, pa_b, pb_b, l)
        if l == L - 1:
            xs = _outproj(xs, y, wo_b, l, nw_next[l], last=True)
        else:
            xs, h = _outproj(xs, y, wo_b, l, nw_next[l], last=False)
    return xs.reshape(B, T, D)
```

```python
import functools

import jax
import jax.numpy as jnp
import numpy as np
from jax import lax
from jax.experimental import pallas as pl
from jax.experimental.pallas import tpu as pltpu

F32 = jnp.float32
BF16 = jnp.bfloat16

D_MODEL = 2048
A_HEADS = 16
A_GROUPS = 4
A_HPG = A_HEADS // A_GROUPS
A_DH = 128
A_KVW = A_GROUPS * A_DH
CMP_BLOCK = 32
CMP_STRIDE = 16
CMP_HIDDEN = 256
SEL_BLOCK = 64
SEL_SHIFT = 6
SEL_TOP_N = 16
WIN_SIZE = 512
B_HEADS = 4
B_DK = 256
B_DV = 512
B_RANK = 16
B_GATE_TEMP = 16.0
B_CHUNK = 64
NORM_EPS = 1e-6

LANES = 128
NB_PAD = LANES
CMP_WIDTH_CLASSES = 4
MASK_BIG = 2.0 ** 100
LOG2E = float(np.log2(np.e))

SRC_Q_A = 0
SRC_KV_CMP = 2048
SRC_KV_SW = 3072
SRC_G_BR = 5120
SRC_R2 = 5168
SRC_A_LR = 11312
SRC_R3 = 11328
SRC_END = 17472
G_BR_COLS = 48
KVSW_K_SEL, KVSW_V_SEL, KVSW_K_WIN, KVSW_V_WIN = 0, 512, 1024, 1536
R2_Z_A, R2_Q_B, R2_K_B, R2_V_B = 0, 2048, 3072, 4096
R3_Z_B, R3_GATE_A, R3_GATE_B = 0, 2048, 4096

VMEM_LIMIT = 56 * 1024 * 1024


def _cparams(sem):
    return pltpu.CompilerParams(dimension_semantics=sem, vmem_limit_bytes=VMEM_LIMIT)


def _dot(a, b):
    return jnp.dot(a, b, preferred_element_type=F32)


def _dot_nt(a, b):
    return lax.dot_general(a, b, (((1,), (1,)), ((), ())), preferred_element_type=F32)


def _dot_tn(a, b):
    return lax.dot_general(a, b, (((0,), (0,)), ((), ())), preferred_element_type=F32)


def _split_bf16(x):
    hi = x.astype(BF16)
    lo = (x - hi.astype(F32)).astype(BF16)
    return hi, lo


def _rmsnorm_kernel(x_ref, w_ref, o_ref):
    x = x_ref[...]
    ms = jnp.mean(x * x, axis=-1, keepdims=True)
    o_ref[...] = (x * lax.rsqrt(ms + NORM_EPS) * w_ref[...]).astype(o_ref.dtype)


def _rmsnorm(x, w, tm=512):
    T, D = x.shape
    tm = min(tm, T)
    return pl.pallas_call(
        _rmsnorm_kernel,
        grid=(T // tm,),
        in_specs=[pl.BlockSpec((tm, D), lambda i: (i, 0)), pl.BlockSpec((1, D), lambda i: (0, 0))],
        out_specs=pl.BlockSpec((tm, D), lambda i: (i, 0)),
        out_shape=jax.ShapeDtypeStruct((T, D), BF16),
        compiler_params=_cparams(("parallel",)),
        name="rmsnorm",
    )(x, w)


def _proj_kernel(h_ref, *refs, shift, valid, scale):
    if shift:
        wa_ref, wb_ref, o_ref, wc_ref = refs
    else:
        wa_ref, o_ref, wc_ref = refs
    tn = o_ref.shape[1]

    @pl.when(pl.program_id(1) == 0)
    def _():
        if shift:
            wc_ref[0:tn - shift, :] = wa_ref[shift:tn, :].astype(BF16)
            wc_ref[tn - shift:tn, :] = wb_ref[0:shift, :].astype(BF16)
        else:
            wc_ref[...] = wa_ref[...].astype(BF16)
        if valid < tn:
            wc_ref[valid:tn, :] = jnp.zeros((tn - valid, wc_ref.shape[1]), BF16)

    r = _dot_nt(h_ref[...], wc_ref[...])
    if scale != 1.0:
        r = r * scale
    o_ref[...] = r.astype(o_ref.dtype)


def _proj(h, w_t, layer, col0, width, tn, out_dtype, scale=1.0, valid=None, tm=2048):
    T, D = h.shape
    tm = min(tm, T)
    shift = col0 % LANES
    a0 = col0 - shift
    valid = tn if valid is None else valid
    assert a0 % tn == 0 and width % tn == 0 and (valid == tn or width == tn)
    assert shift % 16 == 0 and valid % 16 == 0
    in_specs = [
        pl.BlockSpec((tm, D), lambda j, i: (i, 0)),
        pl.BlockSpec((None, tn, D), lambda j, i: (layer, a0 // tn + j, 0)),
    ]
    args = [h, w_t]
    if shift:
        in_specs.append(pl.BlockSpec((None, LANES, D), lambda j, i: (layer, (a0 + (j + 1) * tn) // LANES, 0)))
        args.append(w_t)
    return pl.pallas_call(
        functools.partial(_proj_kernel, shift=shift, valid=valid, scale=scale),
        grid=(width // tn, T // tm),
        in_specs=in_specs,
        out_specs=pl.BlockSpec((tm, tn), lambda j, i: (i, j)),
        out_shape=jax.ShapeDtypeStruct((T, width), out_dtype),
        scratch_shapes=[pltpu.VMEM((tn, D), BF16)],
        compiler_params=_cparams(("arbitrary", "arbitrary")),
        name="proj",
    )(*args)


def _gelu_tanh(x):
    c = np.float32(np.sqrt(2.0 / np.pi))
    return 0.5 * x * (1.0 + jnp.tanh(c * (x + np.float32(0.044715) * (x * x * x))))


def _compress_kernel(x_ref, pe_ref, w1_ref, b1_ref, w2_ref, b2_ref, o_ref, xlo_ref, xhi_ref):
    nch = o_ref.shape[2]
    half = CMP_STRIDE * A_DH
    for p in range(CMP_STRIDE):
        tok = x_ref[pl.ds(p, nch, stride=CMP_STRIDE), :]
        cs = slice(p * A_DH, (p + 1) * A_DH)
        xlo_ref[:, cs] = (tok + pe_ref[0, p:p + 1, :]).astype(BF16)
        xhi_ref[:, cs] = (tok + pe_ref[0, CMP_STRIDE + p:CMP_STRIDE + p + 1, :]).astype(BF16)
    a = _dot(xlo_ref[...], w1_ref[0, 0:half, :])
    b = _dot(xhi_ref[...], w1_ref[0, half:2 * half, :])
    b_next = pltpu.roll(b, nch - 1, axis=0)
    row = lax.broadcasted_iota(jnp.int32, b.shape, 0)
    b_next = jnp.where(row < nch - 1, b_next, 0.0)
    hid = _gelu_tanh(a + b_next + b1_ref[0])
    out = _dot(hid.astype(BF16), w2_ref[0]) + b2_ref[0]
    o_ref[0, 0] = out.astype(o_ref.dtype)


def _compress(kv, pe, w1, b1, w2, b2):
    T = kv.shape[0]
    G = A_GROUPS
    nch = T // CMP_STRIDE
    width = CMP_STRIDE * A_DH
    return pl.pallas_call(
        _compress_kernel,
        grid=(2, G),
        in_specs=[
            pl.BlockSpec((T, A_DH), lambda s, g: (0, s * G + g)),
            pl.BlockSpec((1, CMP_BLOCK, A_DH), lambda s, g: (s, 0, 0)),
            pl.BlockSpec((1, 2 * width, CMP_HIDDEN), lambda s, g: (s, 0, 0)),
            pl.BlockSpec((1, 1, CMP_HIDDEN), lambda s, g: (s, 0, 0)),
            pl.BlockSpec((1, CMP_HIDDEN, A_DH), lambda s, g: (s, 0, 0)),
            pl.BlockSpec((1, 1, A_DH), lambda s, g: (s, 0, 0)),
        ],
        out_specs=pl.BlockSpec((1, 1, nch, A_DH), lambda s, g: (s, g, 0, 0)),
        out_shape=jax.ShapeDtypeStruct((2, G, nch, A_DH), BF16),
        scratch_shapes=[pltpu.VMEM((nch, width), BF16), pltpu.VMEM((nch, width), BF16)],
        compiler_params=_cparams(("parallel", "parallel")),
        name="compress",
    )(kv, pe, w1, b1, w2, b2)


def _stack_heads(q_ref, qs_ref):
    tq = q_ref.shape[0]
    for h in range(A_HPG):
        qs_ref[h * tq:(h + 1) * tq, :] = q_ref[:, h * A_DH:(h + 1) * A_DH]


def _masked_chunks(s, masks, tq):
    out = []
    for c, mk in enumerate(masks):
        sc = s[:, c * LANES:(c + 1) * LANES]
        out.append(jnp.concatenate(
            [jnp.where(mk, sc[h * tq:(h + 1) * tq], -jnp.inf) for h in range(A_HPG)], axis=0))
    return out


def _chunk_softmax(sc):
    mx = sc[0]
    for x in sc[1:]:
        mx = jnp.maximum(mx, x)
    m = jnp.max(mx, axis=-1, keepdims=True)
    m = jnp.where(m > -jnp.inf, m, 0.0)
    pc = [jnp.exp2(x - m) for x in sc]
    ps = pc[0]
    for x in pc[1:]:
        ps = ps + x
    return pc, jnp.sum(ps, axis=-1, keepdims=True)


def _cmpwin_kernel(q_ref, kc_ref, vc_ref, k0_ref, k1_ref, k2_ref, v0_ref, v1_ref, v2_ref,
                   ocmp_ref, qsel_ref, owin_ref, qs_ref, imp_ref, p_ref, inv_ref):
    i = pl.program_id(1)

    @pl.when(i == 0)
    def _():
        imp_ref[1] = jnp.zeros(imp_ref.shape[1:], F32)
        p_ref[1] = jnp.zeros(p_ref.shape[1:], BF16)
        inv_ref[1] = jnp.ones(inv_ref.shape[1:], F32)

    def step(slot, nck):
        _stack_heads(q_ref, qs_ref)
        _cmp_attn_step(kc_ref, vc_ref, ocmp_ref, qsel_ref, qs_ref, imp_ref, slot, nck)
        _win_attn_step(k0_ref, k1_ref, k2_ref, v0_ref, v1_ref, v2_ref, owin_ref, qs_ref, p_ref, inv_ref, slot)

    tq = q_ref.shape[0]
    nch = kc_ref.shape[2]
    t_last = jnp.minimum(i, pl.num_programs(1) - 2) * tq + tq - 1
    need = (t_last - (CMP_BLOCK - 1)) // CMP_STRIDE + 1
    nw = min(CMP_WIDTH_CLASSES, nch // LANES)
    widths = [nch * (w + 1) // nw for w in range(nw)] if nch % (nw * LANES) == 0 else [nch]
    for slot in (0, 1):
        for wi, nck in enumerate(widths):
            fits = need <= nck
            if wi > 0:
                fits = fits & (need > widths[wi - 1])
            elif len(widths) == 1:
                fits = True
            pl.when((i % 2 == slot) & fits)(functools.partial(step, slot, nck))


def _cmp_attn_step(kc_ref, vc_ref, o_ref, qsel_ref, qs_ref, imp_ref, slot, nch):
    tq = qsel_ref.shape[0]
    i = pl.program_id(1)
    t0 = jnp.minimum(i, pl.num_programs(1) - 2) * tq
    s = _dot_nt(qs_ref[...], kc_ref[0, 0, 0:nch, :])
    tpos = t0 + lax.broadcasted_iota(jnp.int32, (tq, LANES), 0)
    lane = lax.broadcasted_iota(jnp.int32, (tq, LANES), 1)
    masks = [(lane + c * LANES) * CMP_STRIDE + (CMP_BLOCK - 1) <= tpos for c in range(nch // LANES)]
    pc, denom = _chunk_softmax(_masked_chunks(s, masks, tq))
    inv = 1.0 / jnp.where(denom > 0, denom, 1.0)
    o = _dot(jnp.concatenate([x.astype(BF16) for x in pc], axis=1), vc_ref[0, 0, 0:nch, :]) * inv
    for h in range(A_HPG):
        o_ref[:, h * A_DH:(h + 1) * A_DH] = o[h * tq:(h + 1) * tq].astype(o_ref.dtype)
    pn = [x * inv for x in pc]
    psum = jnp.concatenate(
        [sum(x[h * tq:(h + 1) * tq] for h in range(A_HPG)) for x in pn], axis=1)

    jj = lax.broadcasted_iota(jnp.int32, (NB_PAD, nch), 0)
    cs = lax.broadcasted_iota(jnp.int32, (NB_PAD, nch), 1) * CMP_STRIDE
    ov = ((cs <= jj * SEL_BLOCK + SEL_BLOCK - 1) & (cs + CMP_BLOCK - 1 >= jj * SEL_BLOCK))
    ov = jnp.where(ov, 1.0, 0.0).astype(BF16)
    p_hi, p_lo = _split_bf16(psum)
    imp_ref[slot] = _dot_nt(ov, p_hi) + _dot_nt(ov, p_lo)

    pslc = imp_ref[1 - slot]
    jf = lax.broadcasted_iota(jnp.int32, (NB_PAD, tq), 0)
    tl = jnp.maximum(i - 1, 0) * tq + lax.broadcasted_iota(jnp.int32, (NB_PAD, tq), 1)
    cur = jnp.right_shift(tl, SEL_SHIFT)
    valid = jf <= cur
    forced = (jf == 0) | (jf == cur) | (jf == cur - 1)
    score = jnp.where(valid & ~forced, pslc, -jnp.inf)
    jff = jf.astype(F32)
    for _ in range(SEL_TOP_N - 3):
        mx = jnp.max(score, axis=0, keepdims=True)
        first = jnp.min(jnp.where(score == mx, jff, float(NB_PAD)), axis=0, keepdims=True)
        score = jnp.where(jff == first, -jnp.inf, score)
    neg = jnp.where(valid & (score == -jnp.inf), 0.0, -MASK_BIG)
    qsel_ref[...] = neg.T.astype(qsel_ref.dtype)


def _win_attn_step(k0_ref, k1_ref, k2_ref, v0_ref, v1_ref, v2_ref, o_ref, qs_ref, p_ref, inv_ref, slot):
    tq = o_ref.shape[0]
    i = pl.program_id(1)
    ia = jnp.minimum(i, pl.num_programs(1) - 2)
    k = jnp.concatenate([k0_ref[...], k1_ref[...], k2_ref[...]], axis=0)
    s = _dot_nt(qs_ref[...], k)
    r = lax.broadcasted_iota(jnp.int32, (tq, LANES), 0)
    lane = lax.broadcasted_iota(jnp.int32, (tq, LANES), 1)
    masks = []
    for c in range(3 * tq // LANES):
        col = lane + c * LANES
        rel = 2 * tq + r - col
        masks.append((rel >= 0) & (rel < WIN_SIZE) & ((ia - 2) * tq + col >= 0))
    pc, denom = _chunk_softmax(_masked_chunks(s, masks, tq))

    v = jnp.concatenate([v0_ref[...], v1_ref[...], v2_ref[...]], axis=0)
    o = _dot(p_ref[1 - slot], v) * inv_ref[1 - slot]
    for h in range(A_HPG):
        o_ref[:, h * A_DH:(h + 1) * A_DH] = o[h * tq:(h + 1) * tq].astype(o_ref.dtype)

    p_ref[slot] = jnp.concatenate([x.astype(BF16) for x in pc], axis=1)
    inv_ref[slot] = jnp.broadcast_to(1.0 / denom, inv_ref.shape[1:])


def _cmpwin_attn(qa, kvc, kvsw, tq=256):
    T = qa.shape[0]
    G = A_GROUPS
    nch = kvc.shape[2]
    tq = min(tq, T)
    assert WIN_SIZE <= 2 * tq
    qw = A_HPG * A_DH
    n = T // tq
    cur = lambda g, i: (jnp.minimum(i, n - 1), g)
    prev = lambda g, i: (jnp.maximum(i - 1, 0), g)

    def kv_spec(col, back, lag):
        def f(g, i):
            tile = jnp.clip(i - lag, 0, n - 1)
            return (jnp.maximum(tile - back, 0), col // A_DH + g)
        return pl.BlockSpec((tq, A_DH), f)

    return pl.pallas_call(
        _cmpwin_kernel,
        grid=(G, n + 1),
        in_specs=[
            pl.BlockSpec((tq, qw), cur),
            pl.BlockSpec((1, 1, nch, A_DH), lambda g, i: (0, g, 0, 0)),
            pl.BlockSpec((1, 1, nch, A_DH), lambda g, i: (1, g, 0, 0)),
            kv_spec(KVSW_K_WIN, 2, 0), kv_spec(KVSW_K_WIN, 1, 0), kv_spec(KVSW_K_WIN, 0, 0),
            kv_spec(KVSW_V_WIN, 2, 1), kv_spec(KVSW_V_WIN, 1, 1), kv_spec(KVSW_V_WIN, 0, 1),
        ],
        out_specs=[
            pl.BlockSpec((tq, qw), cur),
            pl.BlockSpec((tq, NB_PAD), prev),
            pl.BlockSpec((tq, qw), prev),
        ],
        out_shape=[
            jax.ShapeDtypeStruct((T, A_HEADS * A_DH), BF16),
            jax.ShapeDtypeStruct((T, G * NB_PAD), BF16),
            jax.ShapeDtypeStruct((T, A_HEADS * A_DH), BF16),
        ],
        scratch_shapes=[pltpu.VMEM((A_HPG * tq, A_DH), BF16),
                        pltpu.VMEM((2, NB_PAD, tq), F32),
                        pltpu.VMEM((2, A_HPG * tq, 3 * tq), BF16),
                        pltpu.VMEM((2, A_HPG * tq, LANES), F32)],
        compiler_params=_cparams(("parallel", "arbitrary")),
        name="cmpwin_attn",
    )(qa, kvc, kvc, kvsw, kvsw, kvsw, kvsw, kvsw, kvsw)


def _sel_attn_kernel(q_ref, qsel_ref, k_ref, v_ref, oh_ref, o_ref, qa_ref, m_ref, l_ref, acc_ref, *, tk):
    tq = q_ref.shape[0]
    nsub = k_ref.shape[0] // tk
    i = pl.program_id(1)
    kp = pl.program_id(2)
    last_kt = (i * tq + tq - 1) // tk
    nlc = tk // LANES

    @pl.when(kp == 0)
    def _():
        for h in range(A_HPG):
            qa_ref[h * tq:(h + 1) * tq, 0:A_DH] = q_ref[:, h * A_DH:(h + 1) * A_DH]
            qa_ref[h * tq:(h + 1) * tq, A_DH:2 * A_DH] = qsel_ref[...]
        m_ref[...] = jnp.full(m_ref.shape, -jnp.inf, F32)
        l_ref[...] = jnp.zeros(l_ref.shape, F32)
        acc_ref[...] = jnp.zeros(acc_ref.shape, F32)

    def step(diagonal, kt, off):
        rows = pl.ds(off, tk)
        ka = jnp.concatenate([k_ref[rows, :], oh_ref[rows, :]], axis=1)
        s = _dot_nt(qa_ref[...], ka)
        sc = [s[:, c * LANES:(c + 1) * LANES] for c in range(nlc)]
        if diagonal:
            tpos = i * tq + lax.broadcasted_iota(jnp.int32, (tq, LANES), 0)
            lane = kt * tk + lax.broadcasted_iota(jnp.int32, (tq, LANES), 1)
            for c in range(nlc):
                causal = lane + c * LANES <= tpos
                sc[c] = jnp.concatenate(
                    [jnp.where(causal, sc[c][h * tq:(h + 1) * tq], -MASK_BIG) for h in range(A_HPG)], axis=0)
        mx = sc[0]
        for c in range(1, nlc):
            mx = jnp.maximum(mx, sc[c])
        m_old = m_ref[...]
        m_new = jnp.maximum(m_old, jnp.max(mx, axis=-1, keepdims=True))
        alpha = jnp.exp2(m_old - m_new)
        pc = [jnp.exp2(sc[c] - m_new) for c in range(nlc)]
        ps = pc[0]
        for c in range(1, nlc):
            ps = ps + pc[c]
        l_ref[...] = alpha * l_ref[...] + ps
        p = jnp.concatenate([x.astype(BF16) for x in pc], axis=1)
        acc_ref[...] = alpha * acc_ref[...] + _dot(p, v_ref[rows, :])
        m_ref[...] = m_new

    def sub_tile(sub, carry):
        kt = kp * nsub + sub
        off = pl.multiple_of(sub * tk, tk)
        pl.when(kt < last_kt)(functools.partial(step, False, kt, off))
        pl.when(kt == last_kt)(functools.partial(step, True, kt, off))
        return carry

    all_interior = kp * nsub + nsub - 1 < last_kt

    @pl.when(all_interior)
    def _():
        for sub in range(nsub):
            step(False, kp * nsub + sub, sub * tk)

    @pl.when(jnp.logical_not(all_interior))
    def _():
        lax.fori_loop(0, nsub, sub_tile, 0)

    @pl.when(kp == pl.num_programs(2) - 1)
    def _():
        for h in range(A_HPG):
            l = jnp.sum(l_ref[h * tq:(h + 1) * tq], axis=-1, keepdims=True)
            o = acc_ref[h * tq:(h + 1) * tq] / l
            o_ref[:, h * A_DH:(h + 1) * A_DH] = o.astype(o_ref.dtype)


def _sel_attn(qa, kvsw, qsel, tq=512, tk=512, kb=2048):
    T = qa.shape[0]
    G = A_GROUPS
    tq = min(tq, T)
    tk = min(tk, T)
    kb = min(kb, T)
    qw = A_HPG * A_DH
    rows = A_HPG * tq

    def kv_map(col, per_group):
        def f(g, i, kp):
            return (jnp.minimum(kp, (i * tq + tq - 1) // kb), col // A_DH + g * per_group)
        return f

    onehot = jnp.asarray(np.arange(T)[:, None] // SEL_BLOCK == np.arange(NB_PAD)[None, :], dtype=BF16)

    return pl.pallas_call(
        functools.partial(_sel_attn_kernel, tk=tk),
        grid=(G, T // tq, T // kb),
        in_specs=[
            pl.BlockSpec((tq, qw), lambda g, i, kp: (i, g)),
            pl.BlockSpec((tq, NB_PAD), lambda g, i, kp: (i, g)),
            pl.BlockSpec((kb, A_DH), kv_map(KVSW_K_SEL, 1)),
            pl.BlockSpec((kb, A_DH), kv_map(KVSW_V_SEL, 1)),
            pl.BlockSpec((kb, NB_PAD), kv_map(0, 0)),
        ],
        out_specs=pl.BlockSpec((tq, qw), lambda g, i, kp: (i, g)),
        out_shape=jax.ShapeDtypeStruct((T, A_HEADS * A_DH), BF16),
        scratch_shapes=[
            pltpu.VMEM((rows, 2 * A_DH), BF16),
            pltpu.VMEM((rows, LANES), F32),
            pltpu.VMEM((rows, LANES), F32),
            pltpu.VMEM((rows, A_DH), F32),
        ],
        compiler_params=_cparams(("parallel", "parallel", "arbitrary")),
        name="sel_attn",
    )(qa, qsel, kvsw, kvsw, onehot)


def _gla_kernel(q_ref, k_ref, v_ref, a_ref, aw_ref, ab_ref, nw_ref, o_ref, st_ref):
    tc = q_ref.shape[0]
    C = B_CHUNK

    @pl.when(pl.program_id(0) == 0)
    def _():
        st_ref[...] = jnp.zeros(st_ref.shape, F32)

    logits = _dot(a_ref[...], aw_ref[...]) + ab_ref[...]
    log_a = jax.nn.log_sigmoid(logits) * (1.0 / B_GATE_TEMP)
    ri = lax.broadcasted_iota(jnp.int32, (C, C), 0)
    ci = lax.broadcasted_iota(jnp.int32, (C, C), 1)
    tril = ci <= ri
    tril_b = jnp.where(tril, 1.0, 0.0).astype(BF16)
    nw = nw_ref[...]
    for c in range(tc // C):
        sl = slice(c * C, (c + 1) * C)
        la_hi, la_lo = _split_bf16(log_a[sl])
        cum = _dot(tril_b, la_hi) + _dot(tril_b, la_lo)
        last = cum[C - 1:C, :]
        qc = q_ref[sl, :].astype(F32) * (B_DK ** -0.5)
        kc = k_ref[sl, :].astype(F32)
        q_dec = (qc * jnp.exp(cum)).astype(BF16)
        k_inv = (kc * jnp.exp(-cum)).astype(BF16)
        k_state = (kc * jnp.exp(last - cum)).astype(BF16)
        dec = jnp.exp(last)
        for h in range(B_HEADS):
            ks = slice(h * B_DK, (h + 1) * B_DK)
            vs = slice(h * B_DV, (h + 1) * B_DV)
            vc = v_ref[sl, vs]
            att = jnp.where(tril, _dot_nt(q_dec[:, ks], k_inv[:, ks]), 0.0)
            st = st_ref[h]
            o = _dot(att.astype(BF16), vc) + _dot_nt(q_dec[:, ks], st.astype(BF16))
            st_ref[h] = st * dec[:, ks] + _dot_tn(vc, k_state[:, ks])
            o = o * lax.rsqrt(jnp.mean(o * o, axis=-1, keepdims=True) + NORM_EPS) * nw
            o_ref[sl, vs] = o.astype(o_ref.dtype)


def _gla(cols2, alr, aw, ab, nw, tc=512):
    T = cols2.shape[0]
    tc = min(tc, T)
    kw = B_HEADS * B_DK
    vw = B_HEADS * B_DV
    return pl.pallas_call(
        _gla_kernel,
        grid=(T // tc,),
        in_specs=[
            pl.BlockSpec((tc, kw), lambda i: (i, R2_Q_B // kw)),
            pl.BlockSpec((tc, kw), lambda i: (i, R2_K_B // kw)),
            pl.BlockSpec((tc, vw), lambda i: (i, R2_V_B // vw)),
            pl.BlockSpec((tc, LANES), lambda i: (i, 0)),
            pl.BlockSpec((LANES, kw), lambda i: (0, 0)),
            pl.BlockSpec((1, kw), lambda i: (0, 0)),
            pl.BlockSpec((1, B_DV), lambda i: (0, 0)),
        ],
        out_specs=pl.BlockSpec((tc, vw), lambda i: (i, 0)),
        out_shape=jax.ShapeDtypeStruct((T, vw), BF16),
        scratch_shapes=[pltpu.VMEM((B_HEADS, B_DV, B_DK), F32)],
        compiler_params=_cparams(("arbitrary",)),
        name="gla",
    )(cols2, cols2, cols2, alr, aw, ab, nw)


def _silu(x):
    return x * jax.nn.sigmoid(x)


def _merge_a_kernel(ocmp_ref, osel_ref, owin_ref, gbr_ref, za_ref, ga_ref, pa_ref, y_ref, ua_ref):
    i = pl.program_id(0)

    @pl.when(i == 0)
    def _():
        ua_ref[1] = jnp.zeros(ua_ref.shape[1:], BF16)

    def step(slot):
        sg = jax.nn.sigmoid(gbr_ref[...].astype(F32))
        for hh in range(A_HEADS):
            cs = slice(hh * A_DH, (hh + 1) * A_DH)
            oa = (sg[:, 3 * hh:3 * hh + 1] * ocmp_ref[:, cs].astype(F32)
                  + sg[:, 3 * hh + 1:3 * hh + 2] * osel_ref[:, cs].astype(F32)
                  + sg[:, 3 * hh + 2:3 * hh + 3] * owin_ref[:, cs].astype(F32))
            ua_ref[slot, :, cs] = (oa * _silu(za_ref[:, cs].astype(F32))).astype(BF16)
        ya = _dot(ua_ref[1 - slot], pa_ref[...])
        y_ref[...] = (jax.nn.sigmoid(ga_ref[...].astype(F32)) * ya).astype(y_ref.dtype)

    for slot in (0, 1):
        pl.when(i % 2 == slot)(functools.partial(step, slot))


def _merge_b_kernel(ya_ref, ob_ref, zb_ref, gb_ref, pb_ref, y_ref):
    ub = (ob_ref[...].astype(F32) * _silu(zb_ref[...].astype(F32))).astype(BF16)
    yb = _dot(ub, pb_ref[...])
    y = ya_ref[...].astype(F32) + jax.nn.sigmoid(gb_ref[...].astype(F32)) * yb
    y_ref[...] = y.astype(y_ref.dtype)


def _merge(o_cmp, o_sel, o_win, gbr, cols2, o_b, cols3, pa, pb, layer, tm=512):
    T = gbr.shape[0]
    D = D_MODEL
    tm = min(tm, T)
    row = lambda c: pl.BlockSpec((tm, D), lambda i: (i, c))
    const = lambda: pl.BlockSpec((None, D, D), lambda i: (layer, 0, 0), pipeline_mode=pl.Buffered(1))
    n = T // tm
    cur = lambda c: pl.BlockSpec((tm, D), lambda i: (jnp.minimum(i, n - 1), c))
    prev = lambda c: pl.BlockSpec((tm, D), lambda i: (jnp.maximum(i - 1, 0), c))
    ya = pl.pallas_call(
        _merge_a_kernel,
        grid=(n + 1,),
        in_specs=[cur(0), cur(0), cur(0),
                  pl.BlockSpec((tm, LANES), lambda i: (jnp.minimum(i, n - 1), 0)),
                  cur(R2_Z_A // D), prev(R3_GATE_A // D), const()],
        out_specs=prev(0),
        out_shape=jax.ShapeDtypeStruct((T, D), BF16),
        scratch_shapes=[pltpu.VMEM((2, tm, D), BF16)],
        compiler_params=_cparams(("arbitrary",)),
        name="merge_a",
    )(o_cmp, o_sel, o_win, gbr, cols2, cols3, pa)
    return pl.pallas_call(
        _merge_b_kernel,
        grid=(T // tm,),
        in_specs=[row(0), row(0), row(R3_Z_B // D), row(R3_GATE_B // D), const()],
        out_specs=row(0),
        out_shape=jax.ShapeDtypeStruct((T, D), BF16),
        compiler_params=_cparams(("parallel",)),
        name="merge_b",
    )(ya, o_b, cols3, cols3, pb)


def _outproj_kernel(x_ref, y_ref, w_ref, nw_ref, *out_refs, last):
    xn = x_ref[...] + _dot(y_ref[...], w_ref[...])
    ms = jnp.mean(xn * xn, axis=-1, keepdims=True)
    hn = xn * lax.rsqrt(ms + NORM_EPS) * nw_ref[...]
    if last:
        out_refs[0][...] = hn
    else:
        out_refs[0][...] = xn
        out_refs[1][...] = hn.astype(BF16)


def _outproj(x, y, w, layer, nw, last, tm=512):
    T, D = x.shape
    tm = min(tm, T)
    row = pl.BlockSpec((tm, D), lambda i: (i, 0))
    if last:
        out_specs, out_shape = row, jax.ShapeDtypeStruct((T, D), F32)
    else:
        out_specs = [row, row]
        out_shape = [jax.ShapeDtypeStruct((T, D), F32), jax.ShapeDtypeStruct((T, D), BF16)]
    return pl.pallas_call(
        functools.partial(_outproj_kernel, last=last),
        grid=(T // tm,),
        in_specs=[
            row, row,
            pl.BlockSpec((None, D, D), lambda i: (layer, 0, 0), pipeline_mode=pl.Buffered(1)),
            pl.BlockSpec((1, D), lambda i: (0, 0)),
        ],
        out_specs=out_specs,
        out_shape=out_shape,
        compiler_params=_cparams(("parallel",)),
        name="outproj",
    )(x, y, w, nw)


def kernel(x, norm_w, w_in, cmp_pe, cmp_w1, cmp_b1, cmp_w2, cmp_b2, gla_alpha_w, gla_alpha_b, gla_norm_w,
           p_a, p_b, w_out, final_norm_w):
    B, T, D = x.shape
    L = norm_w.shape[0]
    assert B == 1 and D == D_MODEL and T % 512 == 0 and T // SEL_BLOCK <= NB_PAD
    assert w_in.shape[2] == SRC_END
    w_t = jnp.swapaxes(w_in, 1, 2)

    w1_b = cmp_w1.astype(BF16)
    w2_b = cmp_w2.astype(BF16)
    b1_r = cmp_b1.reshape(L, 2, 1, CMP_HIDDEN)
    b2_r = cmp_b2.reshape(L, 2, 1, A_DH)
    aw_p = jnp.pad(gla_alpha_w, ((0, 0), (0, LANES - B_RANK), (0, 0))).astype(BF16)
    ab_r = gla_alpha_b.reshape(L, 1, B_HEADS * B_DK)
    gnw_r = gla_norm_w.reshape(L, 1, B_DV)
    pa_b = p_a.astype(BF16)
    pb_b = p_b.astype(BF16)
    wo_b = w_out.astype(BF16)
    nw_next = jnp.concatenate([norm_w[1:], final_norm_w[None, :]], axis=0).reshape(L, 1, D)

    xs = x.reshape(T, D)
    h = _rmsnorm(xs, norm_w[0].reshape(1, D))
    for l in range(L):
        qa = _proj(h, w_t, l, SRC_Q_A, A_HEADS * A_DH, 512, BF16, scale=A_DH ** -0.5 * LOG2E)
        kv_cmp = _proj(h, w_t, l, SRC_KV_CMP, 2 * A_KVW, 512, F32)
        kvsw = _proj(h, w_t, l, SRC_KV_SW, 4 * A_KVW, 512, BF16)
        gbr = _proj(h, w_t, l, SRC_G_BR, LANES, LANES, BF16, valid=G_BR_COLS)
        cols2 = _proj(h, w_t, l, SRC_R2, SRC_A_LR - SRC_R2, 512, BF16)
        alr = _proj(h, w_t, l, SRC_A_LR, LANES, LANES, BF16, valid=B_RANK)
        cols3 = _proj(h, w_t, l, SRC_R3, SRC_END - SRC_R3, 512, BF16)
        kvc = _compress(kv_cmp, cmp_pe[l], w1_b[l], b1_r[l], w2_b[l], b2_r[l])
        o_cmp, qsel, o_win = _cmpwin_attn(qa, kvc, kvsw)
        o_sel = _sel_attn(qa, kvsw, qsel)
        o_b = _gla(cols2, alr, aw_p[l], ab_r[l], gnw_r[l])
        y = _merge(o_cmp, o_sel, o_win, gbr, cols2, o_b, cols3, pa_b, pb_b, l)
        if l == L - 1:
            xs = _outproj(xs, y, wo_b, l, nw_next[l], last=True)
        else:
            xs, h = _outproj(xs, y, wo_b, l, nw_next[l], last=False)
    return xs.reshape(B, T, D)
```

```python
import functools

import jax
import jax.numpy as jnp
import numpy as np
from jax import lax
from jax.experimental import pallas as pl
from jax.experimental.pallas import tpu as pltpu

F32 = jnp.float32
BF16 = jnp.bfloat16

D_MODEL = 2048
A_HEADS = 16
A_GROUPS = 4
A_HPG = A_HEADS // A_GROUPS
A_DH = 128
A_KVW = A_GROUPS * A_DH
CMP_BLOCK = 32
CMP_STRIDE = 16
CMP_HIDDEN = 256
SEL_BLOCK = 64
SEL_SHIFT = 6
SEL_TOP_N = 16
WIN_SIZE = 512
B_HEADS = 4
B_DK = 256
B_DV = 512
B_RANK = 16
B_GATE_TEMP = 16.0
B_CHUNK = 64
NORM_EPS = 1e-6

LANES = 128
NB_PAD = LANES
CMP_WIDTH_CLASSES = 4
MASK_BIG = 2.0 ** 100
LOG2E = float(np.log2(np.e))

SRC_Q_A = 0
SRC_KV_CMP = 2048
SRC_KV_SW = 3072
SRC_G_BR = 5120
SRC_R2 = 5168
SRC_A_LR = 11312
SRC_R3 = 11328
SRC_END = 17472
G_BR_COLS = 48
KVSW_K_SEL, KVSW_V_SEL, KVSW_K_WIN, KVSW_V_WIN = 0, 512, 1024, 1536
R2_Z_A, R2_Q_B, R2_K_B, R2_V_B = 0, 2048, 3072, 4096
R3_Z_B, R3_GATE_A, R3_GATE_B = 0, 2048, 4096

VMEM_LIMIT = 56 * 1024 * 1024


def _cparams(sem):
    return pltpu.CompilerParams(dimension_semantics=sem, vmem_limit_bytes=VMEM_LIMIT)


def _dot(a, b):
    return jnp.dot(a, b, preferred_element_type=F32)


def _dot_nt(a, b):
    return lax.dot_general(a, b, (((1,), (1,)), ((), ())), preferred_element_type=F32)


def _dot_tn(a, b):
    return lax.dot_general(a, b, (((0,), (0,)), ((), ())), preferred_element_type=F32)


def _split_bf16(x):
    hi = x.astype(BF16)
    lo = (x - hi.astype(F32)).astype(BF16)
    return hi, lo


def _rmsnorm_kernel(x_ref, w_ref, o_ref):
    x = x_ref[...]
    ms = jnp.mean(x * x, axis=-1, keepdims=True)
    o_ref[...] = (x * lax.rsqrt(ms + NORM_EPS) * w_ref[...]).astype(o_ref.dtype)


def _rmsnorm(x, w, tm=512):
    T, D = x.shape
    tm = min(tm, T)
    return pl.pallas_call(
        _rmsnorm_kernel,
        grid=(T // tm,),
        in_specs=[pl.BlockSpec((tm, D), lambda i: (i, 0)), pl.BlockSpec((1, D), lambda i: (0, 0))],
        out_specs=pl.BlockSpec((tm, D), lambda i: (i, 0)),
        out_shape=jax.ShapeDtypeStruct((T, D), BF16),
        compiler_params=_cparams(("parallel",)),
        name="rmsnorm",
    )(x, w)


def _proj_kernel(h_ref, *refs, shift, valid, scale):
    if shift:
        wa_ref, wb_ref, o_ref, wc_ref = refs
    else:
        wa_ref, o_ref, wc_ref = refs
    tn = o_ref.shape[1]

    @pl.when(pl.program_id(1) == 0)
    def _():
        if shift:
            wc_ref[0:tn - shift, :] = wa_ref[shift:tn, :].astype(BF16)
            wc_ref[tn - shift:tn, :] = wb_ref[0:shift, :].astype(BF16)
        else:
            wc_ref[...] = wa_ref[...].astype(BF16)
        if valid < tn:
            wc_ref[valid:tn, :] = jnp.zeros((tn - valid, wc_ref.shape[1]), BF16)

    r = _dot_nt(h_ref[...], wc_ref[...])
    if scale != 1.0:
        r = r * scale
    o_ref[...] = r.astype(o_ref.dtype)


def _proj_pipe_kernel(h_ref, *refs, shift, scale, ni):
    if shift:
        wa0_ref, wb0_ref, wa_ref, wb_ref, o_ref, wc_ref = refs
    else:
        wa0_ref, wa_ref, o_ref, wc_ref = refs
    tn = o_ref.shape[1]
    q = tn // ni
    j = pl.program_id(0)
    i = pl.program_id(1)

    @pl.when((j == 0) & (i == 0))
    def _():
        if shift:
            wc_ref[0, 0:tn - shift, :] = wa0_ref[shift:tn, :].astype(BF16)
            wc_ref[0, tn - shift:tn, :] = wb0_ref[0:shift, :].astype(BF16)
        else:
            wc_ref[0] = wa0_ref[...].astype(BF16)

    def step(slot):
        start = pl.multiple_of(jnp.minimum(i * q, tn - shift - q), 16)
        wc_ref[1 - slot, pl.ds(start, q), :] = wa_ref[pl.ds(start + shift, q), :].astype(BF16)
        if shift:
            wc_ref[1 - slot, tn - shift:tn, :] = wb_ref[0:shift, :].astype(BF16)
        r = _dot_nt(h_ref[...], wc_ref[slot])
        if scale != 1.0:
            r = r * scale
        o_ref[...] = r.astype(o_ref.dtype)

    for slot in (0, 1):
        pl.when(j % 2 == slot)(functools.partial(step, slot))


def _proj(h, w_t, layer, col0, width, tn, out_dtype, scale=1.0, valid=None, tm=2048):
    T, D = h.shape
    tm = min(tm, T)
    shift = col0 % LANES
    a0 = col0 - shift
    valid = tn if valid is None else valid
    assert a0 % tn == 0 and width % tn == 0 and (valid == tn or width == tn)
    assert shift % 16 == 0 and valid % 16 == 0
    nj, ni = width // tn, T // tm
    if nj > 1 and ni > 1 and (tn // ni) % 16 == 0 and tn - shift - tn // ni >= 0:
        nxt = lambda j: jnp.minimum(j + 1, nj - 1)
        once = pl.Buffered(1)
        w_specs = [pl.BlockSpec((None, tn, D), lambda j, i: (layer, a0 // tn, 0), pipeline_mode=once)]
        if shift:
            w_specs.append(pl.BlockSpec((None, LANES, D), lambda j, i: (layer, (a0 + tn) // LANES, 0),
                                        pipeline_mode=once))
        w_specs.append(pl.BlockSpec((None, tn, D), lambda j, i: (layer, a0 // tn + nxt(j), 0)))
        if shift:
            w_specs.append(pl.BlockSpec((None, LANES, D), lambda j, i: (layer, (a0 + (nxt(j) + 1) * tn) // LANES, 0)))
        return pl.pallas_call(
            functools.partial(_proj_pipe_kernel, shift=shift, scale=scale, ni=ni),
            grid=(nj, ni),
            in_specs=[pl.BlockSpec((tm, D), lambda j, i: (i, 0))] + w_specs,
            out_specs=pl.BlockSpec((tm, tn), lambda j, i: (i, j)),
            out_shape=jax.ShapeDtypeStruct((T, width), out_dtype),
            scratch_shapes=[pltpu.VMEM((2, tn, D), BF16)],
            compiler_params=_cparams(("arbitrary", "arbitrary")),
            name="proj_pipe",
        )(h, *([w_t] * len(w_specs)))
    in_specs = [
        pl.BlockSpec((tm, D), lambda j, i: (i, 0)),
        pl.BlockSpec((None, tn, D), lambda j, i: (layer, a0 // tn + j, 0)),
    ]
    args = [h, w_t]
    if shift:
        in_specs.append(pl.BlockSpec((None, LANES, D), lambda j, i: (layer, (a0 + (j + 1) * tn) // LANES, 0)))
        args.append(w_t)
    return pl.pallas_call(
        functools.partial(_proj_kernel, shift=shift, valid=valid, scale=scale),
        grid=(width // tn, T // tm),
        in_specs=in_specs,
        out_specs=pl.BlockSpec((tm, tn), lambda j, i: (i, j)),
        out_shape=jax.ShapeDtypeStruct((T, width), out_dtype),
        scratch_shapes=[pltpu.VMEM((tn, D), BF16)],
        compiler_params=_cparams(("arbitrary", "arbitrary")),
        name="proj",
    )(*args)


def _gelu_tanh(x):
    c = np.float32(np.sqrt(2.0 / np.pi))
    return 0.5 * x * (1.0 + jnp.tanh(c * (x + np.float32(0.044715) * (x * x * x))))


def _compress_kernel(x_ref, pe_ref, w1_ref, b1_ref, w2_ref, b2_ref, o_ref, xlo_ref, xhi_ref):
    nch = o_ref.shape[2]
    half = CMP_STRIDE * A_DH
    for p in range(CMP_STRIDE):
        tok = x_ref[pl.ds(p, nch, stride=CMP_STRIDE), :]
        cs = slice(p * A_DH, (p + 1) * A_DH)
        xlo_ref[:, cs] = (tok + pe_ref[0, p:p + 1, :]).astype(BF16)
        xhi_ref[:, cs] = (tok + pe_ref[0, CMP_STRIDE + p:CMP_STRIDE + p + 1, :]).astype(BF16)
    a = _dot(xlo_ref[...], w1_ref[0, 0:half, :])
    b = _dot(xhi_ref[...], w1_ref[0, half:2 * half, :])
    b_next = pltpu.roll(b, nch - 1, axis=0)
    row = lax.broadcasted_iota(jnp.int32, b.shape, 0)
    b_next = jnp.where(row < nch - 1, b_next, 0.0)
    hid = _gelu_tanh(a + b_next + b1_ref[0])
    out = _dot(hid.astype(BF16), w2_ref[0]) + b2_ref[0]
    o_ref[0, 0] = out.astype(o_ref.dtype)


def _compress(kv, pe, w1, b1, w2, b2):
    T = kv.shape[0]
    G = A_GROUPS
    nch = T // CMP_STRIDE
    width = CMP_STRIDE * A_DH
    return pl.pallas_call(
        _compress_kernel,
        grid=(2, G),
        in_specs=[
            pl.BlockSpec((T, A_DH), lambda s, g: (0, s * G + g)),
            pl.BlockSpec((1, CMP_BLOCK, A_DH), lambda s, g: (s, 0, 0)),
            pl.BlockSpec((1, 2 * width, CMP_HIDDEN), lambda s, g: (s, 0, 0)),
            pl.BlockSpec((1, 1, CMP_HIDDEN), lambda s, g: (s, 0, 0)),
            pl.BlockSpec((1, CMP_HIDDEN, A_DH), lambda s, g: (s, 0, 0)),
            pl.BlockSpec((1, 1, A_DH), lambda s, g: (s, 0, 0)),
        ],
        out_specs=pl.BlockSpec((1, 1, nch, A_DH), lambda s, g: (s, g, 0, 0)),
        out_shape=jax.ShapeDtypeStruct((2, G, nch, A_DH), BF16),
        scratch_shapes=[pltpu.VMEM((nch, width), BF16), pltpu.VMEM((nch, width), BF16)],
        compiler_params=_cparams(("parallel", "parallel")),
        name="compress",
    )(kv, pe, w1, b1, w2, b2)


def _stack_heads(q_ref, qs_ref):
    tq = q_ref.shape[0]
    for h in range(A_HPG):
        qs_ref[h * tq:(h + 1) * tq, :] = q_ref[:, h * A_DH:(h + 1) * A_DH]


def _masked_chunks(s, masks, tq):
    out = []
    for c, mk in enumerate(masks):
        sc = s[:, c * LANES:(c + 1) * LANES]
        out.append(jnp.concatenate(
            [jnp.where(mk, sc[h * tq:(h + 1) * tq], -jnp.inf) for h in range(A_HPG)], axis=0))
    return out


def _chunk_softmax(sc):
    mx = sc[0]
    for x in sc[1:]:
        mx = jnp.maximum(mx, x)
    m = jnp.max(mx, axis=-1, keepdims=True)
    m = jnp.where(m > -jnp.inf, m, 0.0)
    pc = [jnp.exp2(x - m) for x in sc]
    ps = pc[0]
    for x in pc[1:]:
        ps = ps + x
    return pc, jnp.sum(ps, axis=-1, keepdims=True)


def _cmpwin_kernel(q_ref, kc_ref, vc_ref, k0_ref, k1_ref, k2_ref, v0_ref, v1_ref, v2_ref,
                   ocmp_ref, qsel_ref, owin_ref, qs_ref, imp_ref, p_ref, inv_ref):
    i = pl.program_id(1)

    @pl.when(i == 0)
    def _():
        imp_ref[1] = jnp.zeros(imp_ref.shape[1:], F32)
        p_ref[1] = jnp.zeros(p_ref.shape[1:], BF16)
        inv_ref[1] = jnp.ones(inv_ref.shape[1:], F32)

    def step(slot, nck):
        _stack_heads(q_ref, qs_ref)
        _cmp_attn_step(kc_ref, vc_ref, ocmp_ref, qsel_ref, qs_ref, imp_ref, slot, nck)
        _win_attn_step(k0_ref, k1_ref, k2_ref, v0_ref, v1_ref, v2_ref, owin_ref, qs_ref, p_ref, inv_ref, slot)

    tq = q_ref.shape[0]
    nch = kc_ref.shape[2]
    t_last = jnp.minimum(i, pl.num_programs(1) - 2) * tq + tq - 1
    need = (t_last - (CMP_BLOCK - 1)) // CMP_STRIDE + 1
    nw = min(CMP_WIDTH_CLASSES, nch // LANES)
    widths = [nch * (w + 1) // nw for w in range(nw)] if nch % (nw * LANES) == 0 else [nch]
    for slot in (0, 1):
        for wi, nck in enumerate(widths):
            fits = need <= nck
            if wi > 0:
                fits = fits & (need > widths[wi - 1])
            elif len(widths) == 1:
                fits = True
            pl.when((i % 2 == slot) & fits)(functools.partial(step, slot, nck))


def _cmp_attn_step(kc_ref, vc_ref, o_ref, qsel_ref, qs_ref, imp_ref, slot, nch):
    tq = qsel_ref.shape[0]
    i = pl.program_id(1)
    t0 = jnp.minimum(i, pl.num_programs(1) - 2) * tq
    s = _dot_nt(qs_ref[...], kc_ref[0, 0, 0:nch, :])
    tpos = t0 + lax.broadcasted_iota(jnp.int32, (tq, LANES), 0)
    lane = lax.broadcasted_iota(jnp.int32, (tq, LANES), 1)
    masks = [(lane + c * LANES) * CMP_STRIDE + (CMP_BLOCK - 1) <= tpos for c in range(nch // LANES)]
    pc, denom = _chunk_softmax(_masked_chunks(s, masks, tq))
    inv = 1.0 / jnp.where(denom > 0, denom, 1.0)
    o = _dot(jnp.concatenate([x.astype(BF16) for x in pc], axis=1), vc_ref[0, 0, 0:nch, :]) * inv
    for h in range(A_HPG):
        o_ref[:, h * A_DH:(h + 1) * A_DH] = o[h * tq:(h + 1) * tq].astype(o_ref.dtype)
    pn = [x * inv for x in pc]
    psum = jnp.concatenate(
        [sum(x[h * tq:(h + 1) * tq] for h in range(A_HPG)) for x in pn], axis=1)

    jj = lax.broadcasted_iota(jnp.int32, (NB_PAD, nch), 0)
    cs = lax.broadcasted_iota(jnp.int32, (NB_PAD, nch), 1) * CMP_STRIDE
    ov = ((cs <= jj * SEL_BLOCK + SEL_BLOCK - 1) & (cs + CMP_BLOCK - 1 >= jj * SEL_BLOCK))
    ov = jnp.where(ov, 1.0, 0.0).astype(BF16)
    p_hi, p_lo = _split_bf16(psum)
    imp_ref[slot] = _dot_nt(ov, p_hi) + _dot_nt(ov, p_lo)

    nr = min(NB_PAD, nch * CMP_STRIDE // SEL_BLOCK)
    pslc = imp_ref[1 - slot, 0:nr, :]
    jf = lax.broadcasted_iota(jnp.int32, (nr, tq), 0)
    tl = jnp.maximum(i - 1, 0) * tq + lax.broadcasted_iota(jnp.int32, (nr, tq), 1)
    cur = jnp.right_shift(tl, SEL_SHIFT)
    valid = jf <= cur
    forced = (jf == 0) | (jf == cur) | (jf == cur - 1)
    score = jnp.where(valid & ~forced, pslc, -jnp.inf)
    jff = jf.astype(F32)
    for _ in range(SEL_TOP_N - 3):
        mx = jnp.max(score, axis=0, keepdims=True)
        first = jnp.min(jnp.where(score == mx, jff, float(NB_PAD)), axis=0, keepdims=True)
        score = jnp.where(jff == first, -jnp.inf, score)
    neg = jnp.where(valid & (score == -jnp.inf), 0.0, -MASK_BIG)
    if nr < NB_PAD:
        neg = jnp.concatenate([neg, jnp.full((NB_PAD - nr, tq), -MASK_BIG, F32)], axis=0)
    qsel_ref[...] = neg.T.astype(qsel_ref.dtype)


def _win_attn_step(k0_ref, k1_ref, k2_ref, v0_ref, v1_ref, v2_ref, o_ref, qs_ref, p_ref, inv_ref, slot):
    tq = o_ref.shape[0]
    i = pl.program_id(1)
    ia = jnp.minimum(i, pl.num_programs(1) - 2)
    k = jnp.concatenate([k0_ref[...], k1_ref[...], k2_ref[...]], axis=0)
    s = _dot_nt(qs_ref[...], k)
    r = lax.broadcasted_iota(jnp.int32, (tq, LANES), 0)
    lane = lax.broadcasted_iota(jnp.int32, (tq, LANES), 1)
    masks = []
    for c in range(3 * tq // LANES):
        col = lane + c * LANES
        rel = 2 * tq + r - col
        masks.append((rel >= 0) & (rel < WIN_SIZE) & ((ia - 2) * tq + col >= 0))
    pc, denom = _chunk_softmax(_masked_chunks(s, masks, tq))

    v = jnp.concatenate([v0_ref[...], v1_ref[...], v2_ref[...]], axis=0)
    o = _dot(p_ref[1 - slot], v) * inv_ref[1 - slot]
    for h in range(A_HPG):
        o_ref[:, h * A_DH:(h + 1) * A_DH] = o[h * tq:(h + 1) * tq].astype(o_ref.dtype)

    p_ref[slot] = jnp.concatenate([x.astype(BF16) for x in pc], axis=1)
    inv_ref[slot] = jnp.broadcast_to(1.0 / denom, inv_ref.shape[1:])


def _cmpwin_attn(qa, kvc, kvsw, tq=256):
    T = qa.shape[0]
    G = A_GROUPS
    nch = kvc.shape[2]
    tq = min(tq, T)
    assert WIN_SIZE <= 2 * tq
    qw = A_HPG * A_DH
    n = T // tq
    cur = lambda g, i: (jnp.minimum(i, n - 1), g)
    prev = lambda g, i: (jnp.maximum(i - 1, 0), g)

    def kv_spec(col, back, lag):
        def f(g, i):
            tile = jnp.clip(i - lag, 0, n - 1)
            return (jnp.maximum(tile - back, 0), col // A_DH + g)
        return pl.BlockSpec((tq, A_DH), f)

    return pl.pallas_call(
        _cmpwin_kernel,
        grid=(G, n + 1),
        in_specs=[
            pl.BlockSpec((tq, qw), cur),
            pl.BlockSpec((1, 1, nch, A_DH), lambda g, i: (0, g, 0, 0)),
            pl.BlockSpec((1, 1, nch, A_DH), lambda g, i: (1, g, 0, 0)),
            kv_spec(KVSW_K_WIN, 2, 0), kv_spec(KVSW_K_WIN, 1, 0), kv_spec(KVSW_K_WIN, 0, 0),
            kv_spec(KVSW_V_WIN, 2, 1), kv_spec(KVSW_V_WIN, 1, 1), kv_spec(KVSW_V_WIN, 0, 1),
        ],
        out_specs=[
            pl.BlockSpec((tq, qw), cur),
            pl.BlockSpec((tq, NB_PAD), prev),
            pl.BlockSpec((tq, qw), prev),
        ],
        out_shape=[
            jax.ShapeDtypeStruct((T, A_HEADS * A_DH), BF16),
            jax.ShapeDtypeStruct((T, G * NB_PAD), BF16),
            jax.ShapeDtypeStruct((T, A_HEADS * A_DH), BF16),
        ],
        scratch_shapes=[pltpu.VMEM((A_HPG * tq, A_DH), BF16),
                        pltpu.VMEM((2, NB_PAD, tq), F32),
                        pltpu.VMEM((2, A_HPG * tq, 3 * tq), BF16),
                        pltpu.VMEM((2, A_HPG * tq, LANES), F32)],
        compiler_params=_cparams(("parallel", "arbitrary")),
        name="cmpwin_attn",
    )(qa, kvc, kvc, kvsw, kvsw, kvsw, kvsw, kvsw, kvsw)


def _sel_attn_kernel(q_ref, qsel_ref, k_ref, v_ref, oh_ref, o_ref, qa_ref, m_ref, l_ref, acc_ref, *, tk):
    tq = q_ref.shape[0]
    nsub = k_ref.shape[0] // tk
    i = pl.program_id(1)
    kp = pl.program_id(2)
    last_kt = (i * tq + tq - 1) // tk
    nlc = tk // LANES

    @pl.when(kp == 0)
    def _():
        for h in range(A_HPG):
            qa_ref[h * tq:(h + 1) * tq, 0:A_DH] = q_ref[:, h * A_DH:(h + 1) * A_DH]
            qa_ref[h * tq:(h + 1) * tq, A_DH:2 * A_DH] = qsel_ref[...]
        m_ref[...] = jnp.full(m_ref.shape, -jnp.inf, F32)
        l_ref[...] = jnp.zeros(l_ref.shape, F32)
        acc_ref[...] = jnp.zeros(acc_ref.shape, F32)

    def step(diagonal, kt, off):
        rows = pl.ds(off, tk)
        ka = jnp.concatenate([k_ref[rows, :], oh_ref[rows, :]], axis=1)
        s = _dot_nt(qa_ref[...], ka)
        sc = [s[:, c * LANES:(c + 1) * LANES] for c in range(nlc)]
        if diagonal:
            tpos = i * tq + lax.broadcasted_iota(jnp.int32, (tq, LANES), 0)
            lane = kt * tk + lax.broadcasted_iota(jnp.int32, (tq, LANES), 1)
            for c in range(nlc):
                causal = lane + c * LANES <= tpos
                sc[c] = jnp.concatenate(
                    [jnp.where(causal, sc[c][h * tq:(h + 1) * tq], -MASK_BIG) for h in range(A_HPG)], axis=0)
        mx = sc[0]
        for c in range(1, nlc):
            mx = jnp.maximum(mx, sc[c])
        m_old = m_ref[...]
        m_new = jnp.maximum(m_old, jnp.max(mx, axis=-1, keepdims=True))
        alpha = jnp.exp2(m_old - m_new)
        pc = [jnp.exp2(sc[c] - m_new) for c in range(nlc)]
        ps = pc[0]
        for c in range(1, nlc):
            ps = ps + pc[c]
        l_ref[...] = alpha * l_ref[...] + ps
        p = jnp.concatenate([x.astype(BF16) for x in pc], axis=1)
        acc_ref[...] = alpha * acc_ref[...] + _dot(p, v_ref[rows, :])
        m_ref[...] = m_new

    def sub_tile(sub, carry):
        kt = kp * nsub + sub
        off = pl.multiple_of(sub * tk, tk)
        pl.when(kt < last_kt)(functools.partial(step, False, kt, off))
        pl.when(kt == last_kt)(functools.partial(step, True, kt, off))
        return carry

    all_interior = kp * nsub + nsub - 1 < last_kt

    @pl.when(all_interior)
    def _():
        for sub in range(nsub):
            step(False, kp * nsub + sub, sub * tk)

    @pl.when(jnp.logical_not(all_interior))
    def _():
        lax.fori_loop(0, nsub, sub_tile, 0)

    @pl.when(kp == pl.num_programs(2) - 1)
    def _():
        for h in range(A_HPG):
            l = jnp.sum(l_ref[h * tq:(h + 1) * tq], axis=-1, keepdims=True)
            o = acc_ref[h * tq:(h + 1) * tq] / l
            o_ref[:, h * A_DH:(h + 1) * A_DH] = o.astype(o_ref.dtype)


def _sel_attn(qa, kvsw, qsel, tq=512, tk=512, kb=2048):
    T = qa.shape[0]
    G = A_GROUPS
    tq = min(tq, T)
    tk = min(tk, T)
    kb = min(kb, T)
    qw = A_HPG * A_DH
    rows = A_HPG * tq

    def kv_map(col, per_group):
        def f(g, i, kp):
            return (jnp.minimum(kp, (i * tq + tq - 1) // kb), col // A_DH + g * per_group)
        return f

    onehot = jnp.asarray(np.arange(T)[:, None] // SEL_BLOCK == np.arange(NB_PAD)[None, :], dtype=BF16)

    return pl.pallas_call(
        functools.partial(_sel_attn_kernel, tk=tk),
        grid=(G, T // tq, T // kb),
        in_specs=[
            pl.BlockSpec((tq, qw), lambda g, i, kp: (i, g)),
            pl.BlockSpec((tq, NB_PAD), lambda g, i, kp: (i, g)),
            pl.BlockSpec((kb, A_DH), kv_map(KVSW_K_SEL, 1)),
            pl.BlockSpec((kb, A_DH), kv_map(KVSW_V_SEL, 1)),
            pl.BlockSpec((kb, NB_PAD), kv_map(0, 0)),
        ],
        out_specs=pl.BlockSpec((tq, qw), lambda g, i, kp: (i, g)),
        out_shape=jax.ShapeDtypeStruct((T, A_HEADS * A_DH), BF16),
        scratch_shapes=[
            pltpu.VMEM((rows, 2 * A_DH), BF16),
            pltpu.VMEM((rows, LANES), F32),
            pltpu.VMEM((rows, LANES), F32),
            pltpu.VMEM((rows, A_DH), F32),
        ],
        compiler_params=_cparams(("parallel", "parallel", "arbitrary")),
        name="sel_attn",
    )(qa, qsel, kvsw, kvsw, onehot)


def _gla_kernel(q_ref, k_ref, v_ref, a_ref, aw_ref, ab_ref, nw_ref, o_ref, st_ref):
    tc = q_ref.shape[0]
    C = B_CHUNK

    @pl.when(pl.program_id(0) == 0)
    def _():
        st_ref[...] = jnp.zeros(st_ref.shape, F32)

    logits = _dot(a_ref[...], aw_ref[...]) + ab_ref[...]
    log_a = jax.nn.log_sigmoid(logits) * (1.0 / B_GATE_TEMP)
    ri = lax.broadcasted_iota(jnp.int32, (C, C), 0)
    ci = lax.broadcasted_iota(jnp.int32, (C, C), 1)
    tril = ci <= ri
    tril_b = jnp.where(tril, 1.0, 0.0).astype(BF16)
    nw = nw_ref[...]
    for c in range(tc // C):
        sl = slice(c * C, (c + 1) * C)
        la_hi, la_lo = _split_bf16(log_a[sl])
        cum = _dot(tril_b, la_hi) + _dot(tril_b, la_lo)
        last = cum[C - 1:C, :]
        qc = q_ref[sl, :].astype(F32) * (B_DK ** -0.5)
        kc = k_ref[sl, :].astype(F32)
        q_dec = (qc * jnp.exp(cum)).astype(BF16)
        k_inv = (kc * jnp.exp(-cum)).astype(BF16)
        k_state = (kc * jnp.exp(last - cum)).astype(BF16)
        dec = jnp.exp(last)
        for h in range(B_HEADS):
            ks = slice(h * B_DK, (h + 1) * B_DK)
            vs = slice(h * B_DV, (h + 1) * B_DV)
            vc = v_ref[sl, vs]
            att = jnp.where(tril, _dot_nt(q_dec[:, ks], k_inv[:, ks]), 0.0)
            st = st_ref[h]
            o = _dot(att.astype(BF16), vc) + _dot_nt(q_dec[:, ks], st.astype(BF16))
            st_ref[h] = st * dec[:, ks] + _dot_tn(vc, k_state[:, ks])
            o = o * lax.rsqrt(jnp.mean(o * o, axis=-1, keepdims=True) + NORM_EPS) * nw
            o_ref[sl, vs] = o.astype(o_ref.dtype)


def _gla(cols2, alr, aw, ab, nw, tc=512):
    T = cols2.shape[0]
    tc = min(tc, T)
    kw = B_HEADS * B_DK
    vw = B_HEADS * B_DV
    return pl.pallas_call(
        _gla_kernel,
        grid=(T // tc,),
        in_specs=[
            pl.BlockSpec((tc, kw), lambda i: (i, R2_Q_B // kw)),
            pl.BlockSpec((tc, kw), lambda i: (i, R2_K_B // kw)),
            pl.BlockSpec((tc, vw), lambda i: (i, R2_V_B // vw)),
            pl.BlockSpec((tc, LANES), lambda i: (i, 0)),
            pl.BlockSpec((LANES, kw), lambda i: (0, 0)),
            pl.BlockSpec((1, kw), lambda i: (0, 0)),
            pl.BlockSpec((1, B_DV), lambda i: (0, 0)),
        ],
        out_specs=pl.BlockSpec((tc, vw), lambda i: (i, 0)),
        out_shape=jax.ShapeDtypeStruct((T, vw), BF16),
        scratch_shapes=[pltpu.VMEM((B_HEADS, B_DV, B_DK), F32)],
        compiler_params=_cparams(("arbitrary",)),
        name="gla",
    )(cols2, cols2, cols2, alr, aw, ab, nw)


def _silu(x):
    return x * jax.nn.sigmoid(x)


def _merge_a_kernel(ocmp_ref, osel_ref, owin_ref, gbr_ref, za_ref, ga_ref, pa_ref, y_ref, ua_ref):
    i = pl.program_id(0)

    @pl.when(i == 0)
    def _():
        ua_ref[1] = jnp.zeros(ua_ref.shape[1:], BF16)

    def step(slot):
        sg = jax.nn.sigmoid(gbr_ref[...].astype(F32))
        for hh in range(A_HEADS):
            cs = slice(hh * A_DH, (hh + 1) * A_DH)
            oa = (sg[:, 3 * hh:3 * hh + 1] * ocmp_ref[:, cs].astype(F32)
                  + sg[:, 3 * hh + 1:3 * hh + 2] * osel_ref[:, cs].astype(F32)
                  + sg[:, 3 * hh + 2:3 * hh + 3] * owin_ref[:, cs].astype(F32))
            ua_ref[slot, :, cs] = (oa * _silu(za_ref[:, cs].astype(F32))).astype(BF16)
        ya = _dot(ua_ref[1 - slot], pa_ref[...])
        y_ref[...] = (jax.nn.sigmoid(ga_ref[...].astype(F32)) * ya).astype(y_ref.dtype)

    for slot in (0, 1):
        pl.when(i % 2 == slot)(functools.partial(step, slot))


def _merge_b_kernel(ya_ref, ob_ref, zb_ref, gb_ref, pb_ref, y_ref):
    ub = (ob_ref[...].astype(F32) * _silu(zb_ref[...].astype(F32))).astype(BF16)
    yb = _dot(ub, pb_ref[...])
    y = ya_ref[...].astype(F32) + jax.nn.sigmoid(gb_ref[...].astype(F32)) * yb
    y_ref[...] = y.astype(y_ref.dtype)


def _merge(o_cmp, o_sel, o_win, gbr, cols2, o_b, cols3, pa, pb, layer, tm=512):
    T = gbr.shape[0]
    D = D_MODEL
    tm = min(tm, T)
    row = lambda c: pl.BlockSpec((tm, D), lambda i: (i, c))
    const = lambda: pl.BlockSpec((None, D, D), lambda i: (layer, 0, 0), pipeline_mode=pl.Buffered(1))
    n = T // tm
    cur = lambda c: pl.BlockSpec((tm, D), lambda i: (jnp.minimum(i, n - 1), c))
    prev = lambda c: pl.BlockSpec((tm, D), lambda i: (jnp.maximum(i - 1, 0), c))
    ya = pl.pallas_call(
        _merge_a_kernel,
        grid=(n + 1,),
        in_specs=[cur(0), cur(0), cur(0),
                  pl.BlockSpec((tm, LANES), lambda i: (jnp.minimum(i, n - 1), 0)),
                  cur(R2_Z_A // D), prev(R3_GATE_A // D), const()],
        out_specs=prev(0),
        out_shape=jax.ShapeDtypeStruct((T, D), BF16),
        scratch_shapes=[pltpu.VMEM((2, tm, D), BF16)],
        compiler_params=_cparams(("arbitrary",)),
        name="merge_a",
    )(o_cmp, o_sel, o_win, gbr, cols2, cols3, pa)
    return pl.pallas_call(
        _merge_b_kernel,
        grid=(T // tm,),
        in_specs=[row(0), row(0), row(R3_Z_B // D), row(R3_GATE_B // D), const()],
        out_specs=row(0),
        out_shape=jax.ShapeDtypeStruct((T, D), BF16),
        compiler_params=_cparams(("parallel",)),
        name="merge_b",
    )(ya, o_b, cols3, cols3, pb)


def _outproj_kernel(x_ref, y_ref, w_ref, nw_ref, *out_refs, last):
    xn = x_ref[...] + _dot(y_ref[...], w_ref[...])
    ms = jnp.mean(xn * xn, axis=-1, keepdims=True)
    hn = xn * lax.rsqrt(ms + NORM_EPS) * nw_ref[...]
    if last:
        out_refs[0][...] = hn
    else:
        out_refs[0][...] = xn
        out_refs[1][...] = hn.astype(BF16)


def _outproj(x, y, w, layer, nw, last, tm=512):
    T, D = x.shape
    tm = min(tm, T)
    row = pl.BlockSpec((tm, D), lambda i: (i, 0))
    if last:
        out_specs, out_shape = row, jax.ShapeDtypeStruct((T, D), F32)
    else:
        out_specs = [row, row]
        out_shape = [jax.ShapeDtypeStruct((T, D), F32), jax.ShapeDtypeStruct((T, D), BF16)]
    return pl.pallas_call(
        functools.partial(_outproj_kernel, last=last),
        grid=(T // tm,),
        in_specs=[
            row, row,
            pl.BlockSpec((None, D, D), lambda i: (layer, 0, 0), pipeline_mode=pl.Buffered(1)),
            pl.BlockSpec((1, D), lambda i: (0, 0)),
        ],
        out_specs=out_specs,
        out_shape=out_shape,
        compiler_params=_cparams(("parallel",)),
        name="outproj",
    )(x, y, w, nw)


def kernel(x, norm_w, w_in, cmp_pe, cmp_w1, cmp_b1, cmp_w2, cmp_b2, gla_alpha_w, gla_alpha_b, gla_norm_w,
           p_a, p_b, w_out, final_norm_w):
    B, T, D = x.shape
    L = norm_w.shape[0]
    assert B == 1 and D == D_MODEL and T % 512 == 0 and T // SEL_BLOCK <= NB_PAD
    assert w_in.shape[2] == SRC_END
    w_t = jnp.swapaxes(w_in, 1, 2)

    w1_b = cmp_w1.astype(BF16)
    w2_b = cmp_w2.astype(BF16)
    b1_r = cmp_b1.reshape(L, 2, 1, CMP_HIDDEN)
    b2_r = cmp_b2.reshape(L, 2, 1, A_DH)
    aw_p = jnp.pad(gla_alpha_w, ((0, 0), (0, LANES - B_RANK), (0, 0))).astype(BF16)
    ab_r = gla_alpha_b.reshape(L, 1, B_HEADS * B_DK)
    gnw_r = gla_norm_w.reshape(L, 1, B_DV)
    pa_b = p_a.astype(BF16)
    pb_b = p_b.astype(BF16)
    wo_b = w_out.astype(BF16)
    nw_next = jnp.concatenate([norm_w[1:], final_norm_w[None, :]], axis=0).reshape(L, 1, D)

    xs = x.reshape(T, D)
    h = _rmsnorm(xs, norm_w[0].reshape(1, D))
    for l in range(L):
        qa = _proj(h, w_t, l, SRC_Q_A, A_HEADS * A_DH, 512, BF16, scale=A_DH ** -0.5 * LOG2E)
        kv_cmp = _proj(h, w_t, l, SRC_KV_CMP, 2 * A_KVW, 512, F32)
        kvsw = _proj(h, w_t, l, SRC_KV_SW, 4 * A_KVW, 512, BF16)
        gbr = _proj(h, w_t, l, SRC_G_BR, LANES, LANES, BF16, valid=G_BR_COLS)
        cols2 = _proj(h, w_t, l, SRC_R2, SRC_A_LR - SRC_R2, 512, BF16)
        alr = _proj(h, w_t, l, SRC_A_LR, LANES, LANES, BF16, valid=B_RANK)
        cols3 = _proj(h, w_t, l, SRC_R3, SRC_END - SRC_R3, 512, BF16)
        kvc = _compress(kv_cmp, cmp_pe[l], w1_b[l], b1_r[l], w2_b[l], b2_r[l])
        o_cmp, qsel, o_win = _cmpwin_attn(qa, kvc, kvsw)
        o_sel = _sel_attn(qa, kvsw, qsel)
        o_b = _gla(cols2, alr, aw_p[l], ab_r[l], gnw_r[l])
        y = _merge(o_cmp, o_sel, o_win, gbr, cols2, o_b, cols3, pa_b, pb_b, l)
        if l == L - 1:
            xs = _outproj(xs, y, wo_b, l, nw_next[l], last=True)
        else:
            xs, h = _outproj(xs, y, wo_b, l, nw_next[l], last=False)
    return xs.reshape(B, T, D)
```

```python
import functools

import jax
import jax.numpy as jnp
import numpy as np
from jax import lax
from jax.experimental import pallas as pl
from jax.experimental.pallas import tpu as pltpu

F32 = jnp.float32
BF16 = jnp.bfloat16

D_MODEL = 2048
A_HEADS = 16
A_GROUPS = 4
A_HPG = A_HEADS // A_GROUPS
A_DH = 128
A_KVW = A_GROUPS * A_DH
CMP_BLOCK = 32
CMP_STRIDE = 16
CMP_HIDDEN = 256
SEL_BLOCK = 64
SEL_SHIFT = 6
SEL_TOP_N = 16
WIN_SIZE = 512
B_HEADS = 4
B_DK = 256
B_DV = 512
B_RANK = 16
B_GATE_TEMP = 16.0
B_CHUNK = 64
NORM_EPS = 1e-6

LANES = 128
NB_PAD = LANES
CMP_WIDTH_CLASSES = 4
MASK_BIG = 2.0 ** 100
LOG2E = float(np.log2(np.e))

SRC_Q_A = 0
SRC_KV_CMP = 2048
SRC_KV_SW = 3072
SRC_G_BR = 5120
SRC_R2 = 5168
SRC_A_LR = 11312
SRC_R3 = 11328
SRC_END = 17472
G_BR_COLS = 48
KVSW_K_SEL, KVSW_V_SEL, KVSW_K_WIN, KVSW_V_WIN = 0, 512, 1024, 1536
R2_Z_A, R2_Q_B, R2_K_B, R2_V_B = 0, 2048, 3072, 4096
R3_Z_B, R3_GATE_A, R3_GATE_B = 0, 2048, 4096

VMEM_LIMIT = 56 * 1024 * 1024


def _cparams(sem):
    return pltpu.CompilerParams(dimension_semantics=sem, vmem_limit_bytes=VMEM_LIMIT)


def _dot(a, b):
    return jnp.dot(a, b, preferred_element_type=F32)


def _dot_nt(a, b):
    return lax.dot_general(a, b, (((1,), (1,)), ((), ())), preferred_element_type=F32)


def _dot_tn(a, b):
    return lax.dot_general(a, b, (((0,), (0,)), ((), ())), preferred_element_type=F32)


def _split_bf16(x):
    hi = x.astype(BF16)
    lo = (x - hi.astype(F32)).astype(BF16)
    return hi, lo


def _rmsnorm_kernel(x_ref, w_ref, o_ref):
    x = x_ref[...]
    ms = jnp.mean(x * x, axis=-1, keepdims=True)
    o_ref[...] = (x * lax.rsqrt(ms + NORM_EPS) * w_ref[...]).astype(o_ref.dtype)


def _rmsnorm(x, w, tm=512):
    T, D = x.shape
    tm = min(tm, T)
    return pl.pallas_call(
        _rmsnorm_kernel,
        grid=(T // tm,),
        in_specs=[pl.BlockSpec((tm, D), lambda i: (i, 0)), pl.BlockSpec((1, D), lambda i: (0, 0))],
        out_specs=pl.BlockSpec((tm, D), lambda i: (i, 0)),
        out_shape=jax.ShapeDtypeStruct((T, D), BF16),
        compiler_params=_cparams(("parallel",)),
        name="rmsnorm",
    )(x, w)


def _proj_kernel(h_ref, *refs, shift, valid, scale):
    if shift:
        wa_ref, wb_ref, o_ref, wc_ref = refs
    else:
        wa_ref, o_ref, wc_ref = refs
    tn = o_ref.shape[1]

    @pl.when(pl.program_id(1) == 0)
    def _():
        if shift:
            wc_ref[0:tn - shift, :] = wa_ref[shift:tn, :].astype(BF16)
            wc_ref[tn - shift:tn, :] = wb_ref[0:shift, :].astype(BF16)
        else:
            wc_ref[...] = wa_ref[...].astype(BF16)
        if valid < tn:
            wc_ref[valid:tn, :] = jnp.zeros((tn - valid, wc_ref.shape[1]), BF16)

    r = _dot_nt(h_ref[...], wc_ref[...])
    if scale != 1.0:
        r = r * scale
    o_ref[...] = r.astype(o_ref.dtype)


def _proj(h, w_t, layer, col0, width, tn, out_dtype, scale=1.0, valid=None, tm=2048):
    T, D = h.shape
    tm = min(tm, T)
    shift = col0 % LANES
    a0 = col0 - shift
    valid = tn if valid is None else valid
    assert a0 % tn == 0 and width % tn == 0 and (valid == tn or width == tn)
    assert shift % 16 == 0 and valid % 16 == 0
    in_specs = [
        pl.BlockSpec((tm, D), lambda j, i: (i, 0)),
        pl.BlockSpec((None, tn, D), lambda j, i: (layer, a0 // tn + j, 0)),
    ]
    args = [h, w_t]
    if shift:
        in_specs.append(pl.BlockSpec((None, LANES, D), lambda j, i: (layer, (a0 + (j + 1) * tn) // LANES, 0)))
        args.append(w_t)
    return pl.pallas_call(
        functools.partial(_proj_kernel, shift=shift, valid=valid, scale=scale),
        grid=(width // tn, T // tm),
        in_specs=in_specs,
        out_specs=pl.BlockSpec((tm, tn), lambda j, i: (i, j)),
        out_shape=jax.ShapeDtypeStruct((T, width), out_dtype),
        scratch_shapes=[pltpu.VMEM((tn, D), BF16)],
        compiler_params=_cparams(("arbitrary", "arbitrary")),
        name="proj",
    )(*args)


def _gelu_tanh(x):
    c = np.float32(np.sqrt(2.0 / np.pi))
    return 0.5 * x * (1.0 + jnp.tanh(c * (x + np.float32(0.044715) * (x * x * x))))


def _compress_kernel(x_ref, pe_ref, w1_ref, b1_ref, w2_ref, b2_ref, o_ref, xlo_ref, xhi_ref):
    nch = o_ref.shape[2]
    half = CMP_STRIDE * A_DH
    for p in range(CMP_STRIDE):
        tok = x_ref[pl.ds(p, nch, stride=CMP_STRIDE), :]
        cs = slice(p * A_DH, (p + 1) * A_DH)
        xlo_ref[:, cs] = (tok + pe_ref[0, p:p + 1, :]).astype(BF16)
        xhi_ref[:, cs] = (tok + pe_ref[0, CMP_STRIDE + p:CMP_STRIDE + p + 1, :]).astype(BF16)
    a = _dot(xlo_ref[...], w1_ref[0, 0:half, :])
    b = _dot(xhi_ref[...], w1_ref[0, half:2 * half, :])
    b_next = pltpu.roll(b, nch - 1, axis=0)
    row = lax.broadcasted_iota(jnp.int32, b.shape, 0)
    b_next = jnp.where(row < nch - 1, b_next, 0.0)
    hid = _gelu_tanh(a + b_next + b1_ref[0])
    out = _dot(hid.astype(BF16), w2_ref[0]) + b2_ref[0]
    o_ref[0, 0] = out.astype(o_ref.dtype)


def _compress(kv, pe, w1, b1, w2, b2):
    T = kv.shape[0]
    G = A_GROUPS
    nch = T // CMP_STRIDE
    width = CMP_STRIDE * A_DH
    return pl.pallas_call(
        _compress_kernel,
        grid=(2, G),
        in_specs=[
            pl.BlockSpec((T, A_DH), lambda s, g: (0, s * G + g)),
            pl.BlockSpec((1, CMP_BLOCK, A_DH), lambda s, g: (s, 0, 0)),
            pl.BlockSpec((1, 2 * width, CMP_HIDDEN), lambda s, g: (s, 0, 0)),
            pl.BlockSpec((1, 1, CMP_HIDDEN), lambda s, g: (s, 0, 0)),
            pl.BlockSpec((1, CMP_HIDDEN, A_DH), lambda s, g: (s, 0, 0)),
            pl.BlockSpec((1, 1, A_DH), lambda s, g: (s, 0, 0)),
        ],
        out_specs=pl.BlockSpec((1, 1, nch, A_DH), lambda s, g: (s, g, 0, 0)),
        out_shape=jax.ShapeDtypeStruct((2, G, nch, A_DH), BF16),
        scratch_shapes=[pltpu.VMEM((nch, width), BF16), pltpu.VMEM((nch, width), BF16)],
        compiler_params=_cparams(("parallel", "parallel")),
        name="compress",
    )(kv, pe, w1, b1, w2, b2)


def _stack_heads(q_ref, qs_ref):
    tq = q_ref.shape[0]
    for h in range(A_HPG):
        qs_ref[h * tq:(h + 1) * tq, :] = q_ref[:, h * A_DH:(h + 1) * A_DH]


def _masked_chunks(s, masks, tq):
    out = []
    for c, mk in enumerate(masks):
        sc = s[:, c * LANES:(c + 1) * LANES]
        out.append(jnp.concatenate(
            [jnp.where(mk, sc[h * tq:(h + 1) * tq], -jnp.inf) for h in range(A_HPG)], axis=0))
    return out


def _chunk_softmax(sc):
    mx = sc[0]
    for x in sc[1:]:
        mx = jnp.maximum(mx, x)
    m = jnp.max(mx, axis=-1, keepdims=True)
    m = jnp.where(m > -jnp.inf, m, 0.0)
    pc = [jnp.exp2(x - m) for x in sc]
    ps = pc[0]
    for x in pc[1:]:
        ps = ps + x
    return pc, jnp.sum(ps, axis=-1, keepdims=True)


def _cmpwin_kernel(q_ref, kc_ref, vc_ref, k0_ref, k1_ref, k2_ref, v0_ref, v1_ref, v2_ref,
                   ocmp_ref, qsel_ref, owin_ref, qs_ref, imp_ref, p_ref, inv_ref):
    i = pl.program_id(1)

    @pl.when(i == 0)
    def _():
        imp_ref[1] = jnp.zeros(imp_ref.shape[1:], F32)
        p_ref[1] = jnp.zeros(p_ref.shape[1:], BF16)
        inv_ref[1] = jnp.ones(inv_ref.shape[1:], F32)

    def step(slot, nck):
        _stack_heads(q_ref, qs_ref)
        _cmp_attn_step(kc_ref, vc_ref, ocmp_ref, qsel_ref, qs_ref, imp_ref, slot, nck)
        _win_attn_step(k0_ref, k1_ref, k2_ref, v0_ref, v1_ref, v2_ref, owin_ref, qs_ref, p_ref, inv_ref, slot)

    tq = q_ref.shape[0]
    nch = kc_ref.shape[2]
    t_last = jnp.minimum(i, pl.num_programs(1) - 2) * tq + tq - 1
    need = (t_last - (CMP_BLOCK - 1)) // CMP_STRIDE + 1
    nw = min(CMP_WIDTH_CLASSES, nch // LANES)
    widths = [nch * (w + 1) // nw for w in range(nw)] if nch % (nw * LANES) == 0 else [nch]
    for slot in (0, 1):
        for wi, nck in enumerate(widths):
            fits = need <= nck
            if wi > 0:
                fits = fits & (need > widths[wi - 1])
            elif len(widths) == 1:
                fits = True
            pl.when((i % 2 == slot) & fits)(functools.partial(step, slot, nck))


def _cmp_attn_step(kc_ref, vc_ref, o_ref, qsel_ref, qs_ref, imp_ref, slot, nch):
    tq = qsel_ref.shape[0]
    i = pl.program_id(1)
    t0 = jnp.minimum(i, pl.num_programs(1) - 2) * tq
    s = _dot_nt(qs_ref[...], kc_ref[0, 0, 0:nch, :])
    tpos = t0 + lax.broadcasted_iota(jnp.int32, (tq, LANES), 0)
    lane = lax.broadcasted_iota(jnp.int32, (tq, LANES), 1)
    masks = [(lane + c * LANES) * CMP_STRIDE + (CMP_BLOCK - 1) <= tpos for c in range(nch // LANES)]
    pc, denom = _chunk_softmax(_masked_chunks(s, masks, tq))
    inv = 1.0 / jnp.where(denom > 0, denom, 1.0)
    o = _dot(jnp.concatenate([x.astype(BF16) for x in pc], axis=1), vc_ref[0, 0, 0:nch, :]) * inv
    for h in range(A_HPG):
        o_ref[:, h * A_DH:(h + 1) * A_DH] = o[h * tq:(h + 1) * tq].astype(o_ref.dtype)
    pn = [x * inv for x in pc]
    psum = jnp.concatenate(
        [sum(x[h * tq:(h + 1) * tq] for h in range(A_HPG)) for x in pn], axis=1)

    jj = lax.broadcasted_iota(jnp.int32, (NB_PAD, nch), 0)
    cs = lax.broadcasted_iota(jnp.int32, (NB_PAD, nch), 1) * CMP_STRIDE
    ov = ((cs <= jj * SEL_BLOCK + SEL_BLOCK - 1) & (cs + CMP_BLOCK - 1 >= jj * SEL_BLOCK))
    ov = jnp.where(ov, 1.0, 0.0).astype(BF16)
    p_hi, p_lo = _split_bf16(psum)
    imp_ref[slot] = _dot_nt(ov, p_hi) + _dot_nt(ov, p_lo)

    nr = min(NB_PAD, nch * CMP_STRIDE // SEL_BLOCK)
    pslc = imp_ref[1 - slot, 0:nr, :]
    jf = lax.broadcasted_iota(jnp.int32, (nr, tq), 0)
    tl = jnp.maximum(i - 1, 0) * tq + lax.broadcasted_iota(jnp.int32, (nr, tq), 1)
    cur = jnp.right_shift(tl, SEL_SHIFT)
    valid = jf <= cur
    forced = (jf == 0) | (jf == cur) | (jf == cur - 1)
    score = jnp.where(valid & ~forced, pslc, -jnp.inf)
    jff = jf.astype(F32)
    for _ in range(SEL_TOP_N - 3):
        mx = jnp.max(score, axis=0, keepdims=True)
        first = jnp.min(jnp.where(score == mx, jff, float(NB_PAD)), axis=0, keepdims=True)
        score = jnp.where(jff == first, -jnp.inf, score)
    neg = jnp.where(valid & (score == -jnp.inf), 0.0, -MASK_BIG)
    if nr < NB_PAD:
        neg = jnp.concatenate([neg, jnp.full((NB_PAD - nr, tq), -MASK_BIG, F32)], axis=0)
    qsel_ref[...] = neg.T.astype(qsel_ref.dtype)


def _win_attn_step(k0_ref, k1_ref, k2_ref, v0_ref, v1_ref, v2_ref, o_ref, qs_ref, p_ref, inv_ref, slot):
    tq = o_ref.shape[0]
    i = pl.program_id(1)
    ia = jnp.minimum(i, pl.num_programs(1) - 2)
    k = jnp.concatenate([k0_ref[...], k1_ref[...], k2_ref[...]], axis=0)
    s = _dot_nt(qs_ref[...], k)
    r = lax.broadcasted_iota(jnp.int32, (tq, LANES), 0)
    lane = lax.broadcasted_iota(jnp.int32, (tq, LANES), 1)
    masks = []
    for c in range(3 * tq // LANES):
        col = lane + c * LANES
        rel = 2 * tq + r - col
        masks.append((rel >= 0) & (rel < WIN_SIZE) & ((ia - 2) * tq + col >= 0))
    pc, denom = _chunk_softmax(_masked_chunks(s, masks, tq))

    v = jnp.concatenate([v0_ref[...], v1_ref[...], v2_ref[...]], axis=0)
    o = _dot(p_ref[1 - slot], v) * inv_ref[1 - slot]
    for h in range(A_HPG):
        o_ref[:, h * A_DH:(h + 1) * A_DH] = o[h * tq:(h + 1) * tq].astype(o_ref.dtype)

    p_ref[slot] = jnp.concatenate([x.astype(BF16) for x in pc], axis=1)
    inv_ref[slot] = jnp.broadcast_to(1.0 / denom, inv_ref.shape[1:])


def _cmpwin_attn(qa, kvc, kvsw, tq=256):
    T = qa.shape[0]
    G = A_GROUPS
    nch = kvc.shape[2]
    tq = min(tq, T)
    assert WIN_SIZE <= 2 * tq
    qw = A_HPG * A_DH
    n = T // tq
    cur = lambda g, i: (jnp.minimum(i, n - 1), g)
    prev = lambda g, i: (jnp.maximum(i - 1, 0), g)

    def kv_spec(col, back, lag):
        def f(g, i):
            tile = jnp.clip(i - lag, 0, n - 1)
            return (jnp.maximum(tile - back, 0), col // A_DH + g)
        return pl.BlockSpec((tq, A_DH), f)

    return pl.pallas_call(
        _cmpwin_kernel,
        grid=(G, n + 1),
        in_specs=[
            pl.BlockSpec((tq, qw), cur),
            pl.BlockSpec((1, 1, nch, A_DH), lambda g, i: (0, g, 0, 0)),
            pl.BlockSpec((1, 1, nch, A_DH), lambda g, i: (1, g, 0, 0)),
            kv_spec(KVSW_K_WIN, 2, 0), kv_spec(KVSW_K_WIN, 1, 0), kv_spec(KVSW_K_WIN, 0, 0),
            kv_spec(KVSW_V_WIN, 2, 1), kv_spec(KVSW_V_WIN, 1, 1), kv_spec(KVSW_V_WIN, 0, 1),
        ],
        out_specs=[
            pl.BlockSpec((tq, qw), cur),
            pl.BlockSpec((tq, NB_PAD), prev),
            pl.BlockSpec((tq, qw), prev),
        ],
        out_shape=[
            jax.ShapeDtypeStruct((T, A_HEADS * A_DH), BF16),
            jax.ShapeDtypeStruct((T, G * NB_PAD), BF16),
            jax.ShapeDtypeStruct((T, A_HEADS * A_DH), BF16),
        ],
        scratch_shapes=[pltpu.VMEM((A_HPG * tq, A_DH), BF16),
                        pltpu.VMEM((2, NB_PAD, tq), F32),
                        pltpu.VMEM((2, A_HPG * tq, 3 * tq), BF16),
                        pltpu.VMEM((2, A_HPG * tq, LANES), F32)],
        compiler_params=_cparams(("parallel", "arbitrary")),
        name="cmpwin_attn",
    )(qa, kvc, kvc, kvsw, kvsw, kvsw, kvsw, kvsw, kvsw)


def _sel_attn_kernel(q_ref, qsel_ref, k_ref, v_ref, oh_ref, o_ref, qa_ref, m_ref, l_ref, acc_ref, *, tk):
    tq = q_ref.shape[0]
    nsub = k_ref.shape[0] // tk
    i = pl.program_id(1)
    kp = pl.program_id(2)
    last_kt = (i * tq + tq - 1) // tk
    nlc = tk // LANES

    @pl.when(kp == 0)
    def _():
        for h in range(A_HPG):
            qa_ref[h * tq:(h + 1) * tq, 0:A_DH] = q_ref[:, h * A_DH:(h + 1) * A_DH]
            qa_ref[h * tq:(h + 1) * tq, A_DH:2 * A_DH] = qsel_ref[...]
        m_ref[...] = jnp.full(m_ref.shape, -jnp.inf, F32)
        l_ref[...] = jnp.zeros(l_ref.shape, F32)
        acc_ref[...] = jnp.zeros(acc_ref.shape, F32)

    def step(diagonal, kt, off):
        rows = pl.ds(off, tk)
        ka = jnp.concatenate([k_ref[rows, :], oh_ref[rows, :]], axis=1)
        s = _dot_nt(qa_ref[...], ka)
        sc = [s[:, c * LANES:(c + 1) * LANES] for c in range(nlc)]
        if diagonal:
            tpos = i * tq + lax.broadcasted_iota(jnp.int32, (tq, LANES), 0)
            lane = kt * tk + lax.broadcasted_iota(jnp.int32, (tq, LANES), 1)
            for c in range(nlc):
                causal = lane + c * LANES <= tpos
                sc[c] = jnp.concatenate(
                    [jnp.where(causal, sc[c][h * tq:(h + 1) * tq], -MASK_BIG) for h in range(A_HPG)], axis=0)
        mx = sc[0]
        for c in range(1, nlc):
            mx = jnp.maximum(mx, sc[c])
        m_old = m_ref[...]
        m_new = jnp.maximum(m_old, jnp.max(mx, axis=-1, keepdims=True))
        alpha = jnp.exp2(m_old - m_new)
        pc = [jnp.exp2(sc[c] - m_new) for c in range(nlc)]
        ps = pc[0]
        for c in range(1, nlc):
            ps = ps + pc[c]
        l_ref[...] = alpha * l_ref[...] + ps
        p = jnp.concatenate([x.astype(BF16) for x in pc], axis=1)
        acc_ref[...] = alpha * acc_ref[...] + _dot(p, v_ref[rows, :])
        m_ref[...] = m_new

    def sub_tile(sub, carry):
        kt = kp * nsub + sub
        off = pl.multiple_of(sub * tk, tk)
        pl.when(kt < last_kt)(functools.partial(step, False, kt, off))
        pl.when(kt == last_kt)(functools.partial(step, True, kt, off))
        return carry

    all_interior = kp * nsub + nsub - 1 < last_kt

    @pl.when(all_interior)
    def _():
        for sub in range(nsub):
            step(False, kp * nsub + sub, sub * tk)

    @pl.when(jnp.logical_not(all_interior))
    def _():
        lax.fori_loop(0, nsub, sub_tile, 0)

    @pl.when(kp == pl.num_programs(2) - 1)
    def _():
        for h in range(A_HPG):
            l = jnp.sum(l_ref[h * tq:(h + 1) * tq], axis=-1, keepdims=True)
            o = acc_ref[h * tq:(h + 1) * tq] / l
            o_ref[:, h * A_DH:(h + 1) * A_DH] = o.astype(o_ref.dtype)


def _sel_attn(qa, kvsw, qsel, tq=512, tk=512, kb=2048):
    T = qa.shape[0]
    G = A_GROUPS
    tq = min(tq, T)
    tk = min(tk, T)
    kb = min(kb, T)
    qw = A_HPG * A_DH
    rows = A_HPG * tq

    def kv_map(col, per_group):
        def f(g, i, kp):
            return (jnp.minimum(kp, (i * tq + tq - 1) // kb), col // A_DH + g * per_group)
        return f

    onehot = jnp.asarray(np.arange(T)[:, None] // SEL_BLOCK == np.arange(NB_PAD)[None, :], dtype=BF16)

    return pl.pallas_call(
        functools.partial(_sel_attn_kernel, tk=tk),
        grid=(G, T // tq, T // kb),
        in_specs=[
            pl.BlockSpec((tq, qw), lambda g, i, kp: (i, g)),
            pl.BlockSpec((tq, NB_PAD), lambda g, i, kp: (i, g)),
            pl.BlockSpec((kb, A_DH), kv_map(KVSW_K_SEL, 1)),
            pl.BlockSpec((kb, A_DH), kv_map(KVSW_V_SEL, 1)),
            pl.BlockSpec((kb, NB_PAD), kv_map(0, 0)),
        ],
        out_specs=pl.BlockSpec((tq, qw), lambda g, i, kp: (i, g)),
        out_shape=jax.ShapeDtypeStruct((T, A_HEADS * A_DH), BF16),
        scratch_shapes=[
            pltpu.VMEM((rows, 2 * A_DH), BF16),
            pltpu.VMEM((rows, LANES), F32),
            pltpu.VMEM((rows, LANES), F32),
            pltpu.VMEM((rows, A_DH), F32),
        ],
        compiler_params=_cparams(("parallel", "parallel", "arbitrary")),
        name="sel_attn",
    )(qa, qsel, kvsw, kvsw, onehot)


def _gla_kernel(q_ref, k_ref, v_ref, a_ref, aw_ref, ab_ref, nw_ref, o_ref, st_ref):
    tc = q_ref.shape[0]
    C = B_CHUNK

    @pl.when(pl.program_id(0) == 0)
    def _():
        st_ref[...] = jnp.zeros(st_ref.shape, F32)

    logits = _dot(a_ref[...], aw_ref[...]) + ab_ref[...]
    log_a = jax.nn.log_sigmoid(logits) * (1.0 / B_GATE_TEMP)
    ri = lax.broadcasted_iota(jnp.int32, (C, C), 0)
    ci = lax.broadcasted_iota(jnp.int32, (C, C), 1)
    tril = ci <= ri
    tril_b = jnp.where(tril, 1.0, 0.0).astype(BF16)
    nw = nw_ref[...]
    for c in range(tc // C):
        sl = slice(c * C, (c + 1) * C)
        la_hi, la_lo = _split_bf16(log_a[sl])
        cum = _dot(tril_b, la_hi) + _dot(tril_b, la_lo)
        last = cum[C - 1:C, :]
        qc = q_ref[sl, :].astype(F32) * (B_DK ** -0.5)
        kc = k_ref[sl, :].astype(F32)
        q_dec = (qc * jnp.exp(cum)).astype(BF16)
        k_inv = (kc * jnp.exp(-cum)).astype(BF16)
        k_state = (kc * jnp.exp(last - cum)).astype(BF16)
        dec = jnp.exp(last)
        for h in range(B_HEADS):
            ks = slice(h * B_DK, (h + 1) * B_DK)
            vs = slice(h * B_DV, (h + 1) * B_DV)
            vc = v_ref[sl, vs]
            att = jnp.where(tril, _dot_nt(q_dec[:, ks], k_inv[:, ks]), 0.0)
            st = st_ref[h]
            o = _dot(att.astype(BF16), vc) + _dot_nt(q_dec[:, ks], st.astype(BF16))
            st_ref[h] = st * dec[:, ks] + _dot_tn(vc, k_state[:, ks])
            o = o * lax.rsqrt(jnp.mean(o * o, axis=-1, keepdims=True) + NORM_EPS) * nw
            o_ref[sl, vs] = o.astype(o_ref.dtype)


def _gla(cols2, alr, aw, ab, nw, tc=512):
    T = cols2.shape[0]
    tc = min(tc, T)
    kw = B_HEADS * B_DK
    vw = B_HEADS * B_DV
    return pl.pallas_call(
        _gla_kernel,
        grid=(T // tc,),
        in_specs=[
            pl.BlockSpec((tc, kw), lambda i: (i, R2_Q_B // kw)),
            pl.BlockSpec((tc, kw), lambda i: (i, R2_K_B // kw)),
            pl.BlockSpec((tc, vw), lambda i: (i, R2_V_B // vw)),
            pl.BlockSpec((tc, LANES), lambda i: (i, 0)),
            pl.BlockSpec((LANES, kw), lambda i: (0, 0)),
            pl.BlockSpec((1, kw), lambda i: (0, 0)),
            pl.BlockSpec((1, B_DV), lambda i: (0, 0)),
        ],
        out_specs=pl.BlockSpec((tc, vw), lambda i: (i, 0)),
        out_shape=jax.ShapeDtypeStruct((T, vw), BF16),
        scratch_shapes=[pltpu.VMEM((B_HEADS, B_DV, B_DK), F32)],
        compiler_params=_cparams(("arbitrary",)),
        name="gla",
    )(cols2, cols2, cols2, alr, aw, ab, nw)


def _silu(x):
    return x * jax.nn.sigmoid(x)


def _merge_a_kernel(ocmp_ref, osel_ref, owin_ref, gbr_ref, za_ref, ga_ref, pa_ref, y_ref, ua_ref):
    i = pl.program_id(0)

    @pl.when(i == 0)
    def _():
        ua_ref[1] = jnp.zeros(ua_ref.shape[1:], BF16)

    def step(slot):
        sg = jax.nn.sigmoid(gbr_ref[...].astype(F32))
        for hh in range(A_HEADS):
            cs = slice(hh * A_DH, (hh + 1) * A_DH)
            oa = (sg[:, 3 * hh:3 * hh + 1] * ocmp_ref[:, cs].astype(F32)
                  + sg[:, 3 * hh + 1:3 * hh + 2] * osel_ref[:, cs].astype(F32)
                  + sg[:, 3 * hh + 2:3 * hh + 3] * owin_ref[:, cs].astype(F32))
            ua_ref[slot, :, cs] = (oa * _silu(za_ref[:, cs].astype(F32))).astype(BF16)
        ya = _dot(ua_ref[1 - slot], pa_ref[...])
        y_ref[...] = (jax.nn.sigmoid(ga_ref[...].astype(F32)) * ya).astype(y_ref.dtype)

    for slot in (0, 1):
        pl.when(i % 2 == slot)(functools.partial(step, slot))


def _merge_b_kernel(ya_ref, ob_ref, zb_ref, gb_ref, pb_ref, y_ref):
    ub = (ob_ref[...].astype(F32) * _silu(zb_ref[...].astype(F32))).astype(BF16)
    yb = _dot(ub, pb_ref[...])
    y = ya_ref[...].astype(F32) + jax.nn.sigmoid(gb_ref[...].astype(F32)) * yb
    y_ref[...] = y.astype(y_ref.dtype)


def _merge(o_cmp, o_sel, o_win, gbr, cols2, o_b, cols3, pa, pb, layer, tm=512):
    T = gbr.shape[0]
    D = D_MODEL
    tm = min(tm, T)
    row = lambda c: pl.BlockSpec((tm, D), lambda i: (i, c))
    const = lambda: pl.BlockSpec((None, D, D), lambda i: (layer, 0, 0), pipeline_mode=pl.Buffered(1))
    n = T // tm
    cur = lambda c: pl.BlockSpec((tm, D), lambda i: (jnp.minimum(i, n - 1), c))
    prev = lambda c: pl.BlockSpec((tm, D), lambda i: (jnp.maximum(i - 1, 0), c))
    ya = pl.pallas_call(
        _merge_a_kernel,
        grid=(n + 1,),
        in_specs=[cur(0), cur(0), cur(0),
                  pl.BlockSpec((tm, LANES), lambda i: (jnp.minimum(i, n - 1), 0)),
                  cur(R2_Z_A // D), prev(R3_GATE_A // D), const()],
        out_specs=prev(0),
        out_shape=jax.ShapeDtypeStruct((T, D), BF16),
        scratch_shapes=[pltpu.VMEM((2, tm, D), BF16)],
        compiler_params=_cparams(("arbitrary",)),
        name="merge_a",
    )(o_cmp, o_sel, o_win, gbr, cols2, cols3, pa)
    return pl.pallas_call(
        _merge_b_kernel,
        grid=(T // tm,),
        in_specs=[row(0), row(0), row(R3_Z_B // D), row(R3_GATE_B // D), const()],
        out_specs=row(0),
        out_shape=jax.ShapeDtypeStruct((T, D), BF16),
        compiler_params=_cparams(("parallel",)),
        name="merge_b",
    )(ya, o_b, cols3, cols3, pb)


def _outproj_kernel(x_ref, y_ref, w_ref, nw_ref, *out_refs, last):
    xn = x_ref[...] + _dot(y_ref[...], w_ref[...])
    ms = jnp.mean(xn * xn, axis=-1, keepdims=True)
    hn = xn * lax.rsqrt(ms + NORM_EPS) * nw_ref[...]
    if last:
        out_refs[0][...] = hn
    else:
        out_refs[0][...] = xn
        out_refs[1][...] = hn.astype(BF16)


def _outproj(x, y, w, layer, nw, last, tm=512):
    T, D = x.shape
    tm = min(tm, T)
    row = pl.BlockSpec((tm, D), lambda i: (i, 0))
    if last:
        out_specs, out_shape = row, jax.ShapeDtypeStruct((T, D), F32)
    else:
        out_specs = [row, row]
        out_shape = [jax.ShapeDtypeStruct((T, D), F32), jax.ShapeDtypeStruct((T, D), BF16)]
    return pl.pallas_call(
        functools.partial(_outproj_kernel, last=last),
        grid=(T // tm,),
        in_specs=[
            row, row,
            pl.BlockSpec((None, D, D), lambda i: (layer, 0, 0), pipeline_mode=pl.Buffered(1)),
            pl.BlockSpec((1, D), lambda i: (0, 0)),
        ],
        out_specs=out_specs,
        out_shape=out_shape,
        compiler_params=_cparams(("parallel",)),
        name="outproj",
    )(x, y, w, nw)


def kernel(x, norm_w, w_in, cmp_pe, cmp_w1, cmp_b1, cmp_w2, cmp_b2, gla_alpha_w, gla_alpha_b, gla_norm_w,
           p_a, p_b, w_out, final_norm_w):
    B, T, D = x.shape
    L = norm_w.shape[0]
    assert B == 1 and D == D_MODEL and T % 512 == 0 and T // SEL_BLOCK <= NB_PAD
    assert w_in.shape[2] == SRC_END
    w_t = jnp.swapaxes(w_in, 1, 2)

    w1_b = cmp_w1.astype(BF16)
    w2_b = cmp_w2.astype(BF16)
    b1_r = cmp_b1.reshape(L, 2, 1, CMP_HIDDEN)
    b2_r = cmp_b2.reshape(L, 2, 1, A_DH)
    aw_p = jnp.pad(gla_alpha_w, ((0, 0), (0, LANES - B_RANK), (0, 0))).astype(BF16)
    ab_r = gla_alpha_b.reshape(L, 1, B_HEADS * B_DK)
    gnw_r = gla_norm_w.reshape(L, 1, B_DV)
    pa_b = p_a.astype(BF16)
    pb_b = p_b.astype(BF16)
    wo_b = w_out.astype(BF16)
    nw_next = jnp.concatenate([norm_w[1:], final_norm_w[None, :]], axis=0).reshape(L, 1, D)

    xs = x.reshape(T, D)
    h = _rmsnorm(xs, norm_w[0].reshape(1, D))
    for l in range(L):
        qa = _proj(h, w_t, l, SRC_Q_A, A_HEADS * A_DH, 512, BF16, scale=A_DH ** -0.5 * LOG2E)
        kv_cmp = _proj(h, w_t, l, SRC_KV_CMP, 2 * A_KVW, 512, F32)
        kvsw = _proj(h, w_t, l, SRC_KV_SW, 4 * A_KVW, 512, BF16)
        gbr = _proj(h, w_t, l, SRC_G_BR, LANES, LANES, BF16, valid=G_BR_COLS)
        cols2 = _proj(h, w_t, l, SRC_R2, SRC_A_LR - SRC_R2, 512, BF16)
        alr = _proj(h, w_t, l, SRC_A_LR, LANES, LANES, BF16, valid=B_RANK)
        cols3 = _proj(h, w_t, l, SRC_R3, SRC_END - SRC_R3, 512, BF16)
        kvc = _compress(kv_cmp, cmp_pe[l], w1_b[l], b1_r[l], w2_b[l], b2_r[l])
        o_cmp, qsel, o_win = _cmpwin_attn(qa, kvc, kvsw)
        o_sel = _sel_attn(qa, kvsw, qsel)
        o_b = _gla(cols2, alr, aw_p[l], ab_r[l], gnw_r[l])
        y = _merge(o_cmp, o_sel, o_win, gbr, cols2, o_b, cols3, pa_b, pb_b, l)
        if l == L - 1:
            xs = _outproj(xs, y, wo_b, l, nw_next[l], last=True)
        else:
            xs, h = _outproj(xs, y, wo_b, l, nw_next[l], last=False)
    return xs.reshape(B, T, D)
```

```python
import functools

import jax
import jax.numpy as jnp
import numpy as np
from jax import lax
from jax.experimental import pallas as pl
from jax.experimental.pallas import tpu as pltpu

F32 = jnp.float32
BF16 = jnp.bfloat16

D_MODEL = 2048
A_HEADS = 16
A_GROUPS = 4
A_HPG = A_HEADS // A_GROUPS
A_DH = 128
A_KVW = A_GROUPS * A_DH
CMP_BLOCK = 32
CMP_STRIDE = 16
CMP_HIDDEN = 256
SEL_BLOCK = 64
SEL_SHIFT = 6
SEL_TOP_N = 16
WIN_SIZE = 512
B_HEADS = 4
B_DK = 256
B_DV = 512
B_RANK = 16
B_GATE_TEMP = 16.0
B_CHUNK = 64
NORM_EPS = 1e-6

LANES = 128
NB_PAD = LANES
CMP_WIDTH_CLASSES = 4
MASK_BIG = 2.0 ** 100
LOG2E = float(np.log2(np.e))

SRC_Q_A = 0
SRC_KV_CMP = 2048
SRC_KV_SW = 3072
SRC_G_BR = 5120
SRC_R2 = 5168
SRC_A_LR = 11312
SRC_R3 = 11328
SRC_END = 17472
G_BR_COLS = 48
KVSW_K_SEL, KVSW_V_SEL, KVSW_K_WIN, KVSW_V_WIN = 0, 512, 1024, 1536
R2_Z_A, R2_Q_B, R2_K_B, R2_V_B = 0, 2048, 3072, 4096
R3_Z_B, R3_GATE_A, R3_GATE_B = 0, 2048, 4096

VMEM_LIMIT = 56 * 1024 * 1024


def _cparams(sem):
    return pltpu.CompilerParams(dimension_semantics=sem, vmem_limit_bytes=VMEM_LIMIT)


def _dot(a, b):
    return jnp.dot(a, b, preferred_element_type=F32)


def _dot_nt(a, b):
    return lax.dot_general(a, b, (((1,), (1,)), ((), ())), preferred_element_type=F32)


def _dot_tn(a, b):
    return lax.dot_general(a, b, (((0,), (0,)), ((), ())), preferred_element_type=F32)


def _split_bf16(x):
    hi = x.astype(BF16)
    lo = (x - hi.astype(F32)).astype(BF16)
    return hi, lo


def _rmsnorm_kernel(x_ref, w_ref, o_ref):
    x = x_ref[...]
    ms = jnp.mean(x * x, axis=-1, keepdims=True)
    o_ref[...] = (x * lax.rsqrt(ms + NORM_EPS) * w_ref[...]).astype(o_ref.dtype)


def _rmsnorm(x, w, tm=512):
    T, D = x.shape
    tm = min(tm, T)
    return pl.pallas_call(
        _rmsnorm_kernel,
        grid=(T // tm,),
        in_specs=[pl.BlockSpec((tm, D), lambda i: (i, 0)), pl.BlockSpec((1, D), lambda i: (0, 0))],
        out_specs=pl.BlockSpec((tm, D), lambda i: (i, 0)),
        out_shape=jax.ShapeDtypeStruct((T, D), BF16),
        compiler_params=_cparams(("parallel",)),
        name="rmsnorm",
    )(x, w)


def _proj_kernel(h_ref, *refs, shift, valid, scale):
    if shift:
        wa_ref, wb_ref, o_ref, wc_ref = refs
    else:
        wa_ref, o_ref, wc_ref = refs
    tn = o_ref.shape[1]

    @pl.when(pl.program_id(1) == 0)
    def _():
        if shift:
            wc_ref[0:tn - shift, :] = wa_ref[shift:tn, :].astype(BF16)
            wc_ref[tn - shift:tn, :] = wb_ref[0:shift, :].astype(BF16)
        else:
            wc_ref[...] = wa_ref[...].astype(BF16)
        if valid < tn:
            wc_ref[valid:tn, :] = jnp.zeros((tn - valid, wc_ref.shape[1]), BF16)

    r = _dot_nt(h_ref[...], wc_ref[...])
    if scale != 1.0:
        r = r * scale
    o_ref[...] = r.astype(o_ref.dtype)


def _proj(h, w_t, layer, col0, width, tn, out_dtype, scale=1.0, valid=None, tm=2048):
    T, D = h.shape
    tm = min(tm, T)
    shift = col0 % LANES
    a0 = col0 - shift
    valid = tn if valid is None else valid
    assert a0 % tn == 0 and width % tn == 0 and (valid == tn or width == tn)
    assert shift % 16 == 0 and valid % 16 == 0
    in_specs = [
        pl.BlockSpec((tm, D), lambda j, i: (i, 0)),
        pl.BlockSpec((None, tn, D), lambda j, i: (layer, a0 // tn + j, 0)),
    ]
    args = [h, w_t]
    if shift:
        in_specs.append(pl.BlockSpec((None, LANES, D), lambda j, i: (layer, (a0 + (j + 1) * tn) // LANES, 0)))
        args.append(w_t)
    return pl.pallas_call(
        functools.partial(_proj_kernel, shift=shift, valid=valid, scale=scale),
        grid=(width // tn, T // tm),
        in_specs=in_specs,
        out_specs=pl.BlockSpec((tm, tn), lambda j, i: (i, j)),
        out_shape=jax.ShapeDtypeStruct((T, width), out_dtype),
        scratch_shapes=[pltpu.VMEM((tn, D), BF16)],
        compiler_params=_cparams(("arbitrary", "arbitrary")),
        name="proj",
    )(*args)


def _gelu_tanh(x):
    c = np.float32(np.sqrt(2.0 / np.pi))
    return 0.5 * x * (1.0 + jnp.tanh(c * (x + np.float32(0.044715) * (x * x * x))))


def _compress_kernel(x_ref, pe_ref, w1_ref, b1_ref, w2_ref, b2_ref, o_ref, xlo_ref, xhi_ref):
    nch = o_ref.shape[2]
    half = CMP_STRIDE * A_DH
    for p in range(CMP_STRIDE):
        tok = x_ref[pl.ds(p, nch, stride=CMP_STRIDE), :]
        cs = slice(p * A_DH, (p + 1) * A_DH)
        xlo_ref[:, cs] = (tok + pe_ref[0, p:p + 1, :]).astype(BF16)
        xhi_ref[:, cs] = (tok + pe_ref[0, CMP_STRIDE + p:CMP_STRIDE + p + 1, :]).astype(BF16)
    a = _dot(xlo_ref[...], w1_ref[0, 0:half, :])
    b = _dot(xhi_ref[...], w1_ref[0, half:2 * half, :])
    b_next = pltpu.roll(b, nch - 1, axis=0)
    row = lax.broadcasted_iota(jnp.int32, b.shape, 0)
    b_next = jnp.where(row < nch - 1, b_next, 0.0)
    hid = _gelu_tanh(a + b_next + b1_ref[0])
    out = _dot(hid.astype(BF16), w2_ref[0]) + b2_ref[0]
    o_ref[0, 0] = out.astype(o_ref.dtype)


def _compress(kv, pe, w1, b1, w2, b2):
    T = kv.shape[0]
    G = A_GROUPS
    nch = T // CMP_STRIDE
    width = CMP_STRIDE * A_DH
    return pl.pallas_call(
        _compress_kernel,
        grid=(2, G),
        in_specs=[
            pl.BlockSpec((T, A_DH), lambda s, g: (0, s * G + g)),
            pl.BlockSpec((1, CMP_BLOCK, A_DH), lambda s, g: (s, 0, 0)),
            pl.BlockSpec((1, 2 * width, CMP_HIDDEN), lambda s, g: (s, 0, 0)),
            pl.BlockSpec((1, 1, CMP_HIDDEN), lambda s, g: (s, 0, 0)),
            pl.BlockSpec((1, CMP_HIDDEN, A_DH), lambda s, g: (s, 0, 0)),
            pl.BlockSpec((1, 1, A_DH), lambda s, g: (s, 0, 0)),
        ],
        out_specs=pl.BlockSpec((1, 1, nch, A_DH), lambda s, g: (s, g, 0, 0)),
        out_shape=jax.ShapeDtypeStruct((2, G, nch, A_DH), BF16),
        scratch_shapes=[pltpu.VMEM((nch, width), BF16), pltpu.VMEM((nch, width), BF16)],
        compiler_params=_cparams(("parallel", "parallel")),
        name="compress",
    )(kv, pe, w1, b1, w2, b2)


def _stack_heads(q_ref, qs_ref):
    tq = q_ref.shape[0]
    for h in range(A_HPG):
        qs_ref[h * tq:(h + 1) * tq, :] = q_ref[:, h * A_DH:(h + 1) * A_DH]


def _masked_chunks(s, masks, tq):
    out = []
    for c, mk in enumerate(masks):
        sc = s[:, c * LANES:(c + 1) * LANES]
        out.append(jnp.concatenate(
            [jnp.where(mk, sc[h * tq:(h + 1) * tq], -jnp.inf) for h in range(A_HPG)], axis=0))
    return out


def _chunk_softmax(sc):
    mx = sc[0]
    for x in sc[1:]:
        mx = jnp.maximum(mx, x)
    m = jnp.max(mx, axis=-1, keepdims=True)
    m = jnp.where(m > -jnp.inf, m, 0.0)
    pc = [jnp.exp2(x - m) for x in sc]
    ps = pc[0]
    for x in pc[1:]:
        ps = ps + x
    return pc, jnp.sum(ps, axis=-1, keepdims=True)


def _cmpwin_kernel(q_ref, kc_ref, vc_ref, k0_ref, k1_ref, k2_ref, v0_ref, v1_ref, v2_ref,
                   ocmp_ref, qsel_ref, owin_ref, qs_ref, imp_ref, p_ref, inv_ref):
    i = pl.program_id(1)

    @pl.when(i == 0)
    def _():
        imp_ref[1] = jnp.zeros(imp_ref.shape[1:], F32)
        p_ref[1] = jnp.zeros(p_ref.shape[1:], BF16)
        inv_ref[1] = jnp.ones(inv_ref.shape[1:], F32)

    def step(slot, nck):
        _stack_heads(q_ref, qs_ref)
        _cmp_attn_step(kc_ref, vc_ref, ocmp_ref, qsel_ref, qs_ref, imp_ref, slot, nck)
        _win_attn_step(k0_ref, k1_ref, k2_ref, v0_ref, v1_ref, v2_ref, owin_ref, qs_ref, p_ref, inv_ref, slot)

    tq = q_ref.shape[0]
    nch = kc_ref.shape[2]
    t_last = jnp.minimum(i, pl.num_programs(1) - 2) * tq + tq - 1
    need = (t_last - (CMP_BLOCK - 1)) // CMP_STRIDE + 1
    nw = min(CMP_WIDTH_CLASSES, nch // LANES)
    widths = [nch * (w + 1) // nw for w in range(nw)] if nch % (nw * LANES) == 0 else [nch]
    for slot in (0, 1):
        for wi, nck in enumerate(widths):
            fits = need <= nck
            if wi > 0:
                fits = fits & (need > widths[wi - 1])
            elif len(widths) == 1:
                fits = True
            pl.when((i % 2 == slot) & fits)(functools.partial(step, slot, nck))


def _cmp_attn_step(kc_ref, vc_ref, o_ref, qsel_ref, qs_ref, imp_ref, slot, nch):
    tq = qsel_ref.shape[0]
    i = pl.program_id(1)
    t0 = jnp.minimum(i, pl.num_programs(1) - 2) * tq
    s = _dot_nt(qs_ref[...], kc_ref[0, 0, 0:nch, :])
    tpos = t0 + lax.broadcasted_iota(jnp.int32, (tq, LANES), 0)
    lane = lax.broadcasted_iota(jnp.int32, (tq, LANES), 1)
    masks = [(lane + c * LANES) * CMP_STRIDE + (CMP_BLOCK - 1) <= tpos for c in range(nch // LANES)]
    pc, denom = _chunk_softmax(_masked_chunks(s, masks, tq))
    inv = 1.0 / jnp.where(denom > 0, denom, 1.0)
    o = _dot(jnp.concatenate([x.astype(BF16) for x in pc], axis=1), vc_ref[0, 0, 0:nch, :]) * inv
    for h in range(A_HPG):
        o_ref[:, h * A_DH:(h + 1) * A_DH] = o[h * tq:(h + 1) * tq].astype(o_ref.dtype)
    pn = [x * inv for x in pc]
    psum = jnp.concatenate(
        [sum(x[h * tq:(h + 1) * tq] for h in range(A_HPG)) for x in pn], axis=1)

    jj = lax.broadcasted_iota(jnp.int32, (NB_PAD, nch), 0)
    cs = lax.broadcasted_iota(jnp.int32, (NB_PAD, nch), 1) * CMP_STRIDE
    ov = ((cs <= jj * SEL_BLOCK + SEL_BLOCK - 1) & (cs + CMP_BLOCK - 1 >= jj * SEL_BLOCK))
    ov = jnp.where(ov, 1.0, 0.0).astype(BF16)
    p_hi, p_lo = _split_bf16(psum)
    imp_ref[slot] = _dot_nt(ov, p_hi) + _dot_nt(ov, p_lo)

    nr = min(NB_PAD, nch * CMP_STRIDE // SEL_BLOCK)
    pslc = imp_ref[1 - slot, 0:nr, :]
    jf = lax.broadcasted_iota(jnp.int32, (nr, tq), 0)
    tl = jnp.maximum(i - 1, 0) * tq + lax.broadcasted_iota(jnp.int32, (nr, tq), 1)
    cur = jnp.right_shift(tl, SEL_SHIFT)
    valid = jf <= cur
    forced = (jf == 0) | (jf == cur) | (jf == cur - 1)
    score = jnp.where(valid & ~forced, pslc, -jnp.inf)
    jff = jf.astype(F32)
    for _ in range(SEL_TOP_N - 3):
        mx = jnp.max(score, axis=0, keepdims=True)
        first = jnp.min(jnp.where(score == mx, jff, float(NB_PAD)), axis=0, keepdims=True)
        score = jnp.where(jff == first, -jnp.inf, score)
    neg = jnp.where(valid & (score == -jnp.inf), 0.0, -MASK_BIG)
    if nr < NB_PAD:
        neg = jnp.concatenate([neg, jnp.full((NB_PAD - nr, tq), -MASK_BIG, F32)], axis=0)
    qsel_ref[...] = neg.T.astype(qsel_ref.dtype)


def _win_attn_step(k0_ref, k1_ref, k2_ref, v0_ref, v1_ref, v2_ref, o_ref, qs_ref, p_ref, inv_ref, slot):
    tq = o_ref.shape[0]
    i = pl.program_id(1)
    ia = jnp.minimum(i, pl.num_programs(1) - 2)
    k = jnp.concatenate([k0_ref[...], k1_ref[...], k2_ref[...]], axis=0)
    s = _dot_nt(qs_ref[...], k)
    r = lax.broadcasted_iota(jnp.int32, (tq, LANES), 0)
    lane = lax.broadcasted_iota(jnp.int32, (tq, LANES), 1)
    masks = []
    for c in range(3 * tq // LANES):
        col = lane + c * LANES
        rel = 2 * tq + r - col
        masks.append((rel >= 0) & (rel < WIN_SIZE) & ((ia - 2) * tq + col >= 0))
    pc, denom = _chunk_softmax(_masked_chunks(s, masks, tq))

    v = jnp.concatenate([v0_ref[...], v1_ref[...], v2_ref[...]], axis=0)
    o = _dot(p_ref[1 - slot], v) * inv_ref[1 - slot]
    for h in range(A_HPG):
        o_ref[:, h * A_DH:(h + 1) * A_DH] = o[h * tq:(h + 1) * tq].astype(o_ref.dtype)

    p_ref[slot] = jnp.concatenate([x.astype(BF16) for x in pc], axis=1)
    inv_ref[slot] = jnp.broadcast_to(1.0 / denom, inv_ref.shape[1:])


def _cmpwin_attn(qa, kvc, kvsw, tq=256):
    T = qa.shape[0]
    G = A_GROUPS
    nch = kvc.shape[2]
    tq = min(tq, T)
    assert WIN_SIZE <= 2 * tq
    qw = A_HPG * A_DH
    n = T // tq
    cur = lambda g, i: (jnp.minimum(i, n - 1), g)
    prev = lambda g, i: (jnp.maximum(i - 1, 0), g)

    def kv_spec(col, back, lag):
        def f(g, i):
            tile = jnp.clip(i - lag, 0, n - 1)
            return (jnp.maximum(tile - back, 0), col // A_DH + g)
        return pl.BlockSpec((tq, A_DH), f)

    return pl.pallas_call(
        _cmpwin_kernel,
        grid=(G, n + 1),
        in_specs=[
            pl.BlockSpec((tq, qw), cur),
            pl.BlockSpec((1, 1, nch, A_DH), lambda g, i: (0, g, 0, 0)),
            pl.BlockSpec((1, 1, nch, A_DH), lambda g, i: (1, g, 0, 0)),
            kv_spec(KVSW_K_WIN, 2, 0), kv_spec(KVSW_K_WIN, 1, 0), kv_spec(KVSW_K_WIN, 0, 0),
            kv_spec(KVSW_V_WIN, 2, 1), kv_spec(KVSW_V_WIN, 1, 1), kv_spec(KVSW_V_WIN, 0, 1),
        ],
        out_specs=[
            pl.BlockSpec((tq, qw), cur),
            pl.BlockSpec((tq, NB_PAD), prev),
            pl.BlockSpec((tq, qw), prev),
        ],
        out_shape=[
            jax.ShapeDtypeStruct((T, A_HEADS * A_DH), BF16),
            jax.ShapeDtypeStruct((T, G * NB_PAD), BF16),
            jax.ShapeDtypeStruct((T, A_HEADS * A_DH), BF16),
        ],
        scratch_shapes=[pltpu.VMEM((A_HPG * tq, A_DH), BF16),
                        pltpu.VMEM((2, NB_PAD, tq), F32),
                        pltpu.VMEM((2, A_HPG * tq, 3 * tq), BF16),
                        pltpu.VMEM((2, A_HPG * tq, LANES), F32)],
        compiler_params=_cparams(("parallel", "arbitrary")),
        name="cmpwin_attn",
    )(qa, kvc, kvc, kvsw, kvsw, kvsw, kvsw, kvsw, kvsw)


def _sel_attn_kernel(q_ref, qsel_ref, k_ref, v_ref, oh_ref, o_ref, qa_ref, m_ref, l_ref, acc_ref, *, tk):
    tq = q_ref.shape[0]
    nsub = k_ref.shape[0] // tk
    i = pl.program_id(1)
    kp = pl.program_id(2)
    last_kt = (i * tq + tq - 1) // tk
    nlc = tk // LANES

    @pl.when(kp == 0)
    def _():
        for h in range(A_HPG):
            qa_ref[h * tq:(h + 1) * tq, 0:A_DH] = q_ref[:, h * A_DH:(h + 1) * A_DH]
            qa_ref[h * tq:(h + 1) * tq, A_DH:2 * A_DH] = qsel_ref[...]
        m_ref[...] = jnp.full(m_ref.shape, -jnp.inf, F32)
        l_ref[...] = jnp.zeros(l_ref.shape, F32)
        acc_ref[...] = jnp.zeros(acc_ref.shape, F32)

    def step(diagonal, kt, off):
        rows = pl.ds(off, tk)
        ka = jnp.concatenate([k_ref[rows, :], oh_ref[rows, :]], axis=1)
        s = _dot_nt(qa_ref[...], ka)
        sc = [s[:, c * LANES:(c + 1) * LANES] for c in range(nlc)]
        if diagonal:
            tpos = i * tq + lax.broadcasted_iota(jnp.int32, (tq, LANES), 0)
            lane = kt * tk + lax.broadcasted_iota(jnp.int32, (tq, LANES), 1)
            for c in range(nlc):
                causal = lane + c * LANES <= tpos
                sc[c] = jnp.concatenate(
                    [jnp.where(causal, sc[c][h * tq:(h + 1) * tq], -MASK_BIG) for h in range(A_HPG)], axis=0)
        mx = sc[0]
        for c in range(1, nlc):
            mx = jnp.maximum(mx, sc[c])
        m_old = m_ref[...]
        m_new = jnp.maximum(m_old, jnp.max(mx, axis=-1, keepdims=True))
        alpha = jnp.exp2(m_old - m_new)
        pc = [jnp.exp2(sc[c] - m_new) for c in range(nlc)]
        ps = pc[0]
        for c in range(1, nlc):
            ps = ps + pc[c]
        l_ref[...] = alpha * l_ref[...] + ps
        p = jnp.concatenate([x.astype(BF16) for x in pc], axis=1)
        acc_ref[...] = alpha * acc_ref[...] + _dot(p, v_ref[rows, :])
        m_ref[...] = m_new

    all_interior = kp * nsub + nsub - 1 < last_kt

    @pl.when(all_interior)
    def _():
        for sub in range(nsub):
            step(False, kp * nsub + sub, sub * tk)

    d = last_kt - kp * nsub
    for dd in range(nsub):
        @pl.when(d == dd)
        def _(dd=dd):
            for sub in range(dd):
                step(False, kp * nsub + sub, sub * tk)
            step(True, kp * nsub + dd, dd * tk)

    @pl.when(kp == pl.num_programs(2) - 1)
    def _():
        for h in range(A_HPG):
            l = jnp.sum(l_ref[h * tq:(h + 1) * tq], axis=-1, keepdims=True)
            o = acc_ref[h * tq:(h + 1) * tq] / l
            o_ref[:, h * A_DH:(h + 1) * A_DH] = o.astype(o_ref.dtype)


def _sel_attn(qa, kvsw, qsel, tq=512, tk=512, kb=2048):
    T = qa.shape[0]
    G = A_GROUPS
    tq = min(tq, T)
    tk = min(tk, T)
    kb = min(kb, T)
    qw = A_HPG * A_DH
    rows = A_HPG * tq

    def kv_map(col, per_group):
        def f(g, i, kp):
            return (jnp.minimum(kp, (i * tq + tq - 1) // kb), col // A_DH + g * per_group)
        return f

    onehot = jnp.asarray(np.arange(T)[:, None] // SEL_BLOCK == np.arange(NB_PAD)[None, :], dtype=BF16)

    return pl.pallas_call(
        functools.partial(_sel_attn_kernel, tk=tk),
        grid=(G, T // tq, T // kb),
        in_specs=[
            pl.BlockSpec((tq, qw), lambda g, i, kp: (i, g)),
            pl.BlockSpec((tq, NB_PAD), lambda g, i, kp: (i, g)),
            pl.BlockSpec((kb, A_DH), kv_map(KVSW_K_SEL, 1)),
            pl.BlockSpec((kb, A_DH), kv_map(KVSW_V_SEL, 1)),
            pl.BlockSpec((kb, NB_PAD), kv_map(0, 0)),
        ],
        out_specs=pl.BlockSpec((tq, qw), lambda g, i, kp: (i, g)),
        out_shape=jax.ShapeDtypeStruct((T, A_HEADS * A_DH), BF16),
        scratch_shapes=[
            pltpu.VMEM((rows, 2 * A_DH), BF16),
            pltpu.VMEM((rows, LANES), F32),
            pltpu.VMEM((rows, LANES), F32),
            pltpu.VMEM((rows, A_DH), F32),
        ],
        compiler_params=_cparams(("parallel", "parallel", "arbitrary")),
        name="sel_attn",
    )(qa, qsel, kvsw, kvsw, onehot)


def _gla_kernel(q_ref, k_ref, v_ref, a_ref, aw_ref, ab_ref, nw_ref, o_ref, st_ref):
    tc = q_ref.shape[0]
    C = B_CHUNK

    @pl.when(pl.program_id(0) == 0)
    def _():
        st_ref[...] = jnp.zeros(st_ref.shape, F32)

    logits = _dot(a_ref[...], aw_ref[...]) + ab_ref[...]
    log_a = jax.nn.log_sigmoid(logits) * (1.0 / B_GATE_TEMP)
    ri = lax.broadcasted_iota(jnp.int32, (C, C), 0)
    ci = lax.broadcasted_iota(jnp.int32, (C, C), 1)
    tril = ci <= ri
    tril_b = jnp.where(tril, 1.0, 0.0).astype(BF16)
    nw = nw_ref[...]
    for c in range(tc // C):
        sl = slice(c * C, (c + 1) * C)
        la_hi, la_lo = _split_bf16(log_a[sl])
        cum = _dot(tril_b, la_hi) + _dot(tril_b, la_lo)
        last = cum[C - 1:C, :]
        qc = q_ref[sl, :].astype(F32) * (B_DK ** -0.5)
        kc = k_ref[sl, :].astype(F32)
        q_dec = (qc * jnp.exp(cum)).astype(BF16)
        k_inv = (kc * jnp.exp(-cum)).astype(BF16)
        k_state = (kc * jnp.exp(last - cum)).astype(BF16)
        dec = jnp.exp(last)
        for h in range(B_HEADS):
            ks = slice(h * B_DK, (h + 1) * B_DK)
            vs = slice(h * B_DV, (h + 1) * B_DV)
            vc = v_ref[sl, vs]
            att = jnp.where(tril, _dot_nt(q_dec[:, ks], k_inv[:, ks]), 0.0)
            st = st_ref[h]
            o = _dot(att.astype(BF16), vc) + _dot_nt(q_dec[:, ks], st.astype(BF16))
            st_ref[h] = st * dec[:, ks] + _dot_tn(vc, k_state[:, ks])
            o = o * lax.rsqrt(jnp.mean(o * o, axis=-1, keepdims=True) + NORM_EPS) * nw
            o_ref[sl, vs] = o.astype(o_ref.dtype)


def _gla(cols2, alr, aw, ab, nw, tc=512):
    T = cols2.shape[0]
    tc = min(tc, T)
    kw = B_HEADS * B_DK
    vw = B_HEADS * B_DV
    return pl.pallas_call(
        _gla_kernel,
        grid=(T // tc,),
        in_specs=[
            pl.BlockSpec((tc, kw), lambda i: (i, R2_Q_B // kw)),
            pl.BlockSpec((tc, kw), lambda i: (i, R2_K_B // kw)),
            pl.BlockSpec((tc, vw), lambda i: (i, R2_V_B // vw)),
            pl.BlockSpec((tc, LANES), lambda i: (i, 0)),
            pl.BlockSpec((LANES, kw), lambda i: (0, 0)),
            pl.BlockSpec((1, kw), lambda i: (0, 0)),
            pl.BlockSpec((1, B_DV), lambda i: (0, 0)),
        ],
        out_specs=pl.BlockSpec((tc, vw), lambda i: (i, 0)),
        out_shape=jax.ShapeDtypeStruct((T, vw), BF16),
        scratch_shapes=[pltpu.VMEM((B_HEADS, B_DV, B_DK), F32)],
        compiler_params=_cparams(("arbitrary",)),
        name="gla",
    )(cols2, cols2, cols2, alr, aw, ab, nw)


def _silu(x):
    return x * jax.nn.sigmoid(x)


def _merge_a_kernel(ocmp_ref, osel_ref, owin_ref, gbr_ref, za_ref, ga_ref, pa_ref, y_ref, ua_ref):
    i = pl.program_id(0)

    @pl.when(i == 0)
    def _():
        ua_ref[1] = jnp.zeros(ua_ref.shape[1:], BF16)

    def step(slot):
        sg = jax.nn.sigmoid(gbr_ref[...].astype(F32))
        for hh in range(A_HEADS):
            cs = slice(hh * A_DH, (hh + 1) * A_DH)
            oa = (sg[:, 3 * hh:3 * hh + 1] * ocmp_ref[:, cs].astype(F32)
                  + sg[:, 3 * hh + 1:3 * hh + 2] * osel_ref[:, cs].astype(F32)
                  + sg[:, 3 * hh + 2:3 * hh + 3] * owin_ref[:, cs].astype(F32))
            ua_ref[slot, :, cs] = (oa * _silu(za_ref[:, cs].astype(F32))).astype(BF16)
        ya = _dot(ua_ref[1 - slot], pa_ref[...])
        y_ref[...] = (jax.nn.sigmoid(ga_ref[...].astype(F32)) * ya).astype(y_ref.dtype)

    for slot in (0, 1):
        pl.when(i % 2 == slot)(functools.partial(step, slot))


def _merge_b_kernel(ya_ref, ob_ref, zb_ref, gb_ref, pb_ref, y_ref):
    ub = (ob_ref[...].astype(F32) * _silu(zb_ref[...].astype(F32))).astype(BF16)
    yb = _dot(ub, pb_ref[...])
    y = ya_ref[...].astype(F32) + jax.nn.sigmoid(gb_ref[...].astype(F32)) * yb
    y_ref[...] = y.astype(y_ref.dtype)


def _merge(o_cmp, o_sel, o_win, gbr, cols2, o_b, cols3, pa, pb, layer, tm=512):
    T = gbr.shape[0]
    D = D_MODEL
    tm = min(tm, T)
    row = lambda c: pl.BlockSpec((tm, D), lambda i: (i, c))
    const = lambda: pl.BlockSpec((None, D, D), lambda i: (layer, 0, 0), pipeline_mode=pl.Buffered(1))
    n = T // tm
    cur = lambda c: pl.BlockSpec((tm, D), lambda i: (jnp.minimum(i, n - 1), c))
    prev = lambda c: pl.BlockSpec((tm, D), lambda i: (jnp.maximum(i - 1, 0), c))
    ya = pl.pallas_call(
        _merge_a_kernel,
        grid=(n + 1,),
        in_specs=[cur(0), cur(0), cur(0),
                  pl.BlockSpec((tm, LANES), lambda i: (jnp.minimum(i, n - 1), 0)),
                  cur(R2_Z_A // D), prev(R3_GATE_A // D), const()],
        out_specs=prev(0),
        out_shape=jax.ShapeDtypeStruct((T, D), BF16),
        scratch_shapes=[pltpu.VMEM((2, tm, D), BF16)],
        compiler_params=_cparams(("arbitrary",)),
        name="merge_a",
    )(o_cmp, o_sel, o_win, gbr, cols2, cols3, pa)
    return pl.pallas_call(
        _merge_b_kernel,
        grid=(T // tm,),
        in_specs=[row(0), row(0), row(R3_Z_B // D), row(R3_GATE_B // D), const()],
        out_specs=row(0),
        out_shape=jax.ShapeDtypeStruct((T, D), BF16),
        compiler_params=_cparams(("parallel",)),
        name="merge_b",
    )(ya, o_b, cols3, cols3, pb)


def _outproj_kernel(x_ref, y_ref, w_ref, nw_ref, *out_refs, last):
    xn = x_ref[...] + _dot(y_ref[...], w_ref[...])
    ms = jnp.mean(xn * xn, axis=-1, keepdims=True)
    hn = xn * lax.rsqrt(ms + NORM_EPS) * nw_ref[...]
    if last:
        out_refs[0][...] = hn
    else:
        out_refs[0][...] = xn
        out_refs[1][...] = hn.astype(BF16)


def _outproj(x, y, w, layer, nw, last, tm=512):
    T, D = x.shape
    tm = min(tm, T)
    row = pl.BlockSpec((tm, D), lambda i: (i, 0))
    if last:
        out_specs, out_shape = row, jax.ShapeDtypeStruct((T, D), F32)
    else:
        out_specs = [row, row]
        out_shape = [jax.ShapeDtypeStruct((T, D), F32), jax.ShapeDtypeStruct((T, D), BF16)]
    return pl.pallas_call(
        functools.partial(_outproj_kernel, last=last),
        grid=(T // tm,),
        in_specs=[
            row, row,
            pl.BlockSpec((None, D, D), lambda i: (layer, 0, 0), pipeline_mode=pl.Buffered(1)),
            pl.BlockSpec((1, D), lambda i: (0, 0)),
        ],
        out_specs=out_specs,
        out_shape=out_shape,
        compiler_params=_cparams(("parallel",)),
        name="outproj",
    )(x, y, w, nw)


def kernel(x, norm_w, w_in, cmp_pe, cmp_w1, cmp_b1, cmp_w2, cmp_b2, gla_alpha_w, gla_alpha_b, gla_norm_w,
           p_a, p_b, w_out, final_norm_w):
    B, T, D = x.shape
    L = norm_w.shape[0]
    assert B == 1 and D == D_MODEL and T % 512 == 0 and T // SEL_BLOCK <= NB_PAD
    assert w_in.shape[2] == SRC_END
    w_t = jnp.swapaxes(w_in, 1, 2)

    w1_b = cmp_w1.astype(BF16)
    w2_b = cmp_w2.astype(BF16)
    b1_r = cmp_b1.reshape(L, 2, 1, CMP_HIDDEN)
    b2_r = cmp_b2.reshape(L, 2, 1, A_DH)
    aw_p = jnp.pad(gla_alpha_w, ((0, 0), (0, LANES - B_RANK), (0, 0))).astype(BF16)
    ab_r = gla_alpha_b.reshape(L, 1, B_HEADS * B_DK)
    gnw_r = gla_norm_w.reshape(L, 1, B_DV)
    pa_b = p_a.astype(BF16)
    pb_b = p_b.astype(BF16)
    wo_b = w_out.astype(BF16)
    nw_next = jnp.concatenate([norm_w[1:], final_norm_w[None, :]], axis=0).reshape(L, 1, D)

    xs = x.reshape(T, D)
    h = _rmsnorm(xs, norm_w[0].reshape(1, D))
    for l in range(L):
        qa = _proj(h, w_t, l, SRC_Q_A, A_HEADS * A_DH, 512, BF16, scale=A_DH ** -0.5 * LOG2E)
        kv_cmp = _proj(h, w_t, l, SRC_KV_CMP, 2 * A_KVW, 512, F32)
        kvsw = _proj(h, w_t, l, SRC_KV_SW, 4 * A_KVW, 512, BF16)
        gbr = _proj(h, w_t, l, SRC_G_BR, LANES, LANES, BF16, valid=G_BR_COLS)
        cols2 = _proj(h, w_t, l, SRC_R2, SRC_A_LR - SRC_R2, 512, BF16)
        alr = _proj(h, w_t, l, SRC_A_LR, LANES, LANES, BF16, valid=B_RANK)
        cols3 = _proj(h, w_t, l, SRC_R3, SRC_END - SRC_R3, 512, BF16)
        kvc = _compress(kv_cmp, cmp_pe[l], w1_b[l], b1_r[l], w2_b[l], b2_r[l])
        o_cmp, qsel, o_win = _cmpwin_attn(qa, kvc, kvsw)
        o_sel = _sel_attn(qa, kvsw, qsel)
        o_b = _gla(cols2, alr, aw_p[l], ab_r[l], gnw_r[l])
        y = _merge(o_cmp, o_sel, o_win, gbr, cols2, o_b, cols3, pa_b, pb_b, l)
        if l == L - 1:
            xs = _outproj(xs, y, wo_b, l, nw_next[l], last=True)
        else:
            xs, h = _outproj(xs, y, wo_b, l, nw_next[l], last=False)
    return xs.reshape(B, T, D)
```

```python
import functools

import jax
import jax.numpy as jnp
import numpy as np
from jax import lax
from jax.experimental import pallas as pl
from jax.experimental.pallas import tpu as pltpu

F32 = jnp.float32
BF16 = jnp.bfloat16

D_MODEL = 2048
A_HEADS = 16
A_GROUPS = 4
A_HPG = A_HEADS // A_GROUPS
A_DH = 128
A_KVW = A_GROUPS * A_DH
CMP_BLOCK = 32
CMP_STRIDE = 16
CMP_HIDDEN = 256
SEL_BLOCK = 64
SEL_SHIFT = 6
SEL_TOP_N = 16
WIN_SIZE = 512
B_HEADS = 4
B_DK = 256
B_DV = 512
B_RANK = 16
B_GATE_TEMP = 16.0
B_CHUNK = 64
NORM_EPS = 1e-6

LANES = 128
NB_PAD = LANES
CMP_WIDTH_CLASSES = 4
MASK_BIG = 2.0 ** 100
LOG2E = float(np.log2(np.e))

SRC_Q_A = 0
SRC_KV_CMP = 2048
SRC_KV_SW = 3072
SRC_G_BR = 5120
SRC_R2 = 5168
SRC_A_LR = 11312
SRC_R3 = 11328
SRC_END = 17472
G_BR_COLS = 48
KVSW_K_SEL, KVSW_V_SEL, KVSW_K_WIN, KVSW_V_WIN = 0, 512, 1024, 1536
R2_Z_A, R2_Q_B, R2_K_B, R2_V_B = 0, 2048, 3072, 4096
R3_Z_B, R3_GATE_A, R3_GATE_B = 0, 2048, 4096

VMEM_LIMIT = 56 * 1024 * 1024


def _cparams(sem):
    return pltpu.CompilerParams(dimension_semantics=sem, vmem_limit_bytes=VMEM_LIMIT)


def _dot(a, b):
    return jnp.dot(a, b, preferred_element_type=F32)


def _dot_nt(a, b):
    return lax.dot_general(a, b, (((1,), (1,)), ((), ())), preferred_element_type=F32)


def _dot_tn(a, b):
    return lax.dot_general(a, b, (((0,), (0,)), ((), ())), preferred_element_type=F32)


def _split_bf16(x):
    hi = x.astype(BF16)
    lo = (x - hi.astype(F32)).astype(BF16)
    return hi, lo


def _rmsnorm_kernel(x_ref, w_ref, o_ref):
    x = x_ref[...]
    ms = jnp.mean(x * x, axis=-1, keepdims=True)
    o_ref[...] = (x * lax.rsqrt(ms + NORM_EPS) * w_ref[...]).astype(o_ref.dtype)


def _rmsnorm(x, w, tm=512):
    T, D = x.shape
    tm = min(tm, T)
    return pl.pallas_call(
        _rmsnorm_kernel,
        grid=(T // tm,),
        in_specs=[pl.BlockSpec((tm, D), lambda i: (i, 0)), pl.BlockSpec((1, D), lambda i: (0, 0))],
        out_specs=pl.BlockSpec((tm, D), lambda i: (i, 0)),
        out_shape=jax.ShapeDtypeStruct((T, D), BF16),
        compiler_params=_cparams(("parallel",)),
        name="rmsnorm",
    )(x, w)


def _proj_kernel(h_ref, *refs, shift, valid, scale):
    if shift:
        wa_ref, wb_ref, o_ref, wc_ref = refs
    else:
        wa_ref, o_ref, wc_ref = refs
    tn = o_ref.shape[1]

    @pl.when(pl.program_id(1) == 0)
    def _():
        if shift:
            wc_ref[0:tn - shift, :] = wa_ref[shift:tn, :].astype(BF16)
            wc_ref[tn - shift:tn, :] = wb_ref[0:shift, :].astype(BF16)
        else:
            wc_ref[...] = wa_ref[...].astype(BF16)
        if valid < tn:
            wc_ref[valid:tn, :] = jnp.zeros((tn - valid, wc_ref.shape[1]), BF16)

    r = _dot_nt(h_ref[...], wc_ref[...])
    if scale != 1.0:
        r = r * scale
    o_ref[...] = r.astype(o_ref.dtype)


def _proj(h, w_t, layer, col0, width, tn, out_dtype, scale=1.0, valid=None, tm=2048):
    T, D = h.shape
    tm = min(tm, T)
    shift = col0 % LANES
    a0 = col0 - shift
    valid = tn if valid is None else valid
    assert a0 % tn == 0 and width % tn == 0 and (valid == tn or width == tn)
    assert shift % 16 == 0 and valid % 16 == 0
    in_specs = [
        pl.BlockSpec((tm, D), lambda j, i: (i, 0)),
        pl.BlockSpec((None, tn, D), lambda j, i: (layer, a0 // tn + j, 0)),
    ]
    args = [h, w_t]
    if shift:
        in_specs.append(pl.BlockSpec((None, LANES, D), lambda j, i: (layer, (a0 + (j + 1) * tn) // LANES, 0)))
        args.append(w_t)
    return pl.pallas_call(
        functools.partial(_proj_kernel, shift=shift, valid=valid, scale=scale),
        grid=(width // tn, T // tm),
        in_specs=in_specs,
        out_specs=pl.BlockSpec((tm, tn), lambda j, i: (i, j)),
        out_shape=jax.ShapeDtypeStruct((T, width), out_dtype),
        scratch_shapes=[pltpu.VMEM((tn, D), BF16)],
        compiler_params=_cparams(("arbitrary", "arbitrary")),
        name="proj",
    )(*args)


def _gelu_tanh(x):
    c = np.float32(np.sqrt(2.0 / np.pi))
    return 0.5 * x * (1.0 + jnp.tanh(c * (x + np.float32(0.044715) * (x * x * x))))


def _compress_kernel(x_ref, pe_ref, w1_ref, b1_ref, w2_ref, b2_ref, o_ref, xlo_ref, xhi_ref):
    nch = o_ref.shape[2]
    half = CMP_STRIDE * A_DH
    for p in range(CMP_STRIDE):
        tok = x_ref[pl.ds(p, nch, stride=CMP_STRIDE), :]
        cs = slice(p * A_DH, (p + 1) * A_DH)
        xlo_ref[:, cs] = (tok + pe_ref[0, p:p + 1, :]).astype(BF16)
        xhi_ref[:, cs] = (tok + pe_ref[0, CMP_STRIDE + p:CMP_STRIDE + p + 1, :]).astype(BF16)
    a = _dot(xlo_ref[...], w1_ref[0, 0:half, :])
    b = _dot(xhi_ref[...], w1_ref[0, half:2 * half, :])
    b_next = pltpu.roll(b, nch - 1, axis=0)
    row = lax.broadcasted_iota(jnp.int32, b.shape, 0)
    b_next = jnp.where(row < nch - 1, b_next, 0.0)
    hid = _gelu_tanh(a + b_next + b1_ref[0])
    out = _dot(hid.astype(BF16), w2_ref[0]) + b2_ref[0]
    o_ref[0, 0] = out.astype(o_ref.dtype)


def _compress(kv, pe, w1, b1, w2, b2):
    T = kv.shape[0]
    G = A_GROUPS
    nch = T // CMP_STRIDE
    width = CMP_STRIDE * A_DH
    return pl.pallas_call(
        _compress_kernel,
        grid=(2, G),
        in_specs=[
            pl.BlockSpec((T, A_DH), lambda s, g: (0, s * G + g)),
            pl.BlockSpec((1, CMP_BLOCK, A_DH), lambda s, g: (s, 0, 0)),
            pl.BlockSpec((1, 2 * width, CMP_HIDDEN), lambda s, g: (s, 0, 0)),
            pl.BlockSpec((1, 1, CMP_HIDDEN), lambda s, g: (s, 0, 0)),
            pl.BlockSpec((1, CMP_HIDDEN, A_DH), lambda s, g: (s, 0, 0)),
            pl.BlockSpec((1, 1, A_DH), lambda s, g: (s, 0, 0)),
        ],
        out_specs=pl.BlockSpec((1, 1, nch, A_DH), lambda s, g: (s, g, 0, 0)),
        out_shape=jax.ShapeDtypeStruct((2, G, nch, A_DH), BF16),
        scratch_shapes=[pltpu.VMEM((nch, width), BF16), pltpu.VMEM((nch, width), BF16)],
        compiler_params=_cparams(("parallel", "parallel")),
        name="compress",
    )(kv, pe, w1, b1, w2, b2)


def _stack_heads(q_ref, qs_ref):
    tq = q_ref.shape[0]
    for h in range(A_HPG):
        qs_ref[h * tq:(h + 1) * tq, :] = q_ref[:, h * A_DH:(h + 1) * A_DH]


def _masked_chunks(s, masks, tq):
    out = []
    for c, mk in enumerate(masks):
        sc = s[:, c * LANES:(c + 1) * LANES]
        out.append(jnp.concatenate(
            [jnp.where(mk, sc[h * tq:(h + 1) * tq], -jnp.inf) for h in range(A_HPG)], axis=0))
    return out


def _chunk_softmax(sc):
    mx = sc[0]
    for x in sc[1:]:
        mx = jnp.maximum(mx, x)
    m = jnp.max(mx, axis=-1, keepdims=True)
    m = jnp.where(m > -jnp.inf, m, 0.0)
    pc = [jnp.exp2(x - m) for x in sc]
    ps = pc[0]
    for x in pc[1:]:
        ps = ps + x
    return pc, jnp.sum(ps, axis=-1, keepdims=True)


def _cmpwin_kernel(q_ref, kc_ref, vc_ref, k0_ref, k1_ref, k2_ref, v0_ref, v1_ref, v2_ref,
                   ocmp_ref, qsel_ref, owin_ref, qs_ref, imp_ref, p_ref, inv_ref):
    i = pl.program_id(1)

    @pl.when(i == 0)
    def _():
        imp_ref[1] = jnp.zeros(imp_ref.shape[1:], F32)
        p_ref[1] = jnp.zeros(p_ref.shape[1:], BF16)
        inv_ref[1] = jnp.ones(inv_ref.shape[1:], F32)

    def step(slot, nck):
        _stack_heads(q_ref, qs_ref)
        _cmp_attn_step(kc_ref, vc_ref, ocmp_ref, qsel_ref, qs_ref, imp_ref, slot, nck)
        _win_attn_step(k0_ref, k1_ref, k2_ref, v0_ref, v1_ref, v2_ref, owin_ref, qs_ref, p_ref, inv_ref, slot)

    tq = q_ref.shape[0]
    nch = kc_ref.shape[2]
    t_last = jnp.minimum(i, pl.num_programs(1) - 2) * tq + tq - 1
    need = (t_last - (CMP_BLOCK - 1)) // CMP_STRIDE + 1
    nw = min(CMP_WIDTH_CLASSES, nch // LANES)
    widths = [nch * (w + 1) // nw for w in range(nw)] if nch % (nw * LANES) == 0 else [nch]
    for slot in (0, 1):
        for wi, nck in enumerate(widths):
            fits = need <= nck
            if wi > 0:
                fits = fits & (need > widths[wi - 1])
            elif len(widths) == 1:
                fits = True
            pl.when((i % 2 == slot) & fits)(functools.partial(step, slot, nck))


def _cmp_attn_step(kc_ref, vc_ref, o_ref, qsel_ref, qs_ref, imp_ref, slot, nch):
    tq = qsel_ref.shape[0]
    i = pl.program_id(1)
    t0 = jnp.minimum(i, pl.num_programs(1) - 2) * tq
    s = _dot_nt(qs_ref[...], kc_ref[0, 0, 0:nch, :])
    tpos = t0 + lax.broadcasted_iota(jnp.int32, (tq, LANES), 0)
    lane = lax.broadcasted_iota(jnp.int32, (tq, LANES), 1)
    masks = [(lane + c * LANES) * CMP_STRIDE + (CMP_BLOCK - 1) <= tpos for c in range(nch // LANES)]
    pc, denom = _chunk_softmax(_masked_chunks(s, masks, tq))
    inv = 1.0 / jnp.where(denom > 0, denom, 1.0)
    o = _dot(jnp.concatenate([x.astype(BF16) for x in pc], axis=1), vc_ref[0, 0, 0:nch, :]) * inv
    for h in range(A_HPG):
        o_ref[:, h * A_DH:(h + 1) * A_DH] = o[h * tq:(h + 1) * tq].astype(o_ref.dtype)
    pn = [x * inv for x in pc]
    psum = jnp.concatenate(
        [sum(x[h * tq:(h + 1) * tq] for h in range(A_HPG)) for x in pn], axis=1)

    jj = lax.broadcasted_iota(jnp.int32, (NB_PAD, nch), 0)
    cs = lax.broadcasted_iota(jnp.int32, (NB_PAD, nch), 1) * CMP_STRIDE
    ov = ((cs <= jj * SEL_BLOCK + SEL_BLOCK - 1) & (cs + CMP_BLOCK - 1 >= jj * SEL_BLOCK))
    ov = jnp.where(ov, 1.0, 0.0).astype(BF16)
    p_hi, p_lo = _split_bf16(psum)
    imp_ref[slot] = _dot_nt(ov, p_hi) + _dot_nt(ov, p_lo)

    nr = min(NB_PAD, nch * CMP_STRIDE // SEL_BLOCK)
    pslc = imp_ref[1 - slot, 0:nr, :]
    jf = lax.broadcasted_iota(jnp.int32, (nr, tq), 0)
    tl = jnp.maximum(i - 1, 0) * tq + lax.broadcasted_iota(jnp.int32, (nr, tq), 1)
    cur = jnp.right_shift(tl, SEL_SHIFT)
    valid = jf <= cur
    forced = (jf == 0) | (jf == cur) | (jf == cur - 1)
    score = jnp.where(valid & ~forced, pslc, -jnp.inf)
    jff = jf.astype(F32)
    for _ in range(SEL_TOP_N - 3):
        mx = jnp.max(score, axis=0, keepdims=True)
        first = jnp.min(jnp.where(score == mx, jff, float(NB_PAD)), axis=0, keepdims=True)
        score = jnp.where(jff == first, -jnp.inf, score)
    neg = jnp.where(valid & (score == -jnp.inf), 0.0, -MASK_BIG)
    if nr < NB_PAD:
        neg = jnp.concatenate([neg, jnp.full((NB_PAD - nr, tq), -MASK_BIG, F32)], axis=0)
    qsel_ref[...] = neg.T.astype(qsel_ref.dtype)


def _win_attn_step(k0_ref, k1_ref, k2_ref, v0_ref, v1_ref, v2_ref, o_ref, qs_ref, p_ref, inv_ref, slot):
    tq = o_ref.shape[0]
    i = pl.program_id(1)
    ia = jnp.minimum(i, pl.num_programs(1) - 2)
    k = jnp.concatenate([k0_ref[...], k1_ref[...], k2_ref[...]], axis=0)
    s = _dot_nt(qs_ref[...], k)
    r = lax.broadcasted_iota(jnp.int32, (tq, LANES), 0)
    lane = lax.broadcasted_iota(jnp.int32, (tq, LANES), 1)
    masks = []
    for c in range(3 * tq // LANES):
        col = lane + c * LANES
        rel = 2 * tq + r - col
        masks.append((rel >= 0) & (rel < WIN_SIZE) & ((ia - 2) * tq + col >= 0))
    pc, denom = _chunk_softmax(_masked_chunks(s, masks, tq))

    v = jnp.concatenate([v0_ref[...], v1_ref[...], v2_ref[...]], axis=0)
    o = _dot(p_ref[1 - slot], v) * inv_ref[1 - slot]
    for h in range(A_HPG):
        o_ref[:, h * A_DH:(h + 1) * A_DH] = o[h * tq:(h + 1) * tq].astype(o_ref.dtype)

    p_ref[slot] = jnp.concatenate([x.astype(BF16) for x in pc], axis=1)
    inv_ref[slot] = jnp.broadcast_to(1.0 / denom, inv_ref.shape[1:])


def _cmpwin_attn(qa, kvc, kvsw, tq=256):
    T = qa.shape[0]
    G = A_GROUPS
    nch = kvc.shape[2]
    tq = min(tq, T)
    assert WIN_SIZE <= 2 * tq
    qw = A_HPG * A_DH
    n = T // tq
    cur = lambda g, i: (jnp.minimum(i, n - 1), g)
    prev = lambda g, i: (jnp.maximum(i - 1, 0), g)

    def kv_spec(col, back, lag):
        def f(g, i):
            tile = jnp.clip(i - lag, 0, n - 1)
            return (jnp.maximum(tile - back, 0), col // A_DH + g)
        return pl.BlockSpec((tq, A_DH), f)

    return pl.pallas_call(
        _cmpwin_kernel,
        grid=(G, n + 1),
        in_specs=[
            pl.BlockSpec((tq, qw), cur),
            pl.BlockSpec((1, 1, nch, A_DH), lambda g, i: (0, g, 0, 0)),
            pl.BlockSpec((1, 1, nch, A_DH), lambda g, i: (1, g, 0, 0)),
            kv_spec(KVSW_K_WIN, 2, 0), kv_spec(KVSW_K_WIN, 1, 0), kv_spec(KVSW_K_WIN, 0, 0),
            kv_spec(KVSW_V_WIN, 2, 1), kv_spec(KVSW_V_WIN, 1, 1), kv_spec(KVSW_V_WIN, 0, 1),
        ],
        out_specs=[
            pl.BlockSpec((tq, qw), cur),
            pl.BlockSpec((tq, NB_PAD), prev),
            pl.BlockSpec((tq, qw), prev),
        ],
        out_shape=[
            jax.ShapeDtypeStruct((T, A_HEADS * A_DH), BF16),
            jax.ShapeDtypeStruct((T, G * NB_PAD), BF16),
            jax.ShapeDtypeStruct((T, A_HEADS * A_DH), BF16),
        ],
        scratch_shapes=[pltpu.VMEM((A_HPG * tq, A_DH), BF16),
                        pltpu.VMEM((2, NB_PAD, tq), F32),
                        pltpu.VMEM((2, A_HPG * tq, 3 * tq), BF16),
                        pltpu.VMEM((2, A_HPG * tq, LANES), F32)],
        compiler_params=_cparams(("parallel", "arbitrary")),
        name="cmpwin_attn",
    )(qa, kvc, kvc, kvsw, kvsw, kvsw, kvsw, kvsw, kvsw)


def _sel_attn_kernel(q_ref, qsel_ref, k_ref, v_ref, oh_ref, o_ref, qa_ref, m_ref, l_ref, acc_ref, *, tk):
    tq = q_ref.shape[0]
    nsub = k_ref.shape[0] // tk
    i = pl.program_id(1)
    kp = pl.program_id(2)
    last_kt = (i * tq + tq - 1) // tk
    nlc = tk // LANES

    @pl.when(kp == 0)
    def _():
        for h in range(A_HPG):
            qa_ref[h * tq:(h + 1) * tq, 0:A_DH] = q_ref[:, h * A_DH:(h + 1) * A_DH]
            qa_ref[h * tq:(h + 1) * tq, A_DH:2 * A_DH] = qsel_ref[...]
        m_ref[...] = jnp.full(m_ref.shape, -jnp.inf, F32)
        l_ref[...] = jnp.zeros(l_ref.shape, F32)
        acc_ref[...] = jnp.zeros(acc_ref.shape, F32)

    def step(diagonal, kt, off):
        rows = pl.ds(off, tk)
        ka = jnp.concatenate([k_ref[rows, :], oh_ref[rows, :]], axis=1)
        s = _dot_nt(qa_ref[...], ka)
        sc = [s[:, c * LANES:(c + 1) * LANES] for c in range(nlc)]
        if diagonal:
            tpos = i * tq + lax.broadcasted_iota(jnp.int32, (tq, LANES), 0)
            lane = kt * tk + lax.broadcasted_iota(jnp.int32, (tq, LANES), 1)
            for c in range(nlc):
                causal = lane + c * LANES <= tpos
                sc[c] = jnp.concatenate(
                    [jnp.where(causal, sc[c][h * tq:(h + 1) * tq], -MASK_BIG) for h in range(A_HPG)], axis=0)
        mx = sc[0]
        for c in range(1, nlc):
            mx = jnp.maximum(mx, sc[c])
        m_old = m_ref[...]
        m_new = jnp.maximum(m_old, jnp.max(mx, axis=-1, keepdims=True))
        alpha = jnp.exp2(m_old - m_new)
        pc = [jnp.exp2(sc[c] - m_new) for c in range(nlc)]
        ps = pc[0]
        for c in range(1, nlc):
            ps = ps + pc[c]
        l_ref[...] = alpha * l_ref[...] + ps
        p = jnp.concatenate([x.astype(BF16) for x in pc], axis=1)
        acc_ref[...] = alpha * acc_ref[...] + _dot(p, v_ref[rows, :])
        m_ref[...] = m_new

    all_interior = kp * nsub + nsub - 1 < last_kt

    @pl.when(all_interior)
    def _():
        for sub in range(nsub):
            step(False, kp * nsub + sub, sub * tk)

    d = last_kt - kp * nsub
    for dd in range(nsub):
        @pl.when(d == dd)
        def _(dd=dd):
            for sub in range(dd):
                step(False, kp * nsub + sub, sub * tk)
            step(True, kp * nsub + dd, dd * tk)

    @pl.when(kp == pl.num_programs(2) - 1)
    def _():
        for h in range(A_HPG):
            l = jnp.sum(l_ref[h * tq:(h + 1) * tq], axis=-1, keepdims=True)
            o = acc_ref[h * tq:(h + 1) * tq] / l
            o_ref[:, h * A_DH:(h + 1) * A_DH] = o.astype(o_ref.dtype)


def _sel_attn(qa, kvsw, qsel, tq=512, tk=512, kb=2048):
    T = qa.shape[0]
    G = A_GROUPS
    tq = min(tq, T)
    tk = min(tk, T)
    kb = min(kb, T)
    qw = A_HPG * A_DH
    rows = A_HPG * tq

    def kv_map(col, per_group):
        def f(g, i, kp):
            return (jnp.minimum(kp, (i * tq + tq - 1) // kb), col // A_DH + g * per_group)
        return f

    onehot = jnp.asarray(np.arange(T)[:, None] // SEL_BLOCK == np.arange(NB_PAD)[None, :], dtype=BF16)

    return pl.pallas_call(
        functools.partial(_sel_attn_kernel, tk=tk),
        grid=(G, T // tq, T // kb),
        in_specs=[
            pl.BlockSpec((tq, qw), lambda g, i, kp: (i, g)),
            pl.BlockSpec((tq, NB_PAD), lambda g, i, kp: (i, g)),
            pl.BlockSpec((kb, A_DH), kv_map(KVSW_K_SEL, 1)),
            pl.BlockSpec((kb, A_DH), kv_map(KVSW_V_SEL, 1)),
            pl.BlockSpec((kb, NB_PAD), kv_map(0, 0)),
        ],
        out_specs=pl.BlockSpec((tq, qw), lambda g, i, kp: (i, g)),
        out_shape=jax.ShapeDtypeStruct((T, A_HEADS * A_DH), BF16),
        scratch_shapes=[
            pltpu.VMEM((rows, 2 * A_DH), BF16),
            pltpu.VMEM((rows, LANES), F32),
            pltpu.VMEM((rows, LANES), F32),
            pltpu.VMEM((rows, A_DH), F32),
        ],
        compiler_params=_cparams(("parallel", "parallel", "arbitrary")),
        name="sel_attn",
    )(qa, qsel, kvsw, kvsw, onehot)


def _gla_kernel(q_ref, k_ref, v_ref, a_ref, aw_ref, ab_ref, nw_ref, o_ref, st_ref):
    tc = q_ref.shape[0]
    C = B_CHUNK

    @pl.when(pl.program_id(0) == 0)
    def _():
        st_ref[...] = jnp.zeros(st_ref.shape, F32)

    logits = _dot(a_ref[...], aw_ref[...]) + ab_ref[...]
    log_a = jax.nn.log_sigmoid(logits) * (1.0 / B_GATE_TEMP)
    ri = lax.broadcasted_iota(jnp.int32, (C, C), 0)
    ci = lax.broadcasted_iota(jnp.int32, (C, C), 1)
    tril = ci <= ri
    tril_b = jnp.where(tril, 1.0, 0.0).astype(BF16)
    nw = nw_ref[...]
    for c in range(tc // C):
        sl = slice(c * C, (c + 1) * C)
        la_hi, la_lo = _split_bf16(log_a[sl])
        cum = _dot(tril_b, la_hi) + _dot(tril_b, la_lo)
        last = cum[C - 1:C, :]
        qc = q_ref[sl, :].astype(F32) * (B_DK ** -0.5)
        kc = k_ref[sl, :].astype(F32)
        q_dec = (qc * jnp.exp(cum)).astype(BF16)
        k_inv = (kc * jnp.exp(-cum)).astype(BF16)
        k_state = (kc * jnp.exp(last - cum)).astype(BF16)
        dec = jnp.exp(last)
        for h in range(B_HEADS):
            ks = slice(h * B_DK, (h + 1) * B_DK)
            vs = slice(h * B_DV, (h + 1) * B_DV)
            vc = v_ref[sl, vs]
            att = jnp.where(tril, _dot_nt(q_dec[:, ks], k_inv[:, ks]), 0.0)
            st = st_ref[h]
            o = _dot(att.astype(BF16), vc) + _dot_nt(q_dec[:, ks], st.astype(BF16))
            st_ref[h] = st * dec[:, ks] + _dot_tn(vc, k_state[:, ks])
            o = o * lax.rsqrt(jnp.mean(o * o, axis=-1, keepdims=True) + NORM_EPS) * nw
            o_ref[sl, vs] = o.astype(o_ref.dtype)


def _gla(cols2, alr, aw, ab, nw, tc=512):
    T = cols2.shape[0]
    tc = min(tc, T)
    kw = B_HEADS * B_DK
    vw = B_HEADS * B_DV
    return pl.pallas_call(
        _gla_kernel,
        grid=(T // tc,),
        in_specs=[
            pl.BlockSpec((tc, kw), lambda i: (i, R2_Q_B // kw)),
            pl.BlockSpec((tc, kw), lambda i: (i, R2_K_B // kw)),
            pl.BlockSpec((tc, vw), lambda i: (i, R2_V_B // vw)),
            pl.BlockSpec((tc, LANES), lambda i: (i, 0)),
            pl.BlockSpec((LANES, kw), lambda i: (0, 0)),
            pl.BlockSpec((1, kw), lambda i: (0, 0)),
            pl.BlockSpec((1, B_DV), lambda i: (0, 0)),
        ],
        out_specs=pl.BlockSpec((tc, vw), lambda i: (i, 0)),
        out_shape=jax.ShapeDtypeStruct((T, vw), BF16),
        scratch_shapes=[pltpu.VMEM((B_HEADS, B_DV, B_DK), F32)],
        compiler_params=_cparams(("arbitrary",)),
        name="gla",
    )(cols2, cols2, cols2, alr, aw, ab, nw)


def _silu(x):
    return x * jax.nn.sigmoid(x)


def _merge_a_kernel(ocmp_ref, osel_ref, owin_ref, gbr_ref, za_ref, ga_ref, pa_ref, y_ref, ua_ref):
    i = pl.program_id(0)

    @pl.when(i == 0)
    def _():
        ua_ref[1] = jnp.zeros(ua_ref.shape[1:], BF16)

    def step(slot):
        sg = jax.nn.sigmoid(gbr_ref[...].astype(F32))
        for hh in range(A_HEADS):
            cs = slice(hh * A_DH, (hh + 1) * A_DH)
            oa = (sg[:, 3 * hh:3 * hh + 1] * ocmp_ref[:, cs].astype(F32)
                  + sg[:, 3 * hh + 1:3 * hh + 2] * osel_ref[:, cs].astype(F32)
                  + sg[:, 3 * hh + 2:3 * hh + 3] * owin_ref[:, cs].astype(F32))
            ua_ref[slot, :, cs] = (oa * _silu(za_ref[:, cs].astype(F32))).astype(BF16)
        ya = _dot(ua_ref[1 - slot], pa_ref[...])
        y_ref[...] = (jax.nn.sigmoid(ga_ref[...].astype(F32)) * ya).astype(y_ref.dtype)

    for slot in (0, 1):
        pl.when(i % 2 == slot)(functools.partial(step, slot))


def _merge(o_cmp, o_sel, o_win, gbr, cols2, cols3, pa, layer, tm=512):
    T = gbr.shape[0]
    D = D_MODEL
    tm = min(tm, T)
    const = lambda: pl.BlockSpec((None, D, D), lambda i: (layer, 0, 0), pipeline_mode=pl.Buffered(1))
    n = T // tm
    cur = lambda c: pl.BlockSpec((tm, D), lambda i: (jnp.minimum(i, n - 1), c))
    prev = lambda c: pl.BlockSpec((tm, D), lambda i: (jnp.maximum(i - 1, 0), c))
    return pl.pallas_call(
        _merge_a_kernel,
        grid=(n + 1,),
        in_specs=[cur(0), cur(0), cur(0),
                  pl.BlockSpec((tm, LANES), lambda i: (jnp.minimum(i, n - 1), 0)),
                  cur(R2_Z_A // D), prev(R3_GATE_A // D), const()],
        out_specs=prev(0),
        out_shape=jax.ShapeDtypeStruct((T, D), BF16),
        scratch_shapes=[pltpu.VMEM((2, tm, D), BF16)],
        compiler_params=_cparams(("arbitrary",)),
        name="merge_a",
    )(o_cmp, o_sel, o_win, gbr, cols2, cols3, pa)


def _outproj_kernel(x_ref, ya_ref, ob_ref, zb_ref, gb_ref, pb_ref, w_ref, nw_ref, *out_refs, last):
    ub = (ob_ref[...].astype(F32) * _silu(zb_ref[...].astype(F32))).astype(BF16)
    yb = _dot(ub, pb_ref[...])
    y = (ya_ref[...].astype(F32) + jax.nn.sigmoid(gb_ref[...].astype(F32)) * yb).astype(BF16)
    xn = x_ref[...] + _dot(y, w_ref[...])
    ms = jnp.mean(xn * xn, axis=-1, keepdims=True)
    hn = xn * lax.rsqrt(ms + NORM_EPS) * nw_ref[...]
    if last:
        out_refs[0][...] = hn
    else:
        out_refs[0][...] = xn
        out_refs[1][...] = hn.astype(BF16)


def _outproj(x, ya, o_b, cols3, pb, w, layer, nw, last, tm=256):
    T, D = x.shape
    tm = min(tm, T)
    row = pl.BlockSpec((tm, D), lambda i: (i, 0))
    col = lambda c: pl.BlockSpec((tm, D), lambda i: (i, c))
    const = lambda: pl.BlockSpec((None, D, D), lambda i: (layer, 0, 0), pipeline_mode=pl.Buffered(1))
    if last:
        out_specs, out_shape = row, jax.ShapeDtypeStruct((T, D), F32)
    else:
        out_specs = [row, row]
        out_shape = [jax.ShapeDtypeStruct((T, D), F32), jax.ShapeDtypeStruct((T, D), BF16)]
    return pl.pallas_call(
        functools.partial(_outproj_kernel, last=last),
        grid=(T // tm,),
        in_specs=[
            row, row, row, col(R3_Z_B // D), col(R3_GATE_B // D), const(), const(),
            pl.BlockSpec((1, D), lambda i: (0, 0)),
        ],
        out_specs=out_specs,
        out_shape=out_shape,
        compiler_params=_cparams(("parallel",)),
        name="outproj",
    )(x, ya, o_b, cols3, cols3, pb, w, nw)


def kernel(x, norm_w, w_in, cmp_pe, cmp_w1, cmp_b1, cmp_w2, cmp_b2, gla_alpha_w, gla_alpha_b, gla_norm_w,
           p_a, p_b, w_out, final_norm_w):
    B, T, D = x.shape
    L = norm_w.shape[0]
    assert B == 1 and D == D_MODEL and T % 512 == 0 and T // SEL_BLOCK <= NB_PAD
    assert w_in.shape[2] == SRC_END
    w_t = jnp.swapaxes(w_in, 1, 2)

    w1_b = cmp_w1.astype(BF16)
    w2_b = cmp_w2.astype(BF16)
    b1_r = cmp_b1.reshape(L, 2, 1, CMP_HIDDEN)
    b2_r = cmp_b2.reshape(L, 2, 1, A_DH)
    aw_p = jnp.pad(gla_alpha_w, ((0, 0), (0, LANES - B_RANK), (0, 0))).astype(BF16)
    ab_r = gla_alpha_b.reshape(L, 1, B_HEADS * B_DK)
    gnw_r = gla_norm_w.reshape(L, 1, B_DV)
    pa_b = p_a.astype(BF16)
    pb_b = p_b.astype(BF16)
    wo_b = w_out.astype(BF16)
    nw_next = jnp.concatenate([norm_w[1:], final_norm_w[None, :]], axis=0).reshape(L, 1, D)

    xs = x.reshape(T, D)
    h = _rmsnorm(xs, norm_w[0].reshape(1, D))
    for l in range(L):
        qa = _proj(h, w_t, l, SRC_Q_A, A_HEADS * A_DH, 512, BF16, scale=A_DH ** -0.5 * LOG2E)
        kv_cmp = _proj(h, w_t, l, SRC_KV_CMP, 2 * A_KVW, 512, F32)
        kvsw = _proj(h, w_t, l, SRC_KV_SW, 4 * A_KVW, 512, BF16)
        gbr = _proj(h, w_t, l, SRC_G_BR, LANES, LANES, BF16, valid=G_BR_COLS)
        cols2 = _proj(h, w_t, l, SRC_R2, SRC_A_LR - SRC_R2, 512, BF16)
        alr = _proj(h, w_t, l, SRC_A_LR, LANES, LANES, BF16, valid=B_RANK)
        cols3 = _proj(h, w_t, l, SRC_R3, SRC_END - SRC_R3, 512, BF16)
        kvc = _compress(kv_cmp, cmp_pe[l], w1_b[l], b1_r[l], w2_b[l], b2_r[l])
        o_cmp, qsel, o_win = _cmpwin_attn(qa, kvc, kvsw)
        o_sel = _sel_attn(qa, kvsw, qsel)
        o_b = _gla(cols2, alr, aw_p[l], ab_r[l], gnw_r[l])
        ya = _merge(o_cmp, o_sel, o_win, gbr, cols2, cols3, pa_b, l)
        if l == L - 1:
            xs = _outproj(xs, ya, o_b, cols3, pb_b, wo_b, l, nw_next[l], last=True)
        else:
            xs, h = _outproj(xs, ya, o_b, cols3, pb_b, wo_b, l, nw_next[l], last=False)
    return xs.reshape(B, T, D)
```
